```python
import jax, jax.numpy as jnp
from jax import lax
import numpy as np

D_MODEL = 1024
BATCH = 2
SEQ = 8192
DEPTH = 2
DEC_BATCH = 4
DEC_SEQ = 8192
PAST_LEN = 128

GRID_W = 64
D_FF = 2816
EPS = 1e-6
ROPE_THETA = 10000.0

GDN_H = 4
GDN_DK = 128
GDN_DV = 256
GDN_QK = GDN_H * GDN_DK
GDN_V = GDN_H * GDN_DV
GDN_CONV_W = 5
GDN_CHUNK = 64

RET_H = 4
RET_DK = 128
RET_DV = 256
RET_QK = RET_H * RET_DK
RET_V = RET_H * RET_DV
RET_CHUNK = 64

ATT_HQ = 8
ATT_HKV = 2
ATT_HD = 128
ATT_GROUP = ATT_HQ // ATT_HKV
ATT_Q = ATT_HQ * ATT_HD
ATT_KV = ATT_HKV * ATT_HD
Q_BLOCK = 128

N_BRANCH = 3
GDN_CONV_CH = 2 * GDN_QK + GDN_V
IN_SIZES = (GDN_CONV_CH, GDN_V, 2 * GDN_H, 2 * GDN_H,
            RET_QK, RET_QK, RET_V, RET_V,
            ATT_Q, ATT_KV, ATT_KV,
            N_BRANCH * D_MODEL)
N_IN = sum(IN_SIZES)

kernel_name = "hybrid_bidir_gdn_retention_axial_gqa_encoder"


def _split_points(sizes):
    pts, acc = [], 0
    for n in sizes[:-1]:
        acc += n
        pts.append(acc)
    return pts


def rms_norm(x, gain):
    xf = x.astype(jnp.float32)
    y = xf * lax.rsqrt(jnp.mean(xf * xf, axis=-1, keepdims=True) + EPS)
    return (y * gain.astype(jnp.float32)).astype(x.dtype)


def l2_norm(x):
    return x * lax.rsqrt(jnp.sum(x * x, axis=-1, keepdims=True) + EPS)


def swiglu(x, w1, w3, w2):
    return (jax.nn.silu(x @ w1) * (x @ w3)) @ w2


def rope_angles(pos, dim):
    inv = ROPE_THETA ** (-jnp.arange(0, dim, 2, dtype=jnp.float32) / dim)
    ang = pos.astype(jnp.float32)[:, None] * inv[None, :]
    return jnp.cos(ang), jnp.sin(ang)


def apply_rope(x, cos, sin):
    x1, x2 = jnp.split(x, 2, axis=-1)
    c = cos[None, :, None, :].astype(x.dtype)
    s = sin[None, :, None, :].astype(x.dtype)
    return jnp.concatenate([x1 * c - x2 * s, x1 * s + x2 * c], axis=-1)


def apply_axial_rope(x, rows, cols):
    half = x.shape[-1] // 2
    cr, sr = rope_angles(rows, half)
    cc, sc = rope_angles(cols, half)
    return jnp.concatenate([apply_rope(x[..., :half], cr, sr),
                            apply_rope(x[..., half:], cc, sc)], axis=-1)


def flip(t):
    return t[:, ::-1]


def to_chunks(x, c):
    b, s, h = x.shape[:3]
    y = x.reshape((b, s // c, c, h) + x.shape[3:])
    return jnp.moveaxis(y, 3, 1)


def from_chunks(y):
    b, h, n, c, d = y.shape
    return jnp.moveaxis(y, 1, 3).reshape(b, n * c, h, d)


def gated_delta_chunked(q, k, v, beta, g):
    c = GDN_CHUNK
    q, k, v = to_chunks(q, c), to_chunks(k, c), to_chunks(v, c)
    beta, g = to_chunks(beta, c), to_chunks(g, c)
    gc = jnp.cumsum(g, axis=-1)
    tril = jnp.tril(jnp.ones((c, c), dtype=bool))
    strict = jnp.tril(jnp.ones((c, c), dtype=bool), -1)
    decay = jnp.exp(jnp.where(tril, gc[..., :, None] - gc[..., None, :], -jnp.inf))
    kb = k * beta[..., None]
    lower = jnp.where(strict, jnp.einsum('bhnid,bhnjd->bhnij', kb, k) * decay, 0.0)
    a_mat = lower + jnp.eye(c, dtype=lower.dtype)
    rhs = jnp.concatenate([v * beta[..., None], kb * jnp.exp(gc)[..., None]], axis=-1)
    sol = lax.linalg.triangular_solve(a_mat, rhs, left_side=True, lower=True, unit_diagonal=True)
    u, w = sol[..., :GDN_DV], sol[..., GDN_DV:]
    qk = jnp.where(tril, jnp.einsum('bhnid,bhnjd->bhnij', q, k) * decay, 0.0)
    q_dec = q * jnp.exp(gc)[..., None]
    k_dec = k * jnp.exp(gc[..., -1:] - gc)[..., None]
    g_tot = jnp.exp(gc[..., -1])
    xs = tuple(jnp.moveaxis(t, 2, 0) for t in (qk, q_dec, k_dec, u, w, g_tot))

    def step(state, inp):
        qk_c, qd_c, kd_c, u_c, w_c, gt_c = inp
        v_new = u_c - jnp.einsum('bhcd,bhde->bhce', w_c, state)
        o = jnp.einsum('bhcd,bhde->bhce', qd_c, state) + jnp.einsum('bhij,bhje->bhie', qk_c, v_new)
        state = state * gt_c[..., None, None] + jnp.einsum('bhcd,bhce->bhde', kd_c, v_new)
        return state, o

    b, h = q.shape[:2]
    s0 = jnp.zeros((b, h, GDN_DK, GDN_DV), jnp.float32)
    _, o = lax.scan(step, s0, xs)
    return from_chunks(jnp.moveaxis(o, 0, 2))


def retention_chunked(q, k, v, log_gamma):
    c = RET_CHUNK
    q, k, v = to_chunks(q, c), to_chunks(k, c), to_chunks(v, c)
    idx = jnp.arange(c, dtype=jnp.float32)
    rel = idx[:, None] - idx[None, :]
    dmat = jnp.exp(jnp.where(rel >= 0, rel * log_gamma[:, None, None], -jnp.inf))
    intra = jnp.einsum('bhnid,bhnjd->bhnij', q, k) * dmat[:, None]
    o_intra = jnp.einsum('bhnij,bhnje->bhnie', intra, v)
    xi = jnp.exp((idx + 1.0)[None, :] * log_gamma[:, None])
    zeta = jnp.exp((c - 1.0 - idx)[None, :] * log_gamma[:, None])
    g_chunk = jnp.exp(c * log_gamma)
    qx = q * xi[:, None, :, None]
    kz = k * zeta[:, None, :, None]
    xs = (jnp.moveaxis(qx, 2, 0), jnp.moveaxis(kz, 2, 0), jnp.moveaxis(v, 2, 0))

    def step(r, inp):
        qc, kc, vc = inp
        o = jnp.einsum('bhcd,bhde->bhce', qc, r)
        r = r * g_chunk[:, None, None] + jnp.einsum('bhcd,bhce->bhde', kc, vc)
        return r, o

    b, h = q.shape[:2]
    r0 = jnp.zeros((b, h, RET_DK, RET_DV), jnp.float32)
    _, o_cross = lax.scan(step, r0, xs)
    return from_chunks(o_intra + jnp.moveaxis(o_cross, 0, 2))


def block_attention(q, k, v):
    b, s = q.shape[:2]
    nblk = s // Q_BLOCK
    qb = q.reshape(b, nblk, Q_BLOCK, ATT_HKV, ATT_GROUP, ATT_HD).transpose(1, 0, 3, 4, 2, 5)
    scale = ATT_HD ** -0.5

    def one_block(qi):
        sc = jnp.einsum('bkgqd,bskd->bkgqs', qi, k, preferred_element_type=jnp.float32) * scale
        p = jax.nn.softmax(sc, axis=-1).astype(v.dtype)
        return jnp.einsum('bkgqs,bskd->bkgqd', p, v)

    o = lax.map(one_block, qb)
    return o.transpose(1, 0, 4, 2, 3, 5).reshape(b, s, ATT_Q)


def token_mixer(u, lp):
    b, s, _ = u.shape
    dt = u.dtype
    f32 = jnp.float32
    proj = u @ lp['w_in']
    (gdn_qkv, gdn_z, gdn_b, gdn_a, r_q, r_k, r_v, r_g,
     a_q, a_k, a_v, gates) = jnp.split(proj, _split_points(IN_SIZES), axis=-1)

    conv = lax.conv_general_dilated(
        gdn_qkv, lp['gdn_conv'][:, None, :], window_strides=(1,),
        padding=[(GDN_CONV_W // 2, GDN_CONV_W // 2)],
        dimension_numbers=('NWC', 'WIO', 'NWC'), feature_group_count=GDN_CONV_CH)
    conv = jax.nn.silu(conv).astype(f32)
    gq, gk, gv = jnp.split(conv, [GDN_QK, 2 * GDN_QK], axis=-1)
    gq = l2_norm(gq.reshape(b, s, GDN_H, GDN_DK)) * (GDN_DK ** -0.5)
    gk = l2_norm(gk.reshape(b, s, GDN_H, GDN_DK))
    gv = gv.reshape(b, s, GDN_H, GDN_DV)
    beta = jax.nn.sigmoid(gdn_b.astype(f32)).reshape(b, s, 2, GDN_H)
    logdec = -jnp.exp(lp['gdn_A_log'].astype(f32)) * jax.nn.softplus(
        gdn_a.astype(f32).reshape(b, s, 2, GDN_H) + lp['gdn_dt_bias'].astype(f32))
    go = (gated_delta_chunked(gq, gk, gv, beta[:, :, 0], logdec[:, :, 0])
          + flip(gated_delta_chunked(flip(gq), flip(gk), flip(gv),
                                     flip(beta[:, :, 1]), flip(logdec[:, :, 1]))))
    go = go * lax.rsqrt(jnp.mean(go * go, axis=-1, keepdims=True) + EPS) * lp['gdn_norm'].astype(f32)
    go = go.astype(dt) * jax.nn.silu(gdn_z.reshape(b, s, GDN_H, GDN_DV))
    branch_a = go.reshape(b, s, GDN_V) @ lp['w_branch_gdn']

    cos, sin = rope_angles(jnp.arange(s), RET_DK)
    rq = apply_rope(r_q.astype(f32).reshape(b, s, RET_H, RET_DK), cos, sin)
    rk = apply_rope(r_k.astype(f32).reshape(b, s, RET_H, RET_DK), cos, sin) * (RET_DK ** -0.5)
    rv = r_v.astype(f32).reshape(b, s, RET_H, RET_DV)
    lg = jax.nn.log_sigmoid(lp['ret_decay_logit'].astype(f32))
    ro = (retention_chunked(rq, rk, rv, lg[0])
          + flip(retention_chunked(flip(rq), flip(rk), flip(rv), lg[1])))
    mu = jnp.mean(ro, axis=-1, keepdims=True)
    var = jnp.mean(jnp.square(ro - mu), axis=-1, keepdims=True)
    ro = ((ro - mu) * lax.rsqrt(var + EPS)).reshape(b, s, RET_V) * lp['ret_norm'].astype(f32)
    ro = jax.nn.silu(r_g) * ro.astype(dt)
    branch_b = ro @ lp['w_branch_ret']

    n_rows = s // GRID_W
    rows = jnp.repeat(jnp.arange(n_rows), GRID_W)
    cols = jnp.tile(jnp.arange(GRID_W), n_rows)
    aq = apply_axial_rope(rms_norm(a_q.reshape(b, s, ATT_HQ, ATT_HD), lp['attn_q_norm']), rows, cols)
    ak = apply_axial_rope(rms_norm(a_k.reshape(b, s, ATT_HKV, ATT_HD), lp['attn_k_norm']), rows, cols)
    av = a_v.reshape(b, s, ATT_HKV, ATT_HD)
    branch_c = block_attention(aq, ak, av) @ lp['w_branch_attn']

    gt = jax.nn.sigmoid(gates.astype(f32)).astype(dt).reshape(b, s, N_BRANCH, D_MODEL)
    merged = gt[:, :, 0] * branch_a + gt[:, :, 1] * branch_b + gt[:, :, 2] * branch_c
    return merged @ lp['w_out']


def encoder_layer(x, lp):
    x = x + 0.5 * swiglu(rms_norm(x, lp['ffn1_norm']), lp['ffn1_w1'], lp['ffn1_w3'], lp['ffn1_w2'])
    x = x + token_mixer(rms_norm(x, lp['mix_norm']), lp)
    x = x + 0.5 * swiglu(rms_norm(x, lp['ffn2_norm']), lp['ffn2_w1'], lp['ffn2_w3'], lp['ffn2_w2'])
    return x


def setup_inputs(seed: int = 0) -> dict:
    key = jax.random.key(seed)
    ks = jax.random.split(key, 32)
    f32 = jnp.float32
    L, D = DEPTH, D_MODEL

    def nrm(k, shape, fan_in):
        return jax.random.normal(k, shape, f32) * (fan_in ** -0.5)

    def gain(k, shape):
        return 1.0 + 0.02 * jax.random.normal(k, shape, f32)

    dt0 = jnp.exp(jax.random.uniform(ks[8], (L, 2, GDN_H), f32, np.log(1e-3), np.log(1e-1)))
    gamma0 = 1.0 - 2.0 ** (-5.0 - jnp.arange(RET_H, dtype=f32))
    logit0 = jnp.log(gamma0) - jnp.log1p(-gamma0)
    return {
        "x_prompt": jax.random.normal(ks[0], (BATCH, SEQ, D), f32),
        "x_sample": jax.random.normal(ks[1], (DEC_BATCH, DEC_SEQ, D), f32),
        "ffn1_norm": gain(ks[2], (L, D)),
        "ffn1_w1": nrm(ks[3], (L, D, D_FF), D),
        "ffn1_w3": nrm(ks[4], (L, D, D_FF), D),
        "ffn1_w2": nrm(ks[5], (L, D_FF, D), D_FF),
        "mix_norm": gain(ks[6], (L, D)),
        "w_in": nrm(ks[7], (L, D, N_IN), D),
        "gdn_conv": nrm(ks[9], (L, GDN_CONV_W, GDN_CONV_CH), GDN_CONV_W),
        "gdn_A_log": jnp.log(jax.random.uniform(ks[10], (L, 2, GDN_H), f32, 1.0, 16.0)),
        "gdn_dt_bias": dt0 + jnp.log(-jnp.expm1(-dt0)),
        "gdn_norm": gain(ks[11], (L, GDN_DV)),
        "ret_decay_logit": logit0 + 0.1 * jax.random.normal(ks[12], (L, 2, RET_H), f32),
        "ret_norm": gain(ks[13], (L, RET_V)),
        "attn_q_norm": gain(ks[14], (L, ATT_HD)),
        "attn_k_norm": gain(ks[15], (L, ATT_HD)),
        "w_branch_gdn": nrm(ks[16], (L, GDN_V, D), GDN_V),
        "w_branch_ret": nrm(ks[17], (L, RET_V, D), RET_V),
        "w_branch_attn": nrm(ks[18], (L, ATT_Q, D), ATT_Q),
        "w_out": nrm(ks[19], (L, D, D), D),
        "ffn2_norm": gain(ks[20], (L, D)),
        "ffn2_w1": nrm(ks[21], (L, D, D_FF), D),
        "ffn2_w3": nrm(ks[22], (L, D, D_FF), D),
        "ffn2_w2": nrm(ks[23], (L, D_FF, D), D_FF),
    }


def reference(x_prompt, x_sample, ffn1_norm, ffn1_w1, ffn1_w3, ffn1_w2, mix_norm, w_in,
              gdn_conv, gdn_A_log, gdn_dt_bias, gdn_norm, ret_decay_logit, ret_norm,
              attn_q_norm, attn_k_norm, w_branch_gdn, w_branch_ret, w_branch_attn, w_out,
              ffn2_norm, ffn2_w1, ffn2_w3, ffn2_w2):
    layers = []
    for l in range(DEPTH):
        layers.append(dict(
            ffn1_norm=ffn1_norm[l], ffn1_w1=ffn1_w1[l], ffn1_w3=ffn1_w3[l], ffn1_w2=ffn1_w2[l],
            mix_norm=mix_norm[l], w_in=w_in[l], gdn_conv=gdn_conv[l], gdn_A_log=gdn_A_log[l],
            gdn_dt_bias=gdn_dt_bias[l], gdn_norm=gdn_norm[l], ret_decay_logit=ret_decay_logit[l],
            ret_norm=ret_norm[l], attn_q_norm=attn_q_norm[l], attn_k_norm=attn_k_norm[l],
            w_branch_gdn=w_branch_gdn[l], w_branch_ret=w_branch_ret[l],
            w_branch_attn=w_branch_attn[l], w_out=w_out[l],
            ffn2_norm=ffn2_norm[l], ffn2_w1=ffn2_w1[l], ffn2_w3=ffn2_w3[l], ffn2_w2=ffn2_w2[l]))

    y_prompt = x_prompt
    for l in range(DEPTH):
        y_prompt = encoder_layer(y_prompt, layers[l])
    y_sample = x_sample
    for l in range(DEPTH):
        y_sample = encoder_layer(y_sample, layers[l])
    return (y_prompt, y_sample)
```

```python
import functools
import math

import jax
import jax.numpy as jnp
from jax import lax
from jax.experimental import pallas as pl
from jax.experimental.pallas import tpu as pltpu

F32 = jnp.float32
BF16 = jnp.bfloat16

D_MODEL = 1024
D_FF = 2816
EPS = 1e-6
ROPE_THETA = 10000.0
GRID_W = 64

N_HEAD = 4
DK = 128
DV = 256
CONV_W = 5
CHUNK = 64
ATT_HQ = 8
ATT_HKV = 2
ATT_HD = 128
ATT_GROUP = ATT_HQ // ATT_HKV

C_GQKV = 0
C_GZ = 2048
C_GATE = 3072
C_RQ = 6144
C_RK = 6656
C_RV = 7168
C_RG = 8192
C_AQ = 9216
C_AK = 10240
C_AV = 10496
N_PROJ = 10752

VMEM_LIMIT = 56 * 1024 * 1024

TM_FFN = 512
FF_CHUNKS = ((0, 768), (768, 1536), (1536, 2304), (2304, 2816))
TM_PROJ = 1024
TN_PROJ = 1536
TM_GDN = 256
TR_RET = 256
TM_ATT_PREP = 256
TQ_ATT = 128
TK_ATT = 1024
TM_MERGE = 256


def _cparams(sem):
    return pltpu.CompilerParams(dimension_semantics=sem, vmem_limit_bytes=VMEM_LIMIT)


def _dot(a, b):
    return jnp.dot(a, b, preferred_element_type=F32)


def _dot_nt(a, b):
    return lax.dot_general(a, b, (((1,), (1,)), ((), ())), preferred_element_type=F32)


def _dot_tn(a, b):
    return lax.dot_general(a, b, (((0,), (0,)), ((), ())), preferred_element_type=F32)


def _sigmoid(x):
    return 1.0 / (1.0 + jnp.exp(-x))


def _silu(x):
    return x * _sigmoid(x)


def _rms(x, gain):
    ms = jnp.mean(x * x, axis=-1, keepdims=True)
    return x * lax.rsqrt(ms + EPS) * gain


def _const_spec(shape):
    n = len(shape)
    return pl.BlockSpec(shape, lambda *_: (0,) * n)


def _ffn_kernel(x_ref, g_ref, w1_ref, w3_ref, w2_ref, o_ref):
    x = x_ref[...]
    n = _rms(x, g_ref[...]).astype(BF16)
    acc = jnp.zeros(x.shape, F32)
    for lo, hi in FF_CHUNKS:
        h1 = _dot(n, w1_ref[:, lo:hi])
        h3 = _dot(n, w3_ref[:, lo:hi])
        a = (_silu(h1) * h3).astype(BF16)
        acc = acc + _dot(a, w2_ref[lo:hi, :])
    o_ref[...] = x + 0.5 * acc


def _ffn(x, gain, w1, w3, w2):
    t = x.shape[0]
    resident = dict(pipeline_mode=pl.Buffered(1))
    return pl.pallas_call(
        _ffn_kernel,
        grid=(t // TM_FFN,),
        in_specs=[
            pl.BlockSpec((TM_FFN, D_MODEL), lambda i: (i, 0)),
            pl.BlockSpec((1, D_MODEL), lambda i: (0, 0)),
            pl.BlockSpec((D_MODEL, D_FF), lambda i: (0, 0), **resident),
            pl.BlockSpec((D_MODEL, D_FF), lambda i: (0, 0), **resident),
            pl.BlockSpec((D_FF, D_MODEL), lambda i: (0, 0), **resident),
        ],
        out_specs=pl.BlockSpec((TM_FFN, D_MODEL), lambda i: (i, 0)),
        out_shape=jax.ShapeDtypeStruct((t, D_MODEL), F32),
        compiler_params=_cparams(("parallel",)),
        name="ffn",
    )(x, gain, w1, w3, w2)


def _inproj_kernel(x_ref, g_ref, w_ref, ws_ref, o_ref, os_ref, n_sc):
    @pl.when(pl.program_id(1) == 0)
    def _():
        n = _rms(x_ref[...], g_ref[...]).astype(BF16)
        n_sc[...] = n
        os_ref[...] = _dot(n, ws_ref[...])

    o_ref[...] = _dot(n_sc[...], w_ref[...]).astype(BF16)


def _inproj(x, gain, w_main, w_small):
    t = x.shape[0]
    return pl.pallas_call(
        _inproj_kernel,
        grid=(t // TM_PROJ, N_PROJ // TN_PROJ),
        in_specs=[
            pl.BlockSpec((TM_PROJ, D_MODEL), lambda i, j: (i, 0)),
            pl.BlockSpec((1, D_MODEL), lambda i, j: (0, 0)),
            pl.BlockSpec((D_MODEL, TN_PROJ), lambda i, j: (0, j)),
            pl.BlockSpec((D_MODEL, 128), lambda i, j: (0, 0)),
        ],
        out_specs=[
            pl.BlockSpec((TM_PROJ, TN_PROJ), lambda i, j: (i, j)),
            pl.BlockSpec((TM_PROJ, 128), lambda i, j: (i, 0)),
        ],
        out_shape=[
            jax.ShapeDtypeStruct((t, N_PROJ), BF16),
            jax.ShapeDtypeStruct((t, 128), F32),
        ],
        scratch_shapes=[pltpu.VMEM((TM_PROJ, D_MODEL), BF16)],
        compiler_params=_cparams(("parallel", "arbitrary")),
        name="inproj",
    )(x, gain, w_main, w_small)


def _split_hi_lo(x):
    hi = x.astype(BF16)
    lo = (x - hi.astype(F32)).astype(BF16)
    return hi, lo


def _dot01(m01, x):
    hi, lo = _split_hi_lo(x)
    return _dot(m01, hi) + _dot(m01, lo)


def _block_diag(p, mask):
    return jnp.where(mask, jnp.concatenate([p] * 4, axis=0), jnp.zeros((), p.dtype))


def _neumann_inverse(l_mat, eye4, bd_mask):
    q = -l_mat
    a = eye4 + q
    qb = q.astype(BF16)
    q = _dot(qb, _block_diag(qb, bd_mask))
    for _ in range(4):
        qb = q.astype(BF16)
        r = _dot(jnp.concatenate([a.astype(BF16), qb], axis=0), _block_diag(qb, bd_mask))
        a = a + r[:CHUNK]
        q = r[CHUNK:]
    return a + _dot(a.astype(BF16), _block_diag(q.astype(BF16), bd_mask))


def _gdn_prep_kernel(prev_ref, cur_ref, next_ref, sm_ref, cw_ref, alog_ref, bias_ref,
                     u_ref, w_ref, qkm_ref, qd_ref, kd_ref, gt_ref, ext_sc, *, tiles_per_seq):
    tm = TM_GDN
    i = pl.program_id(0)
    jt = i % tiles_per_seq
    m_prev = (jt != 0).astype(F32)
    m_next = (jt != tiles_per_seq - 1).astype(F32)

    ext_sc[0:16, :] = prev_ref[...].astype(F32) * m_prev
    ext_sc[16:16 + tm, :] = cur_ref[...].astype(F32)
    ext_sc[16 + tm:32 + tm, :] = next_ref[...].astype(F32) * m_next
    conv = jnp.zeros((tm, 2 * N_HEAD * DK + N_HEAD * DV), F32)
    for w in range(CONV_W):
        conv = conv + ext_sc[14 + w:14 + w + tm, :] * cw_ref[w:w + 1, :]
    act = _silu(conv)

    nqk = N_HEAD * DK
    q_parts, k_parts = [], []
    for h in range(N_HEAD):
        qh = act[:, h * DK:(h + 1) * DK]
        kh = act[:, nqk + h * DK:nqk + (h + 1) * DK]
        q_parts.append(qh * lax.rsqrt(jnp.sum(qh * qh, axis=-1, keepdims=True) + EPS) * (DK ** -0.5))
        k_parts.append(kh * lax.rsqrt(jnp.sum(kh * kh, axis=-1, keepdims=True) + EPS))
    q_n = jnp.concatenate(q_parts, axis=1)
    k_n = jnp.concatenate(k_parts, axis=1)
    v = act[:, 2 * nqk:]

    sm = sm_ref[...]
    beta8 = _sigmoid(sm)
    zb = sm + bias_ref[...]
    softplus = jnp.maximum(zb, 0.0) + jnp.log(1.0 + jnp.exp(-jnp.abs(zb)))
    g8 = -jnp.exp(alog_ref[...]) * softplus

    def expand_mat(width, row0):
        r = lax.broadcasted_iota(jnp.int32, (128, 8 * width), 0)
        c = lax.broadcasted_iota(jnp.int32, (128, 8 * width), 1) // width
        return (r == c + row0).astype(BF16)

    e64b, e128b = expand_mat(64, 0), expand_mat(128, 0)
    e64g, e128g = expand_mat(64, 8), expand_mat(128, 8)
    beta8b = beta8.astype(BF16)
    beta64 = _dot(beta8b, e64b)
    beta128 = _dot(beta8b, e128b)
    g64 = _dot01_rhs(g8, e64g)
    g128 = _dot01_rhs(g8, e128g)

    ri = lax.broadcasted_iota(jnp.int32, (tm, tm), 0)
    ci = lax.broadcasted_iota(jnp.int32, (tm, tm), 1)
    same = (ri // CHUNK) == (ci // CHUNK)
    m_le = (same & (ci <= ri)).astype(BF16)
    m_ge = (same & (ci >= ri)).astype(BF16)
    m_lt = (same & (ci < ri)).astype(BF16)
    m_gt = (same & (ci > ri)).astype(BF16)

    tl = lax.broadcasted_iota(jnp.int32, (tm, 4 * CHUNK), 0) % CHUNK
    jl = lax.broadcasted_iota(jnp.int32, (tm, 4 * CHUNK), 1) % CHUNK
    hw = N_HEAD * CHUNK
    delta_f = _dot01(m_le, jnp.where(tl > jl, g64[:, :hw], 0.0))
    delta_b = _dot01(m_ge, jnp.where(tl < jl, g64[:, hw:], 0.0))
    hk = N_HEAD * DK
    gc_f = _dot01(m_le, g128[:, :hk])
    rest_f = _dot01(m_gt, g128[:, :hk])
    gc_b = _dot01(m_ge, g128[:, hk:])
    rest_b = _dot01(m_lt, g128[:, hk:])

    il = lax.broadcasted_iota(jnp.int32, (CHUNK, 4 * CHUNK), 0)
    jc = lax.broadcasted_iota(jnp.int32, (CHUNK, 4 * CHUNK), 1) % CHUNK
    eye4 = (il == jc).astype(F32)
    bd_mask = (lax.broadcasted_iota(jnp.int32, (4 * CHUNK, 4 * CHUNK), 0) // CHUNK
               == lax.broadcasted_iota(jnp.int32, (4 * CHUNK, 4 * CHUNK), 1) // CHUNK)
    bdk_mask = (lax.broadcasted_iota(jnp.int32, (4 * CHUNK, hk), 0) // CHUNK
                == lax.broadcasted_iota(jnp.int32, (4 * CHUNK, hk), 1) // DK)

    k_nb = k_n.astype(BF16)
    q_nb = q_n.astype(BF16)
    per_dir = []
    for d, gc, rest, delta, incl, strict in (
            (0, gc_f, rest_f, delta_f, il >= jc, il > jc),
            (1, gc_b, rest_b, delta_b, il <= jc, il < jc)):
        e_gc = jnp.exp(gc)
        qd_ref[d] = (q_n * e_gc).astype(BF16)
        kd_ref[d] = (k_n * jnp.exp(rest)).astype(BF16)
        b128 = beta128[:, d * hk:(d + 1) * hk]
        kbe = (k_n * b128 * e_gc).astype(BF16)
        tot = jnp.exp(gc + rest)
        for c in range(tm // CHUNK):
            gt_ref[d, c] = tot[c * CHUNK:c * CHUNK + 1, :]
        per_dir.append((d, delta, incl, strict, b128, kbe, beta64[:, d * hw:(d + 1) * hw]))

    for c in range(tm // CHUNK):
        r0 = c * CHUNK
        k_c = k_nb[r0:r0 + CHUNK]
        bdk = jnp.where(bdk_mask, jnp.concatenate([k_c] * 4, axis=0), jnp.zeros((), BF16))
        gq = _dot_nt(jnp.concatenate([k_c, q_nb[r0:r0 + CHUNK]], axis=0), bdk)
        kk, qk = gq[:CHUNK], gq[CHUNK:]
        for d, delta, incl, strict, b128, kbe, b64 in per_dir:
            dm = jnp.where(incl, jnp.exp(delta[r0:r0 + CHUNK]), 0.0)
            l_mat = jnp.where(strict, b64[r0:r0 + CHUNK] * kk * dm, 0.0)
            qkm_ref[d, r0:r0 + CHUNK, :] = (qk * dm).astype(BF16)
            t_all = _neumann_inverse(l_mat, eye4, bd_mask).astype(BF16)
            for h in range(N_HEAD):
                bh = b128[r0:r0 + CHUNK, h * DK:(h + 1) * DK]
                vh = v[r0:r0 + CHUNK, h * DV:(h + 1) * DV]
                vb = jnp.concatenate([vh[:, :DK] * bh, vh[:, DK:] * bh], axis=1).astype(BF16)
                rhs = jnp.concatenate([vb, kbe[r0:r0 + CHUNK, h * DK:(h + 1) * DK]], axis=1)
                uw = _dot(t_all[:, h * CHUNK:(h + 1) * CHUNK], rhs)
                u_ref[d, r0:r0 + CHUNK, h * DV:(h + 1) * DV] = uw[:, :DV].astype(BF16)
                w_ref[d, r0:r0 + CHUNK, h * DK:(h + 1) * DK] = uw[:, DV:].astype(BF16)


def _dot01_rhs(x, m01):
    hi, lo = _split_hi_lo(x)
    return _dot(hi, m01) + _dot(lo, m01)


def _gdn_prep(proj, small, conv_w, alog_row, bias_row, seq):
    t = proj.shape[0]
    tm = TM_GDN
    tps = seq // tm
    nqkv = 2 * N_HEAD * DK + N_HEAD * DV
    hb = tm // 16
    last16 = t // 16 - 1
    kern = functools.partial(_gdn_prep_kernel, tiles_per_seq=tps)
    return pl.pallas_call(
        kern,
        grid=(t // tm,),
        in_specs=[
            pl.BlockSpec((16, nqkv), lambda i: (jnp.maximum(i * hb - 1, 0), 0)),
            pl.BlockSpec((tm, nqkv), lambda i: (i, 0)),
            pl.BlockSpec((16, nqkv), lambda i: (jnp.minimum((i + 1) * hb, last16), 0)),
            pl.BlockSpec((tm, 128), lambda i: (i, 0)),
            _const_spec((8, nqkv)),
            _const_spec((1, 128)),
            _const_spec((1, 128)),
        ],
        out_specs=[
            pl.BlockSpec((2, tm, N_HEAD * DV), lambda i: (0, i, 0)),
            pl.BlockSpec((2, tm, N_HEAD * DK), lambda i: (0, i, 0)),
            pl.BlockSpec((2, tm, N_HEAD * CHUNK), lambda i: (0, i, 0)),
            pl.BlockSpec((2, tm, N_HEAD * DK), lambda i: (0, i, 0)),
            pl.BlockSpec((2, tm, N_HEAD * DK), lambda i: (0, i, 0)),
            pl.BlockSpec((2, tm // CHUNK, 1, N_HEAD * DK), lambda i: (0, i, 0, 0)),
        ],
        out_shape=[
            jax.ShapeDtypeStruct((2, t, N_HEAD * DV), BF16),
            jax.ShapeDtypeStruct((2, t, N_HEAD * DK), BF16),
            jax.ShapeDtypeStruct((2, t, N_HEAD * CHUNK), BF16),
            jax.ShapeDtypeStruct((2, t, N_HEAD * DK), BF16),
            jax.ShapeDtypeStruct((2, t, N_HEAD * DK), BF16),
            jax.ShapeDtypeStruct((2, t // CHUNK, 1, N_HEAD * DK), F32),
        ],
        scratch_shapes=[pltpu.VMEM((tm + 32, nqkv), F32)],
        compiler_params=_cparams(("parallel",)),
        name="gdn_prep",
    )(proj, proj, proj, small, conv_w, alog_row, bias_row)


def _gdn_scan_kernel(uf_ref, wf_ref, qkf_ref, qdf_ref, kdf_ref, gtf_ref,
                     ub_ref, wb_ref, qkb_ref, qdb_ref, kdb_ref, gtb_ref,
                     of_ref, ob_ref, s_sc):
    @pl.when(pl.program_id(1) == 0)
    def _():
        s_sc[...] = jnp.zeros(s_sc.shape, F32)

    nch = TM_GDN // CHUNK
    dirs = (
        (0, uf_ref, wf_ref, qkf_ref, qdf_ref, kdf_ref, gtf_ref, of_ref, range(nch)),
        (1, ub_ref, wb_ref, qkb_ref, qdb_ref, kdb_ref, gtb_ref, ob_ref, range(nch - 1, -1, -1)),
    )
    for step in range(nch):
        for d, u_ref, w_ref, qk_ref, qd_ref, kd_ref, gt_ref, o_ref, order in dirs:
            c = order[step]
            r0 = c * CHUNK
            for h in range(N_HEAD):
                s = s_sc[d * N_HEAD + h]
                sb = s.astype(BF16)
                wq = jnp.concatenate([w_ref[0, r0:r0 + CHUNK, h * DK:(h + 1) * DK],
                                      qd_ref[0, r0:r0 + CHUNK, h * DK:(h + 1) * DK]], axis=0)
                ws = _dot(wq, sb)
                v_new = u_ref[0, r0:r0 + CHUNK, h * DV:(h + 1) * DV].astype(F32) - ws[:CHUNK]
                vb = v_new.astype(BF16)
                o = ws[CHUNK:] + _dot(qk_ref[0, r0:r0 + CHUNK, h * CHUNK:(h + 1) * CHUNK], vb)
                o_ref[r0:r0 + CHUNK, h * DV:(h + 1) * DV] = o.astype(BF16)
                gt = gt_ref[0, c, :, h * DK:(h + 1) * DK]
                gtb = jnp.concatenate([gt, gt], axis=1)
                s_sc[d * N_HEAD + h] = s * gtb + _dot_tn(kd_ref[0, r0:r0 + CHUNK, h * DK:(h + 1) * DK], vb)


def _gdn_scan(u, w, qkm, qd, kd, gt, nseq, seq):
    t = u.shape[1]
    tm = TM_GDN
    tps = seq // tm
    nch = tm // CHUNK

    def fwd(width):
        return pl.BlockSpec((1, tm, width), lambda b, j: (0, b * tps + j, 0))

    def bwd(width):
        return pl.BlockSpec((1, tm, width), lambda b, j: (1, b * tps + tps - 1 - j, 0))

    gt_f = pl.BlockSpec((1, nch, 1, N_HEAD * DK), lambda b, j: (0, b * tps + j, 0, 0))
    gt_b = pl.BlockSpec((1, nch, 1, N_HEAD * DK), lambda b, j: (1, b * tps + tps - 1 - j, 0, 0))
    widths = (N_HEAD * DV, N_HEAD * DK, N_HEAD * CHUNK, N_HEAD * DK, N_HEAD * DK)
    return pl.pallas_call(
        _gdn_scan_kernel,
        grid=(nseq, tps),
        in_specs=[fwd(x) for x in widths] + [gt_f] + [bwd(x) for x in widths] + [gt_b],
        out_specs=[
            pl.BlockSpec((tm, N_HEAD * DV), lambda b, j: (b * tps + j, 0)),
            pl.BlockSpec((tm, N_HEAD * DV), lambda b, j: (b * tps + tps - 1 - j, 0)),
        ],
        out_shape=[jax.ShapeDtypeStruct((t, N_HEAD * DV), BF16)] * 2,
        scratch_shapes=[pltpu.VMEM((2 * N_HEAD, DK, DV), F32)],
        compiler_params=_cparams(("parallel", "arbitrary")),
        name="gdn_scan",
    )(u, w, qkm, qd, kd, gt, u, w, qkm, qd, kd, gt)


def _rope_half(x, cos, sin_signed):
    return x * cos + pltpu.roll(x, 64, axis=1) * sin_signed


def _ret_kernel(qf_ref, kf_ref, vf_ref, cosf_ref, sinf_ref,
                qb_ref, kb_ref, vb_ref, cosb_ref, sinb_ref,
                dsum_ref, xif_ref, zf_ref, xib_ref, zb_ref, gch_ref,
                oa_ref, ob_ref, r_sc):
    @pl.when(pl.program_id(1) == 0)
    def _():
        r_sc[...] = jnp.zeros(r_sc.shape, F32)

    scale = DK ** -0.5
    cosf, sinf = cosf_ref[...], sinf_ref[...]
    cosb, sinb = cosb_ref[...], sinb_ref[...]
    for h in range(N_HEAD):
        sl = slice(h * DK, (h + 1) * DK)
        vs = slice(h * DV, (h + 1) * DV)
        q = _rope_half(qf_ref[:, sl].astype(F32), cosf, sinf)
        k = _rope_half(kf_ref[:, sl].astype(F32), cosf, sinf) * scale
        v = vf_ref[:, vs]
        s = (_dot_nt(q.astype(BF16), k.astype(BF16)) * dsum_ref[h]).astype(BF16)
        o = _dot(s, v)
        rf = r_sc[h]
        o = o + _dot((q * xif_ref[:, sl]).astype(BF16), rf.astype(BF16))
        gf = gch_ref[0, :, sl]
        r_sc[h] = rf * jnp.concatenate([gf, gf], axis=1) + _dot_tn((k * zf_ref[:, sl]).astype(BF16), v)
        oa_ref[:, vs] = o.astype(BF16)
        q2 = _rope_half(qb_ref[:, sl].astype(F32), cosb, sinb)
        k2 = _rope_half(kb_ref[:, sl].astype(F32), cosb, sinb) * scale
        v2 = vb_ref[:, vs]
        rb = r_sc[N_HEAD + h]
        ob_ref[:, vs] = _dot((q2 * xib_ref[:, sl]).astype(BF16), rb.astype(BF16)).astype(BF16)
        gb = gch_ref[1, :, sl]
        r_sc[N_HEAD + h] = rb * jnp.concatenate([gb, gb], axis=1) + _dot_tn((k2 * zb_ref[:, sl]).astype(BF16), v2)


def _retention(proj, cos_t, sin_t, dsum, xi_f, zeta_f, xi_b, zeta_b, gch, nseq, seq):
    t = proj.shape[0]
    tr = TR_RET
    tps = seq // tr
    hk, hv = N_HEAD * DK, N_HEAD * DV

    def tok(width, col, mirror):
        cb = col // width
        if mirror:
            return pl.BlockSpec((tr, width), lambda b, j: (b * tps + tps - 1 - j, cb))
        return pl.BlockSpec((tr, width), lambda b, j: (b * tps + j, cb))

    def pos(mirror):
        if mirror:
            return pl.BlockSpec((tr, DK), lambda b, j: (tps - 1 - j, 0))
        return pl.BlockSpec((tr, DK), lambda b, j: (j, 0))

    def side(mirror):
        return [tok(hk, C_RQ, mirror), tok(hk, C_RK, mirror), tok(hv, C_RV, mirror), pos(mirror), pos(mirror)]

    return pl.pallas_call(
        _ret_kernel,
        grid=(nseq, tps),
        in_specs=side(False) + side(True) + [
            _const_spec((N_HEAD, tr, tr)),
            _const_spec((tr, hk)), _const_spec((tr, hk)), _const_spec((tr, hk)), _const_spec((tr, hk)),
            _const_spec((2, 1, hk)),
        ],
        out_specs=[
            pl.BlockSpec((tr, hv), lambda b, j: (b * tps + j, 0)),
            pl.BlockSpec((tr, hv), lambda b, j: (b * tps + tps - 1 - j, 0)),
        ],
        out_shape=[jax.ShapeDtypeStruct((t, hv), BF16)] * 2,
        scratch_shapes=[pltpu.VMEM((2 * N_HEAD, DK, DV), F32)],
        compiler_params=_cparams(("parallel", "arbitrary")),
        name="retention",
    )(proj, proj, proj, cos_t, sin_t, proj, proj, proj, cos_t, sin_t,
      dsum, xi_f, zeta_f, xi_b, zeta_b, gch)


def _axial_rope(x, cos, sin_signed, first_quarter):
    partner = jnp.where(first_quarter, pltpu.roll(x, 96, axis=1), pltpu.roll(x, 32, axis=1))
    return x * cos + partner * sin_signed


def _attn_prep_kernel(q_ref, k_ref, gq_ref, gk_ref, cos_ref, sin_ref, qo_ref, kt_ref):
    cos, sin = cos_ref[...], sin_ref[...]
    fq = (lax.broadcasted_iota(jnp.int32, cos.shape, 1) % 64) < 32
    scale = ATT_HD ** -0.5
    for h in range(ATT_HQ):
        sl = slice(h * ATT_HD, (h + 1) * ATT_HD)
        x = _rms(q_ref[:, sl].astype(F32), gq_ref[...])
        qo_ref[:, sl] = (_axial_rope(x, cos, sin, fq) * scale).astype(BF16)
    for h in range(ATT_HKV):
        sl = slice(h * ATT_HD, (h + 1) * ATT_HD)
        x = _rms(k_ref[:, sl].astype(F32), gk_ref[...])
        kt_ref[sl, :] = _axial_rope(x, cos, sin, fq).T.astype(BF16)


def _attn_prep(proj, gq, gk, cos_t, sin_t, seq):
    t = proj.shape[0]
    tm = TM_ATT_PREP
    tps = seq // tm
    nq, nkv = ATT_HQ * ATT_HD, ATT_HKV * ATT_HD
    return pl.pallas_call(
        _attn_prep_kernel,
        grid=(t // tm,),
        in_specs=[
            pl.BlockSpec((tm, nq), lambda i: (i, C_AQ // nq)),
            pl.BlockSpec((tm, nkv), lambda i: (i, C_AK // nkv)),
            _const_spec((1, ATT_HD)),
            _const_spec((1, ATT_HD)),
            pl.BlockSpec((tm, ATT_HD), lambda i: (i % tps, 0)),
            pl.BlockSpec((tm, ATT_HD), lambda i: (i % tps, 0)),
        ],
        out_specs=[
            pl.BlockSpec((tm, nq), lambda i: (i, 0)),
            pl.BlockSpec((nkv, tm), lambda i: (0, i)),
        ],
        out_shape=[
            jax.ShapeDtypeStruct((t, nq), BF16),
            jax.ShapeDtypeStruct((nkv, t), BF16),
        ],
        compiler_params=_cparams(("parallel",)),
        name="attn_prep",
    )(proj, proj, gq, gk, cos_t, sin_t)


def _flash_kernel(q_ref, kt_ref, v_ref, o_ref, *, seq):
    tq = TQ_ATT
    tk = min(TK_ATT, seq)
    rows = ATT_GROUP * tq
    q4 = jnp.concatenate([q_ref[:, h * ATT_HD:(h + 1) * ATT_HD] for h in range(ATT_GROUP)], axis=0)

    def body(j, carry):
        m, l, acc = carry
        off = pl.multiple_of(j * tk, tk)
        s = _dot(q4, kt_ref[:, pl.ds(off, tk)])
        m_new = jnp.maximum(m, jnp.max(s, axis=-1, keepdims=True))
        alpha = jnp.exp(m - m_new)
        p = jnp.exp(s - m_new)
        l = alpha * l + jnp.sum(p, axis=-1, keepdims=True)
        acc = alpha * acc + _dot(p.astype(BF16), v_ref[pl.ds(off, tk), :])
        return m_new, l, acc

    m0 = jnp.full((rows, 1), -jnp.inf, F32)
    l0 = jnp.zeros((rows, 1), F32)
    a0 = jnp.zeros((rows, ATT_HD), F32)
    _, l, acc = lax.fori_loop(0, seq // tk, body, (m0, l0, a0))
    out = acc / l
    for h in range(ATT_GROUP):
        o_ref[:, h * ATT_HD:(h + 1) * ATT_HD] = out[h * tq:(h + 1) * tq].astype(BF16)


def _flash(q_r, k_t, proj, nseq, seq):
    t = q_r.shape[0]
    tq = TQ_ATT
    nq = seq // tq
    gw = ATT_GROUP * ATT_HD
    return pl.pallas_call(
        functools.partial(_flash_kernel, seq=seq),
        grid=(nseq, ATT_HKV, nq),
        in_specs=[
            pl.BlockSpec((tq, gw), lambda b, g, i: (b * nq + i, g)),
            pl.BlockSpec((ATT_HD, seq), lambda b, g, i: (g, b)),
            pl.BlockSpec((seq, ATT_HD), lambda b, g, i: (b, C_AV // ATT_HD + g)),
        ],
        out_specs=pl.BlockSpec((tq, gw), lambda b, g, i: (b * nq + i, g)),
        out_shape=jax.ShapeDtypeStruct((t, ATT_HQ * ATT_HD), BF16),
        compiler_params=_cparams(("parallel", "parallel", "arbitrary")),
        name="flash_gqa",
    )(q_r, k_t, proj)


def _merge_kernel(x_ref, gof_ref, gob_ref, gz_ref, roa_ref, rob_ref, rg_ref, att_ref,
                  g0_ref, g1_ref, g2_ref, gn_ref, rn_ref, wg_ref, wr_ref, wa_ref, wo_ref, o_ref):
    go = gof_ref[...].astype(F32) + gob_ref[...].astype(F32)
    ro = roa_ref[...].astype(F32) + rob_ref[...].astype(F32)
    gz = gz_ref[...].astype(F32)
    rg = rg_ref[...].astype(F32)
    ga_parts, rb_parts = [], []
    for h in range(N_HEAD):
        vs = slice(h * DV, (h + 1) * DV)
        gh = go[:, vs]
        gh = gh * lax.rsqrt(jnp.mean(gh * gh, axis=-1, keepdims=True) + EPS) * gn_ref[...]
        ga_parts.append((gh * _silu(gz[:, vs])).astype(BF16))
        rh = ro[:, vs]
        mu = jnp.mean(rh, axis=-1, keepdims=True)
        cen = rh - mu
        var = jnp.mean(cen * cen, axis=-1, keepdims=True)
        rh = cen * lax.rsqrt(var + EPS) * rn_ref[:, vs]
        rb_parts.append((_silu(rg[:, vs]) * rh).astype(BF16))
    branch_a = _dot(jnp.concatenate(ga_parts, axis=1), wg_ref[...])
    branch_b = _dot(jnp.concatenate(rb_parts, axis=1), wr_ref[...])
    branch_c = _dot(att_ref[...], wa_ref[...])
    merged = (_sigmoid(g0_ref[...].astype(F32)) * branch_a
              + _sigmoid(g1_ref[...].astype(F32)) * branch_b
              + _sigmoid(g2_ref[...].astype(F32)) * branch_c)
    o_ref[...] = x_ref[...] + _dot(merged.astype(BF16), wo_ref[...])


def _merge(x, go_f, go_b, ro_a, ro_b, att, proj, gdn_norm, ret_norm, wg, wr, wa, wo):
    t = x.shape[0]
    tm = TM_MERGE
    d = D_MODEL

    def tok(col=0):
        return pl.BlockSpec((tm, d), lambda i: (i, col // d))

    return pl.pallas_call(
        _merge_kernel,
        grid=(t // tm,),
        in_specs=[
            tok(), tok(), tok(), tok(C_GZ), tok(), tok(), tok(C_RG), tok(),
            tok(C_GATE), tok(C_GATE + d), tok(C_GATE + 2 * d),
            _const_spec((1, DV)), _const_spec((1, d)),
            _const_spec((d, d)), _const_spec((d, d)), _const_spec((d, d)), _const_spec((d, d)),
        ],
        out_specs=tok(),
        out_shape=jax.ShapeDtypeStruct((t, d), F32),
        compiler_params=_cparams(("parallel",)),
        name="merge",
    )(x, go_f, go_b, proj, ro_a, ro_b, proj, att, proj, proj, proj,
      gdn_norm, ret_norm, wg, wr, wa, wo)


def _rope_tables(seq):
    pos = jnp.arange(seq, dtype=F32)
    inv = ROPE_THETA ** (-jnp.arange(0, DK, 2, dtype=F32) / DK)
    ang = pos[:, None] * inv[None, :]
    c, s = jnp.cos(ang), jnp.sin(ang)
    ret_cos = jnp.concatenate([c, c], axis=1)
    ret_sin = jnp.concatenate([-s, s], axis=1)
    half = ATT_HD // 2
    inv_a = ROPE_THETA ** (-jnp.arange(0, half, 2, dtype=F32) / half)
    rows = (jnp.arange(seq) // GRID_W).astype(F32)
    cols = (jnp.arange(seq) % GRID_W).astype(F32)
    ar, ac = rows[:, None] * inv_a[None, :], cols[:, None] * inv_a[None, :]
    cr, sr, cc, sc = jnp.cos(ar), jnp.sin(ar), jnp.cos(ac), jnp.sin(ac)
    att_cos = jnp.concatenate([cr, cr, cc, cc], axis=1)
    att_sin = jnp.concatenate([-sr, sr, -sc, sc], axis=1)
    return ret_cos, ret_sin, att_cos, att_sin


def _ret_tables(decay_logit):
    c = TR_RET
    lg = jax.nn.log_sigmoid(decay_logit.astype(F32))
    idx = jnp.arange(c, dtype=F32)
    rel = idx[:, None] - idx[None, :]
    lf, lb = lg[0][:, None, None], lg[1][:, None, None]
    dsum = (jnp.exp(jnp.where(rel >= 0, rel * lf, -jnp.inf))
            + jnp.exp(jnp.where(rel <= 0, -rel * lb, -jnp.inf)))

    def lanes(tab):
        return jnp.repeat(tab.T, DK, axis=1)

    xi_f = lanes(jnp.exp((idx + 1.0)[None, :] * lg[0][:, None]))
    zeta_f = lanes(jnp.exp((c - 1.0 - idx)[None, :] * lg[0][:, None]))
    xi_b = lanes(jnp.exp((c - idx)[None, :] * lg[1][:, None]))
    zeta_b = lanes(jnp.exp(idx[None, :] * lg[1][:, None]))
    gch = jnp.repeat(jnp.exp(c * lg), DK, axis=1)[:, None, :]
    return dsum, xi_f, zeta_f, xi_b, zeta_b, gch


def _layer_params(l, p):
    w_in = p["w_in"][l]
    n_gdn = 2 * N_HEAD * DK + 2 * N_HEAD * DV
    rest = w_in[:, n_gdn + 16:]
    n_ret = 2 * N_HEAD * DK + 2 * N_HEAD * DV
    n_att = ATT_HQ * ATT_HD + 2 * ATT_HKV * ATT_HD
    w_main = jnp.concatenate(
        [w_in[:, :n_gdn], rest[:, n_ret + n_att:], rest[:, :n_ret + n_att]], axis=1).astype(BF16)
    w_small = jnp.pad(w_in[:, n_gdn:n_gdn + 16], ((0, 0), (0, 112))).astype(BF16)
    conv_w = jnp.pad(p["gdn_conv"][l].astype(F32), ((0, 8 - CONV_W), (0, 0)))
    alog_row = jnp.pad(p["gdn_A_log"][l].astype(F32).reshape(1, 8), ((0, 0), (8, 112)))
    bias_row = jnp.pad(p["gdn_dt_bias"][l].astype(F32).reshape(1, 8), ((0, 0), (8, 112)))
    return dict(
        ffn1=(p["ffn1_norm"][l][None, :], p["ffn1_w1"][l].astype(BF16),
              p["ffn1_w3"][l].astype(BF16), p["ffn1_w2"][l].astype(BF16)),
        ffn2=(p["ffn2_norm"][l][None, :], p["ffn2_w1"][l].astype(BF16),
              p["ffn2_w3"][l].astype(BF16), p["ffn2_w2"][l].astype(BF16)),
        mix_norm=p["mix_norm"][l][None, :], w_main=w_main, w_small=w_small,
        conv_w=conv_w, alog_row=alog_row, bias_row=bias_row,
        ret_tabs=_ret_tables(p["ret_decay_logit"][l]),
        gq=p["attn_q_norm"][l][None, :], gk=p["attn_k_norm"][l][None, :],
        gdn_norm=p["gdn_norm"][l][None, :], ret_norm=p["ret_norm"][l][None, :],
        wg=p["w_branch_gdn"][l].astype(BF16), wr=p["w_branch_ret"][l].astype(BF16),
        wa=p["w_branch_attn"][l].astype(BF16), wo=p["w_out"][l].astype(BF16),
    )


def _encoder_layer(x, lp, tabs, nseq, seq):
    ret_cos, ret_sin, att_cos, att_sin = tabs
    x = _ffn(x, *lp["ffn1"])
    proj, small = _inproj(x, lp["mix_norm"], lp["w_main"], lp["w_small"])
    u, w, qkm, qd, kd, gt = _gdn_prep(proj, small, lp["conv_w"], lp["alog_row"], lp["bias_row"], seq)
    go_f, go_b = _gdn_scan(u, w, qkm, qd, kd, gt, nseq, seq)
    ro_a, ro_b = _retention(proj, ret_cos, ret_sin, *lp["ret_tabs"], nseq, seq)
    q_r, k_t = _attn_prep(proj, lp["gq"], lp["gk"], att_cos, att_sin, seq)
    att = _flash(q_r, k_t, proj, nseq, seq)
    x = _merge(x, go_f, go_b, ro_a, ro_b, att, proj, lp["gdn_norm"], lp["ret_norm"],
               lp["wg"], lp["wr"], lp["wa"], lp["wo"])
    return _ffn(x, *lp["ffn2"])


def _trunk(x3, layers, tabs):
    nseq, seq, d = x3.shape
    x = x3.reshape(nseq * seq, d)
    for lp in layers:
        x = _encoder_layer(x, lp, tabs, nseq, seq)
    return x.reshape(nseq, seq, d)


def kernel(x_prompt, x_sample, ffn1_norm, ffn1_w1, ffn1_w3, ffn1_w2, mix_norm, w_in, gdn_conv, gdn_A_log, gdn_dt_bias, gdn_norm, ret_decay_logit, ret_norm, attn_q_norm, attn_k_norm, w_branch_gdn, w_branch_ret, w_branch_attn, w_out, ffn2_norm, ffn2_w1, ffn2_w3, ffn2_w2):
    p = dict(ffn1_norm=ffn1_norm, ffn1_w1=ffn1_w1, ffn1_w3=ffn1_w3, ffn1_w2=ffn1_w2, mix_norm=mix_norm,
             w_in=w_in, gdn_conv=gdn_conv, gdn_A_log=gdn_A_log, gdn_dt_bias=gdn_dt_bias, gdn_norm=gdn_norm,
             ret_decay_logit=ret_decay_logit, ret_norm=ret_norm, attn_q_norm=attn_q_norm,
             attn_k_norm=attn_k_norm, w_branch_gdn=w_branch_gdn, w_branch_ret=w_branch_ret,
             w_branch_attn=w_branch_attn, w_out=w_out, ffn2_norm=ffn2_norm, ffn2_w1=ffn2_w1,
             ffn2_w3=ffn2_w3, ffn2_w2=ffn2_w2)
    depth = w_in.shape[0]
    layers = [_layer_params(l, p) for l in range(depth)]
    assert x_prompt.shape[1] == x_sample.shape[1]
    tabs = _rope_tables(x_prompt.shape[1])
    return (_trunk(x_prompt, layers, tabs), _trunk(x_sample, layers, tabs))
```

```python
import functools
import math

import jax
import jax.numpy as jnp
from jax import lax
from jax.experimental import pallas as pl
from jax.experimental.pallas import tpu as pltpu

F32 = jnp.float32
BF16 = jnp.bfloat16

D_MODEL = 1024
D_FF = 2816
EPS = 1e-6
ROPE_THETA = 10000.0
GRID_W = 64
LOG2E = math.log2(math.e)

N_HEAD = 4
DK = 128
DV = 256
CONV_W = 5
CHUNK = 64
ATT_HQ = 8
ATT_HKV = 2
ATT_HD = 128
ATT_GROUP = ATT_HQ // ATT_HKV

C_GQKV = 0
C_GZ = 2048
C_GATE = 3072
C_RQ = 6144
C_RK = 6656
C_RV = 7168
C_RG = 8192
C_AQ = 9216
C_AK = 10240
C_AV = 10496
N_PROJ = 10752

VMEM_LIMIT = 56 * 1024 * 1024

TM_FFN = 512
FF_CHUNKS = ((0, 768), (768, 1536), (1536, 2304), (2304, 2816))
TM_PROJ = 1024
TN_PROJ = 1536
TM_GDN = 256
TR_RET = 256
TM_ATT_PREP = 256
TQ_ATT = 128
TK_ATT = 1024
TM_MERGE = 256


def _cparams(sem):
    return pltpu.CompilerParams(dimension_semantics=sem, vmem_limit_bytes=VMEM_LIMIT)


def _dot(a, b):
    return jnp.dot(a, b, preferred_element_type=F32)


def _dot_nt(a, b):
    return lax.dot_general(a, b, (((1,), (1,)), ((), ())), preferred_element_type=F32)


def _dot_tn(a, b):
    return lax.dot_general(a, b, (((0,), (0,)), ((), ())), preferred_element_type=F32)


def _sigmoid(x):
    return 1.0 / (1.0 + jnp.exp(-x))


def _silu(x):
    return x * _sigmoid(x)


def _rms(x, gain):
    ms = jnp.mean(x * x, axis=-1, keepdims=True)
    return x * lax.rsqrt(ms + EPS) * gain


def _const_spec(shape):
    n = len(shape)
    return pl.BlockSpec(shape, lambda *_: (0,) * n)


def _ffn_kernel(x_ref, g_ref, w1_ref, w3_ref, w2_ref, o_ref):
    x = x_ref[...]
    n = _rms(x, g_ref[...]).astype(BF16)
    acc = jnp.zeros(x.shape, F32)
    for lo, hi in FF_CHUNKS:
        h1 = _dot(n, w1_ref[:, lo:hi])
        h3 = _dot(n, w3_ref[:, lo:hi])
        a = (_silu(h1) * h3).astype(BF16)
        acc = acc + _dot(a, w2_ref[lo:hi, :])
    o_ref[...] = x + 0.5 * acc


def _ffn(x, gain, w1, w3, w2):
    t = x.shape[0]
    resident = dict(pipeline_mode=pl.Buffered(1))
    return pl.pallas_call(
        _ffn_kernel,
        grid=(t // TM_FFN,),
        in_specs=[
            pl.BlockSpec((TM_FFN, D_MODEL), lambda i: (i, 0)),
            pl.BlockSpec((1, D_MODEL), lambda i: (0, 0)),
            pl.BlockSpec((D_MODEL, D_FF), lambda i: (0, 0), **resident),
            pl.BlockSpec((D_MODEL, D_FF), lambda i: (0, 0), **resident),
            pl.BlockSpec((D_FF, D_MODEL), lambda i: (0, 0), **resident),
        ],
        out_specs=pl.BlockSpec((TM_FFN, D_MODEL), lambda i: (i, 0)),
        out_shape=jax.ShapeDtypeStruct((t, D_MODEL), F32),
        compiler_params=_cparams(("parallel",)),
        name="ffn",
    )(x, gain, w1, w3, w2)


def _inproj_kernel(x_ref, g_ref, w_ref, ws_ref, o_ref, os_ref, n_sc):
    @pl.when(pl.program_id(1) == 0)
    def _():
        n = _rms(x_ref[...], g_ref[...]).astype(BF16)
        n_sc[...] = n
        os_ref[...] = _dot(n, ws_ref[...])

    o_ref[...] = _dot(n_sc[...], w_ref[...]).astype(BF16)


def _inproj(x, gain, w_main, w_small):
    t = x.shape[0]
    return pl.pallas_call(
        _inproj_kernel,
        grid=(t // TM_PROJ, N_PROJ // TN_PROJ),
        in_specs=[
            pl.BlockSpec((TM_PROJ, D_MODEL), lambda i, j: (i, 0)),
            pl.BlockSpec((1, D_MODEL), lambda i, j: (0, 0)),
            pl.BlockSpec((D_MODEL, TN_PROJ), lambda i, j: (0, j)),
            pl.BlockSpec((D_MODEL, 128), lambda i, j: (0, 0)),
        ],
        out_specs=[
            pl.BlockSpec((TM_PROJ, TN_PROJ), lambda i, j: (i, j)),
            pl.BlockSpec((TM_PROJ, 128), lambda i, j: (i, 0)),
        ],
        out_shape=[
            jax.ShapeDtypeStruct((t, N_PROJ), BF16),
            jax.ShapeDtypeStruct((t, 128), F32),
        ],
        scratch_shapes=[pltpu.VMEM((TM_PROJ, D_MODEL), BF16)],
        compiler_params=_cparams(("parallel", "arbitrary")),
        name="inproj",
    )(x, gain, w_main, w_small)


def _split_hi_lo(x):
    hi = x.astype(BF16)
    lo = (x - hi.astype(F32)).astype(BF16)
    return hi, lo


def _dot01(m01, x):
    hi, lo = _split_hi_lo(x)
    return _dot(m01, hi) + _dot(m01, lo)


def _block_diag(p, mask):
    return jnp.where(mask, jnp.concatenate([p] * 4, axis=0), jnp.zeros((), p.dtype))


def _neumann_inverse(l_mats, eye4, bd_mask):
    qs = [-l for l in l_mats]
    accs = [eye4 + q for q in qs]
    qbs = [q.astype(BF16) for q in qs]
    qs = [_dot(qb, _block_diag(qb, bd_mask)) for qb in qbs]
    for _ in range(4):
        qbs = [q.astype(BF16) for q in qs]
        rs = [_dot(jnp.concatenate([a.astype(BF16), qb], axis=0), _block_diag(qb, bd_mask))
              for a, qb in zip(accs, qbs)]
        accs = [a + r[:CHUNK] for a, r in zip(accs, rs)]
        qs = [r[CHUNK:] for r in rs]
    return [(a + _dot(a.astype(BF16), _block_diag(q.astype(BF16), bd_mask))).astype(BF16)
            for a, q in zip(accs, qs)]


def _gdn_prep_kernel(prev_ref, cur_ref, next_ref, sm_ref, cw_ref, alog_ref, bias_ref,
                     u_ref, w_ref, qkm_ref, qd_ref, kd_ref, gt_ref, ext_sc, *, tiles_per_seq):
    tm = TM_GDN
    i = pl.program_id(0)
    jt = i % tiles_per_seq
    m_prev = (jt != 0).astype(F32)
    m_next = (jt != tiles_per_seq - 1).astype(F32)

    ext_sc[0:16, :] = prev_ref[...].astype(F32) * m_prev
    ext_sc[16:16 + tm, :] = cur_ref[...].astype(F32)
    ext_sc[16 + tm:32 + tm, :] = next_ref[...].astype(F32) * m_next
    conv = jnp.zeros((tm, 2 * N_HEAD * DK + N_HEAD * DV), F32)
    for w in range(CONV_W):
        conv = conv + ext_sc[14 + w:14 + w + tm, :] * cw_ref[w:w + 1, :]
    act = _silu(conv)

    nqk = N_HEAD * DK
    q_parts, k_parts = [], []
    for h in range(N_HEAD):
        qh = act[:, h * DK:(h + 1) * DK]
        kh = act[:, nqk + h * DK:nqk + (h + 1) * DK]
        q_parts.append(qh * lax.rsqrt(jnp.sum(qh * qh, axis=-1, keepdims=True) + EPS) * (DK ** -0.5))
        k_parts.append(kh * lax.rsqrt(jnp.sum(kh * kh, axis=-1, keepdims=True) + EPS))
    q_n = jnp.concatenate(q_parts, axis=1)
    k_n = jnp.concatenate(k_parts, axis=1)
    v = act[:, 2 * nqk:]

    sm = sm_ref[...]
    beta8 = _sigmoid(sm)
    zb = sm + bias_ref[...]
    softplus = jnp.maximum(zb, 0.0) + jnp.log(1.0 + jnp.exp(-jnp.abs(zb)))
    g8 = -jnp.exp(alog_ref[...]) * softplus

    def expand_mat(width, row0):
        r = lax.broadcasted_iota(jnp.int32, (128, 8 * width), 0)
        c = lax.broadcasted_iota(jnp.int32, (128, 8 * width), 1) // width
        return (r == c + row0).astype(BF16)

    e64b, e128b = expand_mat(64, 0), expand_mat(128, 0)
    e64g, e128g = expand_mat(64, 8), expand_mat(128, 8)
    beta8b = beta8.astype(BF16)
    beta64 = _dot(beta8b, e64b)
    beta128 = _dot(beta8b, e128b)
    g64 = _dot01_rhs(g8, e64g)
    g128 = _dot01_rhs(g8, e128g)

    ri = lax.broadcasted_iota(jnp.int32, (tm, tm), 0)
    ci = lax.broadcasted_iota(jnp.int32, (tm, tm), 1)
    same = (ri // CHUNK) == (ci // CHUNK)
    m_le = (same & (ci <= ri)).astype(BF16)
    m_ge = (same & (ci >= ri)).astype(BF16)
    m_lt = (same & (ci < ri)).astype(BF16)
    m_gt = (same & (ci > ri)).astype(BF16)

    tl = lax.broadcasted_iota(jnp.int32, (tm, 4 * CHUNK), 0) % CHUNK
    jl = lax.broadcasted_iota(jnp.int32, (tm, 4 * CHUNK), 1) % CHUNK
    hw = N_HEAD * CHUNK
    delta_f = _dot01(m_le, jnp.where(tl > jl, g64[:, :hw], 0.0))
    delta_b = _dot01(m_ge, jnp.where(tl < jl, g64[:, hw:], 0.0))
    hk = N_HEAD * DK
    gc_f = _dot01(m_le, g128[:, :hk])
    rest_f = _dot01(m_gt, g128[:, :hk])
    gc_b = _dot01(m_ge, g128[:, hk:])
    rest_b = _dot01(m_lt, g128[:, hk:])

    il = lax.broadcasted_iota(jnp.int32, (CHUNK, 4 * CHUNK), 0)
    jc = lax.broadcasted_iota(jnp.int32, (CHUNK, 4 * CHUNK), 1) % CHUNK
    eye4 = (il == jc).astype(F32)
    bd_mask = (lax.broadcasted_iota(jnp.int32, (4 * CHUNK, 4 * CHUNK), 0) // CHUNK
               == lax.broadcasted_iota(jnp.int32, (4 * CHUNK, 4 * CHUNK), 1) // CHUNK)
    bdk_mask = (lax.broadcasted_iota(jnp.int32, (4 * CHUNK, hk), 0) // CHUNK
                == lax.broadcasted_iota(jnp.int32, (4 * CHUNK, hk), 1) // DK)

    k_nb = k_n.astype(BF16)
    q_nb = q_n.astype(BF16)
    per_dir = []
    for d, gc, rest, delta, incl, strict in (
            (0, gc_f, rest_f, delta_f, il >= jc, il > jc),
            (1, gc_b, rest_b, delta_b, il <= jc, il < jc)):
        e_gc = jnp.exp(gc)
        qd_ref[d] = (q_n * e_gc).astype(BF16)
        kd_ref[d] = (k_n * jnp.exp(rest)).astype(BF16)
        b128 = beta128[:, d * hk:(d + 1) * hk]
        kbe = (k_n * b128 * e_gc).astype(BF16)
        tot = jnp.exp(gc + rest)
        for c in range(tm // CHUNK):
            gt_ref[d, c] = tot[c * CHUNK:c * CHUNK + 1, :]
        per_dir.append((d, delta, incl, strict, b128, kbe, beta64[:, d * hw:(d + 1) * hw]))

    l_mats, keys = [], []
    for c in range(tm // CHUNK):
        r0 = c * CHUNK
        k_c = k_nb[r0:r0 + CHUNK]
        bdk = jnp.where(bdk_mask, jnp.concatenate([k_c] * 4, axis=0), jnp.zeros((), BF16))
        gq = _dot_nt(jnp.concatenate([k_c, q_nb[r0:r0 + CHUNK]], axis=0), bdk)
        kk, qk = gq[:CHUNK], gq[CHUNK:]
        for d, delta, incl, strict, b128, kbe, b64 in per_dir:
            dm = jnp.where(incl, jnp.exp(delta[r0:r0 + CHUNK]), 0.0)
            l_mats.append(jnp.where(strict, b64[r0:r0 + CHUNK] * kk * dm, 0.0))
            qkm_ref[d, r0:r0 + CHUNK, :] = (qk * dm).astype(BF16)
            keys.append((r0, d, b128, kbe))
    t_alls = _neumann_inverse(l_mats, eye4, bd_mask)
    for (r0, d, b128, kbe), t_all in zip(keys, t_alls):
        for h in range(N_HEAD):
            bh = b128[r0:r0 + CHUNK, h * DK:(h + 1) * DK]
            vh = v[r0:r0 + CHUNK, h * DV:(h + 1) * DV]
            vb = jnp.concatenate([vh[:, :DK] * bh, vh[:, DK:] * bh], axis=1).astype(BF16)
            rhs = jnp.concatenate([vb, kbe[r0:r0 + CHUNK, h * DK:(h + 1) * DK]], axis=1)
            uw = _dot(t_all[:, h * CHUNK:(h + 1) * CHUNK], rhs)
            u_ref[d, r0:r0 + CHUNK, h * DV:(h + 1) * DV] = uw[:, :DV].astype(BF16)
            w_ref[d, r0:r0 + CHUNK, h * DK:(h + 1) * DK] = uw[:, DV:].astype(BF16)


def _dot01_rhs(x, m01):
    hi, lo = _split_hi_lo(x)
    return _dot(hi, m01) + _dot(lo, m01)


def _gdn_prep(proj, small, conv_w, alog_row, bias_row, seq):
    t = proj.shape[0]
    tm = TM_GDN
    tps = seq // tm
    nqkv = 2 * N_HEAD * DK + N_HEAD * DV
    hb = tm // 16
    last16 = t // 16 - 1
    kern = functools.partial(_gdn_prep_kernel, tiles_per_seq=tps)
    return pl.pallas_call(
        kern,
        grid=(t // tm,),
        in_specs=[
            pl.BlockSpec((16, nqkv), lambda i: (jnp.maximum(i * hb - 1, 0), 0)),
            pl.BlockSpec((tm, nqkv), lambda i: (i, 0)),
            pl.BlockSpec((16, nqkv), lambda i: (jnp.minimum((i + 1) * hb, last16), 0)),
            pl.BlockSpec((tm, 128), lambda i: (i, 0)),
            _const_spec((8, nqkv)),
            _const_spec((1, 128)),
            _const_spec((1, 128)),
        ],
        out_specs=[
            pl.BlockSpec((2, tm, N_HEAD * DV), lambda i: (0, i, 0)),
            pl.BlockSpec((2, tm, N_HEAD * DK), lambda i: (0, i, 0)),
            pl.BlockSpec((2, tm, N_HEAD * CHUNK), lambda i: (0, i, 0)),
            pl.BlockSpec((2, tm, N_HEAD * DK), lambda i: (0, i, 0)),
            pl.BlockSpec((2, tm, N_HEAD * DK), lambda i: (0, i, 0)),
            pl.BlockSpec((2, tm // CHUNK, 1, N_HEAD * DK), lambda i: (0, i, 0, 0)),
        ],
        out_shape=[
            jax.ShapeDtypeStruct((2, t, N_HEAD * DV), BF16),
            jax.ShapeDtypeStruct((2, t, N_HEAD * DK), BF16),
            jax.ShapeDtypeStruct((2, t, N_HEAD * CHUNK), BF16),
            jax.ShapeDtypeStruct((2, t, N_HEAD * DK), BF16),
            jax.ShapeDtypeStruct((2, t, N_HEAD * DK), BF16),
            jax.ShapeDtypeStruct((2, t // CHUNK, 1, N_HEAD * DK), F32),
        ],
        scratch_shapes=[pltpu.VMEM((tm + 32, nqkv), F32)],
        compiler_params=_cparams(("parallel",)),
        name="gdn_prep",
    )(proj, proj, proj, small, conv_w, alog_row, bias_row)


def _gdn_scan_kernel(uf_ref, wf_ref, qkf_ref, qdf_ref, kdf_ref, gtf_ref,
                     ub_ref, wb_ref, qkb_ref, qdb_ref, kdb_ref, gtb_ref,
                     of_ref, ob_ref, s_sc):
    @pl.when(pl.program_id(1) == 0)
    def _():
        s_sc[...] = jnp.zeros(s_sc.shape, F32)

    nch = TM_GDN // CHUNK
    dirs = (
        (0, uf_ref, wf_ref, qkf_ref, qdf_ref, kdf_ref, gtf_ref, of_ref, range(nch)),
        (1, ub_ref, wb_ref, qkb_ref, qdb_ref, kdb_ref, gtb_ref, ob_ref, range(nch - 1, -1, -1)),
    )
    chains = [(dr, h) for dr in dirs for h in range(N_HEAD)]
    states = [s_sc[d * N_HEAD + h] for (d, *_), h in chains]
    for step in range(nch):
        wss = []
        for ((d, u_ref, w_ref, qk_ref, qd_ref, kd_ref, gt_ref, o_ref, order), h), s in zip(chains, states):
            r0 = order[step] * CHUNK
            wq = jnp.concatenate([w_ref[0, r0:r0 + CHUNK, h * DK:(h + 1) * DK],
                                  qd_ref[0, r0:r0 + CHUNK, h * DK:(h + 1) * DK]], axis=0)
            wss.append(_dot(wq, s.astype(BF16)))
        vbs = []
        for ((d, u_ref, w_ref, qk_ref, qd_ref, kd_ref, gt_ref, o_ref, order), h), ws in zip(chains, wss):
            r0 = order[step] * CHUNK
            v_new = u_ref[0, r0:r0 + CHUNK, h * DV:(h + 1) * DV].astype(F32) - ws[:CHUNK]
            vbs.append(v_new.astype(BF16))
        new_states = []
        for ((d, u_ref, w_ref, qk_ref, qd_ref, kd_ref, gt_ref, o_ref, order), h), s, vb in zip(chains, states, vbs):
            c = order[step]
            r0 = c * CHUNK
            gt = gt_ref[0, c, :, h * DK:(h + 1) * DK]
            gtb = jnp.concatenate([gt, gt], axis=1)
            new_states.append(s * gtb + _dot_tn(kd_ref[0, r0:r0 + CHUNK, h * DK:(h + 1) * DK], vb))
        for ((d, u_ref, w_ref, qk_ref, qd_ref, kd_ref, gt_ref, o_ref, order), h), ws, vb in zip(chains, wss, vbs):
            r0 = order[step] * CHUNK
            o = ws[CHUNK:] + _dot(qk_ref[0, r0:r0 + CHUNK, h * CHUNK:(h + 1) * CHUNK], vb)
            o_ref[r0:r0 + CHUNK, h * DV:(h + 1) * DV] = o.astype(BF16)
        states = new_states
    for ((d, *_), h), s in zip(chains, states):
        s_sc[d * N_HEAD + h] = s


def _gdn_scan(u, w, qkm, qd, kd, gt, nseq, seq):
    t = u.shape[1]
    tm = TM_GDN
    tps = seq // tm
    nch = tm // CHUNK

    def fwd(width):
        return pl.BlockSpec((1, tm, width), lambda b, j: (0, b * tps + j, 0))

    def bwd(width):
        return pl.BlockSpec((1, tm, width), lambda b, j: (1, b * tps + tps - 1 - j, 0))

    gt_f = pl.BlockSpec((1, nch, 1, N_HEAD * DK), lambda b, j: (0, b * tps + j, 0, 0))
    gt_b = pl.BlockSpec((1, nch, 1, N_HEAD * DK), lambda b, j: (1, b * tps + tps - 1 - j, 0, 0))
    widths = (N_HEAD * DV, N_HEAD * DK, N_HEAD * CHUNK, N_HEAD * DK, N_HEAD * DK)
    return pl.pallas_call(
        _gdn_scan_kernel,
        grid=(nseq, tps),
        in_specs=[fwd(x) for x in widths] + [gt_f] + [bwd(x) for x in widths] + [gt_b],
        out_specs=[
            pl.BlockSpec((tm, N_HEAD * DV), lambda b, j: (b * tps + j, 0)),
            pl.BlockSpec((tm, N_HEAD * DV), lambda b, j: (b * tps + tps - 1 - j, 0)),
        ],
        out_shape=[jax.ShapeDtypeStruct((t, N_HEAD * DV), BF16)] * 2,
        scratch_shapes=[pltpu.VMEM((2 * N_HEAD, DK, DV), F32)],
        compiler_params=_cparams(("parallel", "arbitrary")),
        name="gdn_scan",
    )(u, w, qkm, qd, kd, gt, u, w, qkm, qd, kd, gt)


def _rope_half(x, cos, sin_signed):
    return x * cos + pltpu.roll(x, 64, axis=1) * sin_signed


def _ret_kernel(qf_ref, kf_ref, vf_ref, cosf_ref, sinf_ref,
                qb_ref, kb_ref, vb_ref, cosb_ref, sinb_ref,
                dsum_ref, xif_ref, zf_ref, xib_ref, zb_ref, gch_ref,
                oa_ref, ob_ref, r_sc):
    @pl.when(pl.program_id(1) == 0)
    def _():
        r_sc[...] = jnp.zeros(r_sc.shape, F32)

    scale = DK ** -0.5
    cosf, sinf = cosf_ref[...], sinf_ref[...]
    cosb, sinb = cosb_ref[...], sinb_ref[...]
    for h in range(N_HEAD):
        sl = slice(h * DK, (h + 1) * DK)
        vs = slice(h * DV, (h + 1) * DV)
        q = _rope_half(qf_ref[:, sl].astype(F32), cosf, sinf)
        k = _rope_half(kf_ref[:, sl].astype(F32), cosf, sinf) * scale
        v = vf_ref[:, vs]
        s = (_dot_nt(q.astype(BF16), k.astype(BF16)) * dsum_ref[h]).astype(BF16)
        o = _dot(s, v)
        rf = r_sc[h]
        o = o + _dot((q * xif_ref[:, sl]).astype(BF16), rf.astype(BF16))
        gf = gch_ref[0, :, sl]
        r_sc[h] = rf * jnp.concatenate([gf, gf], axis=1) + _dot_tn((k * zf_ref[:, sl]).astype(BF16), v)
        oa_ref[:, vs] = o.astype(BF16)
        q2 = _rope_half(qb_ref[:, sl].astype(F32), cosb, sinb)
        k2 = _rope_half(kb_ref[:, sl].astype(F32), cosb, sinb) * scale
        v2 = vb_ref[:, vs]
        rb = r_sc[N_HEAD + h]
        ob_ref[:, vs] = _dot((q2 * xib_ref[:, sl]).astype(BF16), rb.astype(BF16)).astype(BF16)
        gb = gch_ref[1, :, sl]
        r_sc[N_HEAD + h] = rb * jnp.concatenate([gb, gb], axis=1) + _dot_tn((k2 * zb_ref[:, sl]).astype(BF16), v2)


def _retention(proj, cos_t, sin_t, dsum, xi_f, zeta_f, xi_b, zeta_b, gch, nseq, seq):
    t = proj.shape[0]
    tr = TR_RET
    tps = seq // tr
    hk, hv = N_HEAD * DK, N_HEAD * DV

    def tok(width, col, mirror):
        cb = col // width
        if mirror:
            return pl.BlockSpec((tr, width), lambda b, j: (b * tps + tps - 1 - j, cb))
        return pl.BlockSpec((tr, width), lambda b, j: (b * tps + j, cb))

    def pos(mirror):
        if mirror:
            return pl.BlockSpec((tr, DK), lambda b, j: (tps - 1 - j, 0))
        return pl.BlockSpec((tr, DK), lambda b, j: (j, 0))

    def side(mirror):
        return [tok(hk, C_RQ, mirror), tok(hk, C_RK, mirror), tok(hv, C_RV, mirror), pos(mirror), pos(mirror)]

    return pl.pallas_call(
        _ret_kernel,
        grid=(nseq, tps),
        in_specs=side(False) + side(True) + [
            _const_spec((N_HEAD, tr, tr)),
            _const_spec((tr, hk)), _const_spec((tr, hk)), _const_spec((tr, hk)), _const_spec((tr, hk)),
            _const_spec((2, 1, hk)),
        ],
        out_specs=[
            pl.BlockSpec((tr, hv), lambda b, j: (b * tps + j, 0)),
            pl.BlockSpec((tr, hv), lambda b, j: (b * tps + tps - 1 - j, 0)),
        ],
        out_shape=[jax.ShapeDtypeStruct((t, hv), BF16)] * 2,
        scratch_shapes=[pltpu.VMEM((2 * N_HEAD, DK, DV), F32)],
        compiler_params=_cparams(("parallel", "arbitrary")),
        name="retention",
    )(proj, proj, proj, cos_t, sin_t, proj, proj, proj, cos_t, sin_t,
      dsum, xi_f, zeta_f, xi_b, zeta_b, gch)


def _axial_rope(x, cos, sin_signed, first_quarter):
    partner = jnp.where(first_quarter, pltpu.roll(x, 96, axis=1), pltpu.roll(x, 32, axis=1))
    return x * cos + partner * sin_signed


def _attn_prep_kernel(q_ref, k_ref, v_ref, gq_ref, gk_ref, cos_ref, sin_ref, qo_ref, kt_ref, vx_ref):
    cos, sin = cos_ref[...], sin_ref[...]
    fq = (lax.broadcasted_iota(jnp.int32, cos.shape, 1) % 64) < 32
    scale = ATT_HD ** -0.5 * LOG2E
    for h in range(ATT_HQ):
        sl = slice(h * ATT_HD, (h + 1) * ATT_HD)
        x = _rms(q_ref[:, sl].astype(F32), gq_ref[...])
        qo_ref[:, sl] = (_axial_rope(x, cos, sin, fq) * scale).astype(BF16)
    for h in range(ATT_HKV):
        sl = slice(h * ATT_HD, (h + 1) * ATT_HD)
        x = _rms(k_ref[:, sl].astype(F32), gk_ref[...])
        kt_ref[sl, :] = _axial_rope(x, cos, sin, fq).T.astype(BF16)
        vx_ref[:, 2 * h * ATT_HD:(2 * h + 1) * ATT_HD] = v_ref[:, sl]
        vx_ref[:, (2 * h + 1) * ATT_HD:(2 * h + 2) * ATT_HD] = jnp.ones((v_ref.shape[0], ATT_HD), BF16)


def _attn_prep(proj, gq, gk, cos_t, sin_t, seq):
    t = proj.shape[0]
    tm = TM_ATT_PREP
    tps = seq // tm
    nq, nkv = ATT_HQ * ATT_HD, ATT_HKV * ATT_HD
    return pl.pallas_call(
        _attn_prep_kernel,
        grid=(t // tm,),
        in_specs=[
            pl.BlockSpec((tm, nq), lambda i: (i, C_AQ // nq)),
            pl.BlockSpec((tm, nkv), lambda i: (i, C_AK // nkv)),
            pl.BlockSpec((tm, nkv), lambda i: (i, C_AV // nkv)),
            _const_spec((1, ATT_HD)),
            _const_spec((1, ATT_HD)),
            pl.BlockSpec((tm, ATT_HD), lambda i: (i % tps, 0)),
            pl.BlockSpec((tm, ATT_HD), lambda i: (i % tps, 0)),
        ],
        out_specs=[
            pl.BlockSpec((tm, nq), lambda i: (i, 0)),
            pl.BlockSpec((nkv, tm), lambda i: (0, i)),
            pl.BlockSpec((tm, 2 * nkv), lambda i: (i, 0)),
        ],
        out_shape=[
            jax.ShapeDtypeStruct((t, nq), BF16),
            jax.ShapeDtypeStruct((nkv, t), BF16),
            jax.ShapeDtypeStruct((t, 2 * nkv), BF16),
        ],
        compiler_params=_cparams(("parallel",)),
        name="attn_prep",
    )(proj, proj, proj, gq, gk, cos_t, sin_t)


def _flash_kernel(q_ref, kt_ref, v_ref, o_ref, *, seq):
    tq = TQ_ATT
    tk = min(TK_ATT, seq)
    rows = ATT_GROUP * tq
    nk = seq // tk
    q4 = jnp.concatenate([q_ref[:, h * ATT_HD:(h + 1) * ATT_HD] for h in range(ATT_GROUP)], axis=0)

    def scores(j):
        s = _dot(q4, kt_ref[:, j * tk:(j + 1) * tk])
        return s, jnp.max(s, axis=-1, keepdims=True)

    def update(j, s, smax, m, acc):
        m_new = jnp.maximum(m, smax)
        alpha = jnp.exp2(m - m_new)
        p = jnp.exp2(s - m_new).astype(BF16)
        return m_new, alpha * acc + _dot(p, v_ref[j * tk:(j + 1) * tk, :])

    m = jnp.full((rows, 1), -jnp.inf, F32)
    acc = jnp.zeros((rows, 2 * ATT_HD), F32)
    s, smax = scores(0)
    for j in range(nk):
        if j + 1 < nk:
            s_next, smax_next = scores(j + 1)
        m, acc = update(j, s, smax, m, acc)
        if j + 1 < nk:
            s, smax = s_next, smax_next
    out = acc[:, :ATT_HD] / acc[:, ATT_HD:]
    for h in range(ATT_GROUP):
        o_ref[:, h * ATT_HD:(h + 1) * ATT_HD] = out[h * tq:(h + 1) * tq].astype(BF16)


def _flash(q_r, k_t, v_ext, nseq, seq):
    t = q_r.shape[0]
    tq = TQ_ATT
    nq = seq // tq
    gw = ATT_GROUP * ATT_HD
    return pl.pallas_call(
        functools.partial(_flash_kernel, seq=seq),
        grid=(nseq, ATT_HKV, nq),
        in_specs=[
            pl.BlockSpec((tq, gw), lambda b, g, i: (b * nq + i, g)),
            pl.BlockSpec((ATT_HD, seq), lambda b, g, i: (g, b)),
            pl.BlockSpec((seq, 2 * ATT_HD), lambda b, g, i: (b, g)),
        ],
        out_specs=pl.BlockSpec((tq, gw), lambda b, g, i: (b * nq + i, g)),
        out_shape=jax.ShapeDtypeStruct((t, ATT_HQ * ATT_HD), BF16),
        compiler_params=_cparams(("parallel", "parallel", "arbitrary")),
        name="flash_gqa",
    )(q_r, k_t, v_ext)


def _merge_kernel(x_ref, gof_ref, gob_ref, gz_ref, roa_ref, rob_ref, rg_ref, att_ref,
                  g0_ref, g1_ref, g2_ref, gn_ref, rn_ref, wg_ref, wr_ref, wa_ref, wo_ref, o_ref):
    go = gof_ref[...].astype(F32) + gob_ref[...].astype(F32)
    ro = roa_ref[...].astype(F32) + rob_ref[...].astype(F32)
    gz = gz_ref[...].astype(F32)
    rg = rg_ref[...].astype(F32)
    ga_parts, rb_parts = [], []
    for h in range(N_HEAD):
        vs = slice(h * DV, (h + 1) * DV)
        gh = go[:, vs]
        gh = gh * lax.rsqrt(jnp.mean(gh * gh, axis=-1, keepdims=True) + EPS) * gn_ref[...]
        ga_parts.append((gh * _silu(gz[:, vs])).astype(BF16))
        rh = ro[:, vs]
        mu = jnp.mean(rh, axis=-1, keepdims=True)
        cen = rh - mu
        var = jnp.mean(cen * cen, axis=-1, keepdims=True)
        rh = cen * lax.rsqrt(var + EPS) * rn_ref[:, vs]
        rb_parts.append((_silu(rg[:, vs]) * rh).astype(BF16))
    branch_a = _dot(jnp.concatenate(ga_parts, axis=1), wg_ref[...])
    branch_b = _dot(jnp.concatenate(rb_parts, axis=1), wr_ref[...])
    branch_c = _dot(att_ref[...], wa_ref[...])
    merged = (_sigmoid(g0_ref[...].astype(F32)) * branch_a
              + _sigmoid(g1_ref[...].astype(F32)) * branch_b
              + _sigmoid(g2_ref[...].astype(F32)) * branch_c)
    o_ref[...] = x_ref[...] + _dot(merged.astype(BF16), wo_ref[...])


def _merge(x, go_f, go_b, ro_a, ro_b, att, proj, gdn_norm, ret_norm, wg, wr, wa, wo):
    t = x.shape[0]
    tm = TM_MERGE
    d = D_MODEL

    def tok(col=0):
        return pl.BlockSpec((tm, d), lambda i: (i, col // d))

    return pl.pallas_call(
        _merge_kernel,
        grid=(t // tm,),
        in_specs=[
            tok(), tok(), tok(), tok(C_GZ), tok(), tok(), tok(C_RG), tok(),
            tok(C_GATE), tok(C_GATE + d), tok(C_GATE + 2 * d),
            _const_spec((1, DV)), _const_spec((1, d)),
            _const_spec((d, d)), _const_spec((d, d)), _const_spec((d, d)), _const_spec((d, d)),
        ],
        out_specs=tok(),
        out_shape=jax.ShapeDtypeStruct((t, d), F32),
        compiler_params=_cparams(("parallel",)),
        name="merge",
    )(x, go_f, go_b, proj, ro_a, ro_b, proj, att, proj, proj, proj,
      gdn_norm, ret_norm, wg, wr, wa, wo)


def _rope_tables(seq):
    pos = jnp.arange(seq, dtype=F32)
    inv = ROPE_THETA ** (-jnp.arange(0, DK, 2, dtype=F32) / DK)
    ang = pos[:, None] * inv[None, :]
    c, s = jnp.cos(ang), jnp.sin(ang)
    ret_cos = jnp.concatenate([c, c], axis=1)
    ret_sin = jnp.concatenate([-s, s], axis=1)
    half = ATT_HD // 2
    inv_a = ROPE_THETA ** (-jnp.arange(0, half, 2, dtype=F32) / half)
    rows = (jnp.arange(seq) // GRID_W).astype(F32)
    cols = (jnp.arange(seq) % GRID_W).astype(F32)
    ar, ac = rows[:, None] * inv_a[None, :], cols[:, None] * inv_a[None, :]
    cr, sr, cc, sc = jnp.cos(ar), jnp.sin(ar), jnp.cos(ac), jnp.sin(ac)
    att_cos = jnp.concatenate([cr, cr, cc, cc], axis=1)
    att_sin = jnp.concatenate([-sr, sr, -sc, sc], axis=1)
    return ret_cos, ret_sin, att_cos, att_sin


def _ret_tables(decay_logit):
    c = TR_RET
    lg = jax.nn.log_sigmoid(decay_logit.astype(F32))
    idx = jnp.arange(c, dtype=F32)
    rel = idx[:, None] - idx[None, :]
    lf, lb = lg[0][:, None, None], lg[1][:, None, None]
    dsum = (jnp.exp(jnp.where(rel >= 0, rel * lf, -jnp.inf))
            + jnp.exp(jnp.where(rel <= 0, -rel * lb, -jnp.inf)))

    def lanes(tab):
        return jnp.repeat(tab.T, DK, axis=1)

    xi_f = lanes(jnp.exp((idx + 1.0)[None, :] * lg[0][:, None]))
    zeta_f = lanes(jnp.exp((c - 1.0 - idx)[None, :] * lg[0][:, None]))
    xi_b = lanes(jnp.exp((c - idx)[None, :] * lg[1][:, None]))
    zeta_b = lanes(jnp.exp(idx[None, :] * lg[1][:, None]))
    gch = jnp.repeat(jnp.exp(c * lg), DK, axis=1)[:, None, :]
    return dsum, xi_f, zeta_f, xi_b, zeta_b, gch


def _layer_params(l, p):
    w_in = p["w_in"][l]
    n_gdn = 2 * N_HEAD * DK + 2 * N_HEAD * DV
    rest = w_in[:, n_gdn + 16:]
    n_ret = 2 * N_HEAD * DK + 2 * N_HEAD * DV
    n_att = ATT_HQ * ATT_HD + 2 * ATT_HKV * ATT_HD
    w_main = jnp.concatenate(
        [w_in[:, :n_gdn], rest[:, n_ret + n_att:], rest[:, :n_ret + n_att]], axis=1).astype(BF16)
    w_small = jnp.pad(w_in[:, n_gdn:n_gdn + 16], ((0, 0), (0, 112))).astype(BF16)
    conv_w = jnp.pad(p["gdn_conv"][l].astype(F32), ((0, 8 - CONV_W), (0, 0)))
    alog_row = jnp.pad(p["gdn_A_log"][l].astype(F32).reshape(1, 8), ((0, 0), (8, 112)))
    bias_row = jnp.pad(p["gdn_dt_bias"][l].astype(F32).reshape(1, 8), ((0, 0), (8, 112)))
    return dict(
        ffn1=(p["ffn1_norm"][l][None, :], p["ffn1_w1"][l].astype(BF16),
              p["ffn1_w3"][l].astype(BF16), p["ffn1_w2"][l].astype(BF16)),
        ffn2=(p["ffn2_norm"][l][None, :], p["ffn2_w1"][l].astype(BF16),
              p["ffn2_w3"][l].astype(BF16), p["ffn2_w2"][l].astype(BF16)),
        mix_norm=p["mix_norm"][l][None, :], w_main=w_main, w_small=w_small,
        conv_w=conv_w, alog_row=alog_row, bias_row=bias_row,
        ret_tabs=_ret_tables(p["ret_decay_logit"][l]),
        gq=p["attn_q_norm"][l][None, :], gk=p["attn_k_norm"][l][None, :],
        gdn_norm=p["gdn_norm"][l][None, :], ret_norm=p["ret_norm"][l][None, :],
        wg=p["w_branch_gdn"][l].astype(BF16), wr=p["w_branch_ret"][l].astype(BF16),
        wa=p["w_branch_attn"][l].astype(BF16), wo=p["w_out"][l].astype(BF16),
    )


def _encoder_layer(x, lp, tabs, nseq, seq):
    ret_cos, ret_sin, att_cos, att_sin = tabs
    x = _ffn(x, *lp["ffn1"])
    proj, small = _inproj(x, lp["mix_norm"], lp["w_main"], lp["w_small"])
    u, w, qkm, qd, kd, gt = _gdn_prep(proj, small, lp["conv_w"], lp["alog_row"], lp["bias_row"], seq)
    go_f, go_b = _gdn_scan(u, w, qkm, qd, kd, gt, nseq, seq)
    ro_a, ro_b = _retention(proj, ret_cos, ret_sin, *lp["ret_tabs"], nseq, seq)
    q_r, k_t, v_ext = _attn_prep(proj, lp["gq"], lp["gk"], att_cos, att_sin, seq)
    att = _flash(q_r, k_t, v_ext, nseq, seq)
    x = _merge(x, go_f, go_b, ro_a, ro_b, att, proj, lp["gdn_norm"], lp["ret_norm"],
               lp["wg"], lp["wr"], lp["wa"], lp["wo"])
    return _ffn(x, *lp["ffn2"])


def _trunk(x3, layers, tabs):
    nseq, seq, d = x3.shape
    x = x3.reshape(nseq * seq, d)
    for lp in layers:
        x = _encoder_layer(x, lp, tabs, nseq, seq)
    return x.reshape(nseq, seq, d)


def kernel(x_prompt, x_sample, ffn1_norm, ffn1_w1, ffn1_w3, ffn1_w2, mix_norm, w_in, gdn_conv, gdn_A_log, gdn_dt_bias, gdn_norm, ret_decay_logit, ret_norm, attn_q_norm, attn_k_norm, w_branch_gdn, w_branch_ret, w_branch_attn, w_out, ffn2_norm, ffn2_w1, ffn2_w3, ffn2_w2):
    p = dict(ffn1_norm=ffn1_norm, ffn1_w1=ffn1_w1, ffn1_w3=ffn1_w3, ffn1_w2=ffn1_w2, mix_norm=mix_norm,
             w_in=w_in, gdn_conv=gdn_conv, gdn_A_log=gdn_A_log, gdn_dt_bias=gdn_dt_bias, gdn_norm=gdn_norm,
             ret_decay_logit=ret_decay_logit, ret_norm=ret_norm, attn_q_norm=attn_q_norm,
             attn_k_norm=attn_k_norm, w_branch_gdn=w_branch_gdn, w_branch_ret=w_branch_ret,
             w_branch_attn=w_branch_attn, w_out=w_out, ffn2_norm=ffn2_norm, ffn2_w1=ffn2_w1,
             ffn2_w3=ffn2_w3, ffn2_w2=ffn2_w2)
    depth = w_in.shape[0]
    layers = [_layer_params(l, p) for l in range(depth)]
    assert x_prompt.shape[1] == x_sample.shape[1]
    tabs = _rope_tables(x_prompt.shape[1])
    return (_trunk(x_prompt, layers, tabs), _trunk(x_sample, layers, tabs))
```

```python
import functools
import math

import jax
import jax.numpy as jnp
from jax import lax
from jax.experimental import pallas as pl
from jax.experimental.pallas import tpu as pltpu

F32 = jnp.float32
BF16 = jnp.bfloat16

D_MODEL = 1024
D_FF = 2816
EPS = 1e-6
ROPE_THETA = 10000.0
GRID_W = 64
LOG2E = math.log2(math.e)

N_HEAD = 4
DK = 128
DV = 256
CONV_W = 5
CHUNK = 64
ATT_HQ = 8
ATT_HKV = 2
ATT_HD = 128
ATT_GROUP = ATT_HQ // ATT_HKV

C_GQKV = 0
C_GZ = 2048
C_GATE = 3072
C_RQ = 6144
C_RK = 6656
C_RV = 7168
C_RG = 8192
C_AQ = 9216
C_AK = 10240
C_AV = 10496
N_PROJ = 10752

VMEM_LIMIT = 56 * 1024 * 1024

TM_FFN = 512
FF_CHUNKS = ((0, 768), (768, 1536), (1536, 2304), (2304, 2816))
TM_PROJ = 1024
TN_PROJ = 1536
TM_GDN = 256
TR_RET = 256
TM_ATT_PREP = 256
TQ_ATT = 256
TK_ATT = 1024
TM_MERGE = 512


def _cparams(sem):
    return pltpu.CompilerParams(dimension_semantics=sem, vmem_limit_bytes=VMEM_LIMIT)


def _dot(a, b):
    return jnp.dot(a, b, preferred_element_type=F32)


def _dot_nt(a, b):
    return lax.dot_general(a, b, (((1,), (1,)), ((), ())), preferred_element_type=F32)


def _dot_tn(a, b):
    return lax.dot_general(a, b, (((0,), (0,)), ((), ())), preferred_element_type=F32)


def _sigmoid(x):
    return 1.0 / (1.0 + jnp.exp2(x * (-LOG2E)))


def _silu(x):
    return x * _sigmoid(x)


def _rms(x, gain):
    ms = jnp.mean(x * x, axis=-1, keepdims=True)
    return x * lax.rsqrt(ms + EPS) * gain


def _const_spec(shape):
    n = len(shape)
    return pl.BlockSpec(shape, lambda *_: (0,) * n)


def _ffn_kernel(x_ref, g_ref, w1_ref, w3_ref, w2_ref, o_ref):
    x = x_ref[...]
    n = _rms(x, g_ref[...]).astype(BF16)
    acc = jnp.zeros(x.shape, F32)
    for lo, hi in FF_CHUNKS:
        h1 = _dot(n, w1_ref[:, lo:hi])
        h3 = _dot(n, w3_ref[:, lo:hi])
        a = (_silu(h1) * h3).astype(BF16)
        acc = acc + _dot(a, w2_ref[lo:hi, :])
    o_ref[...] = x + 0.5 * acc


def _ffn(x, gain, w1, w3, w2):
    t = x.shape[0]
    resident = dict(pipeline_mode=pl.Buffered(1))
    return pl.pallas_call(
        _ffn_kernel,
        grid=(t // TM_FFN,),
        in_specs=[
            pl.BlockSpec((TM_FFN, D_MODEL), lambda i: (i, 0)),
            pl.BlockSpec((1, D_MODEL), lambda i: (0, 0)),
            pl.BlockSpec((D_MODEL, D_FF), lambda i: (0, 0), **resident),
            pl.BlockSpec((D_MODEL, D_FF), lambda i: (0, 0), **resident),
            pl.BlockSpec((D_FF, D_MODEL), lambda i: (0, 0), **resident),
        ],
        out_specs=pl.BlockSpec((TM_FFN, D_MODEL), lambda i: (i, 0)),
        out_shape=jax.ShapeDtypeStruct((t, D_MODEL), F32),
        compiler_params=_cparams(("parallel",)),
        name="ffn",
    )(x, gain, w1, w3, w2)


def _inproj_kernel(x_ref, g_ref, w_ref, ws_ref, o_ref, os_ref, n_sc):
    @pl.when(pl.program_id(1) == 0)
    def _():
        n = _rms(x_ref[...], g_ref[...]).astype(BF16)
        n_sc[...] = n
        os_ref[...] = _dot(n, ws_ref[...])

    o_ref[...] = _dot(n_sc[...], w_ref[...]).astype(BF16)


def _inproj(x, gain, w_main, w_small):
    t = x.shape[0]
    return pl.pallas_call(
        _inproj_kernel,
        grid=(t // TM_PROJ, N_PROJ // TN_PROJ),
        in_specs=[
            pl.BlockSpec((TM_PROJ, D_MODEL), lambda i, j: (i, 0)),
            pl.BlockSpec((1, D_MODEL), lambda i, j: (0, 0)),
            pl.BlockSpec((D_MODEL, TN_PROJ), lambda i, j: (0, j)),
            pl.BlockSpec((D_MODEL, 128), lambda i, j: (0, 0)),
        ],
        out_specs=[
            pl.BlockSpec((TM_PROJ, TN_PROJ), lambda i, j: (i, j)),
            pl.BlockSpec((TM_PROJ, 128), lambda i, j: (i, 0)),
        ],
        out_shape=[
            jax.ShapeDtypeStruct((t, N_PROJ), BF16),
            jax.ShapeDtypeStruct((t, 128), F32),
        ],
        scratch_shapes=[pltpu.VMEM((TM_PROJ, D_MODEL), BF16)],
        compiler_params=_cparams(("parallel", "arbitrary")),
        name="inproj",
    )(x, gain, w_main, w_small)


def _split_hi_lo(x):
    hi = x.astype(BF16)
    lo = (x - hi.astype(F32)).astype(BF16)
    return hi, lo


def _dot01(m01, x):
    hi, lo = _split_hi_lo(x)
    return _dot(m01, hi) + _dot(m01, lo)


def _block_diag(p, mask):
    return jnp.where(mask, jnp.concatenate([p] * 4, axis=0), jnp.zeros((), p.dtype))


def _neumann_inverse(l_mats, eye4, bd_mask):
    qs = [-l for l in l_mats]
    accs = [eye4 + q for q in qs]
    qbs = [q.astype(BF16) for q in qs]
    qs = [_dot(qb, _block_diag(qb, bd_mask)) for qb in qbs]
    for _ in range(4):
        qbs = [q.astype(BF16) for q in qs]
        rs = [_dot(jnp.concatenate([a.astype(BF16), qb], axis=0), _block_diag(qb, bd_mask))
              for a, qb in zip(accs, qbs)]
        accs = [a + r[:CHUNK] for a, r in zip(accs, rs)]
        qs = [r[CHUNK:] for r in rs]
    return [(a + _dot(a.astype(BF16), _block_diag(q.astype(BF16), bd_mask))).astype(BF16)
            for a, q in zip(accs, qs)]


def _gdn_prep_kernel(prev_ref, cur_ref, next_ref, sm_ref, cw_ref, alog_ref, bias_ref,
                     u_ref, w_ref, qkm_ref, qd_ref, kd_ref, gt_ref, ext_sc, *, tiles_per_seq):
    tm = TM_GDN
    i = pl.program_id(0)
    jt = i % tiles_per_seq
    m_prev = (jt != 0).astype(F32)
    m_next = (jt != tiles_per_seq - 1).astype(F32)

    ext_sc[0:16, :] = prev_ref[...].astype(F32) * m_prev
    ext_sc[16:16 + tm, :] = cur_ref[...].astype(F32)
    ext_sc[16 + tm:32 + tm, :] = next_ref[...].astype(F32) * m_next
    conv = jnp.zeros((tm, 2 * N_HEAD * DK + N_HEAD * DV), F32)
    for w in range(CONV_W):
        conv = conv + ext_sc[14 + w:14 + w + tm, :] * cw_ref[w:w + 1, :]
    act = _silu(conv)

    nqk = N_HEAD * DK
    q_parts, k_parts = [], []
    for h in range(N_HEAD):
        qh = act[:, h * DK:(h + 1) * DK]
        kh = act[:, nqk + h * DK:nqk + (h + 1) * DK]
        q_parts.append(qh * lax.rsqrt(jnp.sum(qh * qh, axis=-1, keepdims=True) + EPS) * (DK ** -0.5))
        k_parts.append(kh * lax.rsqrt(jnp.sum(kh * kh, axis=-1, keepdims=True) + EPS))
    q_n = jnp.concatenate(q_parts, axis=1)
    k_n = jnp.concatenate(k_parts, axis=1)
    v = act[:, 2 * nqk:]

    sm = sm_ref[...]
    beta8 = _sigmoid(sm)
    zb = sm + bias_ref[...]
    softplus = jnp.maximum(zb, 0.0) + jnp.log(1.0 + jnp.exp(-jnp.abs(zb)))
    g8 = -jnp.exp(alog_ref[...]) * softplus

    def expand_mat(width, row0):
        r = lax.broadcasted_iota(jnp.int32, (128, 8 * width), 0)
        c = lax.broadcasted_iota(jnp.int32, (128, 8 * width), 1) // width
        return (r == c + row0).astype(BF16)

    e64b, e128b = expand_mat(64, 0), expand_mat(128, 0)
    e64g, e128g = expand_mat(64, 8), expand_mat(128, 8)
    beta8b = beta8.astype(BF16)
    beta64 = _dot(beta8b, e64b)
    beta128 = _dot(beta8b, e128b)
    g64 = _dot01_rhs(g8, e64g)
    g128 = _dot01_rhs(g8, e128g)

    ri = lax.broadcasted_iota(jnp.int32, (tm, tm), 0)
    ci = lax.broadcasted_iota(jnp.int32, (tm, tm), 1)
    same = (ri // CHUNK) == (ci // CHUNK)
    m_le = (same & (ci <= ri)).astype(BF16)
    m_ge = (same & (ci >= ri)).astype(BF16)
    m_lt = (same & (ci < ri)).astype(BF16)
    m_gt = (same & (ci > ri)).astype(BF16)

    tl = lax.broadcasted_iota(jnp.int32, (tm, 4 * CHUNK), 0) % CHUNK
    jl = lax.broadcasted_iota(jnp.int32, (tm, 4 * CHUNK), 1) % CHUNK
    hw = N_HEAD * CHUNK
    delta_f = _dot01(m_le, jnp.where(tl > jl, g64[:, :hw], 0.0))
    delta_b = _dot01(m_ge, jnp.where(tl < jl, g64[:, hw:], 0.0))
    hk = N_HEAD * DK
    gc_f = _dot01(m_le, g128[:, :hk])
    rest_f = _dot01(m_gt, g128[:, :hk])
    gc_b = _dot01(m_ge, g128[:, hk:])
    rest_b = _dot01(m_lt, g128[:, hk:])

    il = lax.broadcasted_iota(jnp.int32, (CHUNK, 4 * CHUNK), 0)
    jc = lax.broadcasted_iota(jnp.int32, (CHUNK, 4 * CHUNK), 1) % CHUNK
    eye4 = (il == jc).astype(F32)
    bd_mask = (lax.broadcasted_iota(jnp.int32, (4 * CHUNK, 4 * CHUNK), 0) // CHUNK
               == lax.broadcasted_iota(jnp.int32, (4 * CHUNK, 4 * CHUNK), 1) // CHUNK)
    bdk_mask = (lax.broadcasted_iota(jnp.int32, (4 * CHUNK, hk), 0) // CHUNK
                == lax.broadcasted_iota(jnp.int32, (4 * CHUNK, hk), 1) // DK)

    k_nb = k_n.astype(BF16)
    q_nb = q_n.astype(BF16)
    per_dir = []
    for d, gc, rest, delta, incl, strict in (
            (0, gc_f, rest_f, delta_f, il >= jc, il > jc),
            (1, gc_b, rest_b, delta_b, il <= jc, il < jc)):
        e_gc = jnp.exp(gc)
        qd_ref[d] = (q_n * e_gc).astype(BF16)
        kd_ref[d] = (k_n * jnp.exp(rest)).astype(BF16)
        b128 = beta128[:, d * hk:(d + 1) * hk]
        kbe = (k_n * b128 * e_gc).astype(BF16)
        tot = jnp.exp(gc + rest)
        for c in range(tm // CHUNK):
            gt_ref[d, c] = tot[c * CHUNK:c * CHUNK + 1, :]
        per_dir.append((d, delta, incl, strict, b128, kbe, beta64[:, d * hw:(d + 1) * hw]))

    l_mats, keys = [], []
    for c in range(tm // CHUNK):
        r0 = c * CHUNK
        k_c = k_nb[r0:r0 + CHUNK]
        bdk = jnp.where(bdk_mask, jnp.concatenate([k_c] * 4, axis=0), jnp.zeros((), BF16))
        gq = _dot_nt(jnp.concatenate([k_c, q_nb[r0:r0 + CHUNK]], axis=0), bdk)
        kk, qk = gq[:CHUNK], gq[CHUNK:]
        for d, delta, incl, strict, b128, kbe, b64 in per_dir:
            dm = jnp.where(incl, jnp.exp(delta[r0:r0 + CHUNK]), 0.0)
            l_mats.append(jnp.where(strict, b64[r0:r0 + CHUNK] * kk * dm, 0.0))
            qkm_ref[d, r0:r0 + CHUNK, :] = (qk * dm).astype(BF16)
            keys.append((r0, d, b128, kbe))
    t_alls = _neumann_inverse(l_mats, eye4, bd_mask)
    for (r0, d, b128, kbe), t_all in zip(keys, t_alls):
        for h in range(N_HEAD):
            bh = b128[r0:r0 + CHUNK, h * DK:(h + 1) * DK]
            vh = v[r0:r0 + CHUNK, h * DV:(h + 1) * DV]
            vb = jnp.concatenate([vh[:, :DK] * bh, vh[:, DK:] * bh], axis=1).astype(BF16)
            rhs = jnp.concatenate([vb, kbe[r0:r0 + CHUNK, h * DK:(h + 1) * DK]], axis=1)
            uw = _dot(t_all[:, h * CHUNK:(h + 1) * CHUNK], rhs)
            u_ref[d, r0:r0 + CHUNK, h * DV:(h + 1) * DV] = uw[:, :DV].astype(BF16)
            w_ref[d, r0:r0 + CHUNK, h * DK:(h + 1) * DK] = uw[:, DV:].astype(BF16)


def _dot01_rhs(x, m01):
    hi, lo = _split_hi_lo(x)
    return _dot(hi, m01) + _dot(lo, m01)


def _gdn_prep(proj, small, conv_w, alog_row, bias_row, seq):
    t = proj.shape[0]
    tm = TM_GDN
    tps = seq // tm
    nqkv = 2 * N_HEAD * DK + N_HEAD * DV
    hb = tm // 16
    last16 = t // 16 - 1
    kern = functools.partial(_gdn_prep_kernel, tiles_per_seq=tps)
    return pl.pallas_call(
        kern,
        grid=(t // tm,),
        in_specs=[
            pl.BlockSpec((16, nqkv), lambda i: (jnp.maximum(i * hb - 1, 0), 0)),
            pl.BlockSpec((tm, nqkv), lambda i: (i, 0)),
            pl.BlockSpec((16, nqkv), lambda i: (jnp.minimum((i + 1) * hb, last16), 0)),
            pl.BlockSpec((tm, 128), lambda i: (i, 0)),
            _const_spec((8, nqkv)),
            _const_spec((1, 128)),
            _const_spec((1, 128)),
        ],
        out_specs=[
            pl.BlockSpec((2, tm, N_HEAD * DV), lambda i: (0, i, 0)),
            pl.BlockSpec((2, tm, N_HEAD * DK), lambda i: (0, i, 0)),
            pl.BlockSpec((2, tm, N_HEAD * CHUNK), lambda i: (0, i, 0)),
            pl.BlockSpec((2, tm, N_HEAD * DK), lambda i: (0, i, 0)),
            pl.BlockSpec((2, tm, N_HEAD * DK), lambda i: (0, i, 0)),
            pl.BlockSpec((2, tm // CHUNK, 1, N_HEAD * DK), lambda i: (0, i, 0, 0)),
        ],
        out_shape=[
            jax.ShapeDtypeStruct((2, t, N_HEAD * DV), BF16),
            jax.ShapeDtypeStruct((2, t, N_HEAD * DK), BF16),
            jax.ShapeDtypeStruct((2, t, N_HEAD * CHUNK), BF16),
            jax.ShapeDtypeStruct((2, t, N_HEAD * DK), BF16),
            jax.ShapeDtypeStruct((2, t, N_HEAD * DK), BF16),
            jax.ShapeDtypeStruct((2, t // CHUNK, 1, N_HEAD * DK), F32),
        ],
        scratch_shapes=[pltpu.VMEM((tm + 32, nqkv), F32)],
        compiler_params=_cparams(("parallel",)),
        name="gdn_prep",
    )(proj, proj, proj, small, conv_w, alog_row, bias_row)


def _gdn_scan_kernel(uf_ref, wf_ref, qkf_ref, qdf_ref, kdf_ref, gtf_ref,
                     ub_ref, wb_ref, qkb_ref, qdb_ref, kdb_ref, gtb_ref,
                     of_ref, ob_ref, s_sc):
    @pl.when(pl.program_id(1) == 0)
    def _():
        s_sc[...] = jnp.zeros(s_sc.shape, F32)

    nch = TM_GDN // CHUNK
    dirs = (
        (0, uf_ref, wf_ref, qkf_ref, qdf_ref, kdf_ref, gtf_ref, of_ref, range(nch)),
        (1, ub_ref, wb_ref, qkb_ref, qdb_ref, kdb_ref, gtb_ref, ob_ref, range(nch - 1, -1, -1)),
    )
    chains = [(dr, h) for dr in dirs for h in range(N_HEAD)]
    states = [s_sc[d * N_HEAD + h] for (d, *_), h in chains]
    for step in range(nch):
        wss = []
        for ((d, u_ref, w_ref, qk_ref, qd_ref, kd_ref, gt_ref, o_ref, order), h), s in zip(chains, states):
            r0 = order[step] * CHUNK
            wq = jnp.concatenate([w_ref[0, r0:r0 + CHUNK, h * DK:(h + 1) * DK],
                                  qd_ref[0, r0:r0 + CHUNK, h * DK:(h + 1) * DK]], axis=0)
            wss.append(_dot(wq, s.astype(BF16)))
        vbs = []
        for ((d, u_ref, w_ref, qk_ref, qd_ref, kd_ref, gt_ref, o_ref, order), h), ws in zip(chains, wss):
            r0 = order[step] * CHUNK
            v_new = u_ref[0, r0:r0 + CHUNK, h * DV:(h + 1) * DV].astype(F32) - ws[:CHUNK]
            vbs.append(v_new.astype(BF16))
        new_states = []
        for ((d, u_ref, w_ref, qk_ref, qd_ref, kd_ref, gt_ref, o_ref, order), h), s, vb in zip(chains, states, vbs):
            c = order[step]
            r0 = c * CHUNK
            gt = gt_ref[0, c, :, h * DK:(h + 1) * DK]
            gtb = jnp.concatenate([gt, gt], axis=1)
            new_states.append(s * gtb + _dot_tn(kd_ref[0, r0:r0 + CHUNK, h * DK:(h + 1) * DK], vb))
        for ((d, u_ref, w_ref, qk_ref, qd_ref, kd_ref, gt_ref, o_ref, order), h), ws, vb in zip(chains, wss, vbs):
            r0 = order[step] * CHUNK
            o = ws[CHUNK:] + _dot(qk_ref[0, r0:r0 + CHUNK, h * CHUNK:(h + 1) * CHUNK], vb)
            o_ref[r0:r0 + CHUNK, h * DV:(h + 1) * DV] = o.astype(BF16)
        states = new_states
    for ((d, *_), h), s in zip(chains, states):
        s_sc[d * N_HEAD + h] = s


def _gdn_scan(u, w, qkm, qd, kd, gt, nseq, seq):
    t = u.shape[1]
    tm = TM_GDN
    tps = seq // tm
    nch = tm // CHUNK

    def fwd(width):
        return pl.BlockSpec((1, tm, width), lambda b, j: (0, b * tps + j, 0))

    def bwd(width):
        return pl.BlockSpec((1, tm, width), lambda b, j: (1, b * tps + tps - 1 - j, 0))

    gt_f = pl.BlockSpec((1, nch, 1, N_HEAD * DK), lambda b, j: (0, b * tps + j, 0, 0))
    gt_b = pl.BlockSpec((1, nch, 1, N_HEAD * DK), lambda b, j: (1, b * tps + tps - 1 - j, 0, 0))
    widths = (N_HEAD * DV, N_HEAD * DK, N_HEAD * CHUNK, N_HEAD * DK, N_HEAD * DK)
    return pl.pallas_call(
        _gdn_scan_kernel,
        grid=(nseq, tps),
        in_specs=[fwd(x) for x in widths] + [gt_f] + [bwd(x) for x in widths] + [gt_b],
        out_specs=[
            pl.BlockSpec((tm, N_HEAD * DV), lambda b, j: (b * tps + j, 0)),
            pl.BlockSpec((tm, N_HEAD * DV), lambda b, j: (b * tps + tps - 1 - j, 0)),
        ],
        out_shape=[jax.ShapeDtypeStruct((t, N_HEAD * DV), BF16)] * 2,
        scratch_shapes=[pltpu.VMEM((2 * N_HEAD, DK, DV), F32)],
        compiler_params=_cparams(("parallel", "arbitrary")),
        name="gdn_scan",
    )(u, w, qkm, qd, kd, gt, u, w, qkm, qd, kd, gt)


def _rope_half(x, cos, sin_signed):
    return x * cos + pltpu.roll(x, 64, axis=1) * sin_signed


def _ret_kernel(qf_ref, kf_ref, vf_ref, cosf_ref, sinf_ref,
                qb_ref, kb_ref, vb_ref, cosb_ref, sinb_ref,
                dsum_ref, xif_ref, zf_ref, xib_ref, zb_ref, gch_ref,
                oa_ref, ob_ref, r_sc):
    @pl.when(pl.program_id(1) == 0)
    def _():
        r_sc[...] = jnp.zeros(r_sc.shape, F32)

    scale = DK ** -0.5
    cosf, sinf = cosf_ref[...], sinf_ref[...]
    cosb, sinb = cosb_ref[...], sinb_ref[...]
    for h in range(N_HEAD):
        sl = slice(h * DK, (h + 1) * DK)
        vs = slice(h * DV, (h + 1) * DV)
        q = _rope_half(qf_ref[:, sl].astype(F32), cosf, sinf)
        k = _rope_half(kf_ref[:, sl].astype(F32), cosf, sinf) * scale
        v = vf_ref[:, vs]
        s = (_dot_nt(q.astype(BF16), k.astype(BF16)) * dsum_ref[h]).astype(BF16)
        o = _dot(s, v)
        rf = r_sc[h]
        o = o + _dot((q * xif_ref[:, sl]).astype(BF16), rf.astype(BF16))
        gf = gch_ref[0, :, sl]
        r_sc[h] = rf * jnp.concatenate([gf, gf], axis=1) + _dot_tn((k * zf_ref[:, sl]).astype(BF16), v)
        oa_ref[:, vs] = o.astype(BF16)
        q2 = _rope_half(qb_ref[:, sl].astype(F32), cosb, sinb)
        k2 = _rope_half(kb_ref[:, sl].astype(F32), cosb, sinb) * scale
        v2 = vb_ref[:, vs]
        rb = r_sc[N_HEAD + h]
        ob_ref[:, vs] = _dot((q2 * xib_ref[:, sl]).astype(BF16), rb.astype(BF16)).astype(BF16)
        gb = gch_ref[1, :, sl]
        r_sc[N_HEAD + h] = rb * jnp.concatenate([gb, gb], axis=1) + _dot_tn((k2 * zb_ref[:, sl]).astype(BF16), v2)


def _retention(proj, cos_t, sin_t, dsum, xi_f, zeta_f, xi_b, zeta_b, gch, nseq, seq):
    t = proj.shape[0]
    tr = TR_RET
    tps = seq // tr
    hk, hv = N_HEAD * DK, N_HEAD * DV

    def tok(width, col, mirror):
        cb = col // width
        if mirror:
            return pl.BlockSpec((tr, width), lambda b, j: (b * tps + tps - 1 - j, cb))
        return pl.BlockSpec((tr, width), lambda b, j: (b * tps + j, cb))

    def pos(mirror):
        if mirror:
            return pl.BlockSpec((tr, DK), lambda b, j: (tps - 1 - j, 0))
        return pl.BlockSpec((tr, DK), lambda b, j: (j, 0))

    def side(mirror):
        return [tok(hk, C_RQ, mirror), tok(hk, C_RK, mirror), tok(hv, C_RV, mirror), pos(mirror), pos(mirror)]

    return pl.pallas_call(
        _ret_kernel,
        grid=(nseq, tps),
        in_specs=side(False) + side(True) + [
            _const_spec((N_HEAD, tr, tr)),
            _const_spec((tr, hk)), _const_spec((tr, hk)), _const_spec((tr, hk)), _const_spec((tr, hk)),
            _const_spec((2, 1, hk)),
        ],
        out_specs=[
            pl.BlockSpec((tr, hv), lambda b, j: (b * tps + j, 0)),
            pl.BlockSpec((tr, hv), lambda b, j: (b * tps + tps - 1 - j, 0)),
        ],
        out_shape=[jax.ShapeDtypeStruct((t, hv), BF16)] * 2,
        scratch_shapes=[pltpu.VMEM((2 * N_HEAD, DK, DV), F32)],
        compiler_params=_cparams(("parallel", "arbitrary")),
        name="retention",
    )(proj, proj, proj, cos_t, sin_t, proj, proj, proj, cos_t, sin_t,
      dsum, xi_f, zeta_f, xi_b, zeta_b, gch)


def _axial_rope(x, cos, sin_signed, first_quarter):
    partner = jnp.where(first_quarter, pltpu.roll(x, 96, axis=1), pltpu.roll(x, 32, axis=1))
    return x * cos + partner * sin_signed


def _attn_prep_kernel(q_ref, k_ref, v_ref, gq_ref, gk_ref, cos_ref, sin_ref, qo_ref, kt_ref, vx_ref):
    cos, sin = cos_ref[...], sin_ref[...]
    fq = (lax.broadcasted_iota(jnp.int32, cos.shape, 1) % 64) < 32
    scale = ATT_HD ** -0.5 * LOG2E
    heads = [(q_ref, h, gq_ref) for h in range(ATT_HQ)] + [(k_ref, h, gk_ref) for h in range(ATT_HKV)]
    xs = [ref[:, h * ATT_HD:(h + 1) * ATT_HD].astype(F32) for ref, h, _ in heads]
    xs = [_rms(x, g[...]) for x, (_, _, g) in zip(xs, heads)]
    xs = [_axial_rope(x, cos, sin, fq) for x in xs]
    for h in range(ATT_HQ):
        qo_ref[:, h * ATT_HD:(h + 1) * ATT_HD] = (xs[h] * scale).astype(BF16)
    for h in range(ATT_HKV):
        sl = slice(h * ATT_HD, (h + 1) * ATT_HD)
        kt_ref[sl, :] = xs[ATT_HQ + h].T.astype(BF16)
        vx_ref[:, 2 * h * ATT_HD:(2 * h + 1) * ATT_HD] = v_ref[:, sl]
        vx_ref[:, (2 * h + 1) * ATT_HD:(2 * h + 2) * ATT_HD] = jnp.ones((v_ref.shape[0], ATT_HD), BF16)


def _attn_prep(proj, gq, gk, cos_t, sin_t, seq):
    t = proj.shape[0]
    tm = TM_ATT_PREP
    tps = seq // tm
    nq, nkv = ATT_HQ * ATT_HD, ATT_HKV * ATT_HD
    return pl.pallas_call(
        _attn_prep_kernel,
        grid=(t // tm,),
        in_specs=[
            pl.BlockSpec((tm, nq), lambda i: (i, C_AQ // nq)),
            pl.BlockSpec((tm, nkv), lambda i: (i, C_AK // nkv)),
            pl.BlockSpec((tm, nkv), lambda i: (i, C_AV // nkv)),
            _const_spec((1, ATT_HD)),
            _const_spec((1, ATT_HD)),
            pl.BlockSpec((tm, ATT_HD), lambda i: (i % tps, 0)),
            pl.BlockSpec((tm, ATT_HD), lambda i: (i % tps, 0)),
        ],
        out_specs=[
            pl.BlockSpec((tm, nq), lambda i: (i, 0)),
            pl.BlockSpec((nkv, tm), lambda i: (0, i)),
            pl.BlockSpec((tm, 2 * nkv), lambda i: (i, 0)),
        ],
        out_shape=[
            jax.ShapeDtypeStruct((t, nq), BF16),
            jax.ShapeDtypeStruct((nkv, t), BF16),
            jax.ShapeDtypeStruct((t, 2 * nkv), BF16),
        ],
        compiler_params=_cparams(("parallel",)),
        name="attn_prep",
    )(proj, proj, proj, gq, gk, cos_t, sin_t)


def _flash_kernel(q_ref, kt_ref, v_ref, o_ref, *, seq):
    tq = TQ_ATT
    tk = min(TK_ATT, seq)
    rows = ATT_GROUP * tq
    nk = seq // tk
    q4 = jnp.concatenate([q_ref[:, h * ATT_HD:(h + 1) * ATT_HD] for h in range(ATT_GROUP)], axis=0)

    def scores(j):
        s = _dot(q4, kt_ref[:, j * tk:(j + 1) * tk])
        return s, jnp.max(s, axis=-1, keepdims=True)

    def update(j, s, smax, m, acc):
        m_new = jnp.maximum(m, smax)
        alpha = jnp.exp2(m - m_new)
        p = jnp.exp2(s - m_new).astype(BF16)
        return m_new, alpha * acc + _dot(p, v_ref[j * tk:(j + 1) * tk, :])

    m = jnp.full((rows, 1), -jnp.inf, F32)
    acc = jnp.zeros((rows, 2 * ATT_HD), F32)
    s, smax = scores(0)
    for j in range(nk):
        if j + 1 < nk:
            s_next, smax_next = scores(j + 1)
        m, acc = update(j, s, smax, m, acc)
        if j + 1 < nk:
            s, smax = s_next, smax_next
    out = acc[:, :ATT_HD] / acc[:, ATT_HD:]
    for h in range(ATT_GROUP):
        o_ref[:, h * ATT_HD:(h + 1) * ATT_HD] = out[h * tq:(h + 1) * tq].astype(BF16)


def _flash(q_r, k_t, v_ext, nseq, seq):
    t = q_r.shape[0]
    tq = TQ_ATT
    nq = seq // tq
    gw = ATT_GROUP * ATT_HD
    return pl.pallas_call(
        functools.partial(_flash_kernel, seq=seq),
        grid=(nseq, ATT_HKV, nq),
        in_specs=[
            pl.BlockSpec((tq, gw), lambda b, g, i: (b * nq + i, g)),
            pl.BlockSpec((ATT_HD, seq), lambda b, g, i: (g, b)),
            pl.BlockSpec((seq, 2 * ATT_HD), lambda b, g, i: (b, g)),
        ],
        out_specs=pl.BlockSpec((tq, gw), lambda b, g, i: (b * nq + i, g)),
        out_shape=jax.ShapeDtypeStruct((t, ATT_HQ * ATT_HD), BF16),
        compiler_params=_cparams(("parallel", "parallel", "arbitrary")),
        name="flash_gqa",
    )(q_r, k_t, v_ext)


def _merge_kernel(x_ref, gof_ref, gob_ref, gz_ref, roa_ref, rob_ref, rg_ref, att_ref,
                  g0_ref, g1_ref, g2_ref, gn_ref, rn_ref, wg_ref, wr_ref, wa_ref, wo_ref, o_ref):
    go = gof_ref[...].astype(F32) + gob_ref[...].astype(F32)
    ro = roa_ref[...].astype(F32) + rob_ref[...].astype(F32)
    gz = gz_ref[...].astype(F32)
    rg = rg_ref[...].astype(F32)
    ga_parts, rb_parts = [], []
    for h in range(N_HEAD):
        vs = slice(h * DV, (h + 1) * DV)
        gh = go[:, vs]
        gh = gh * lax.rsqrt(jnp.mean(gh * gh, axis=-1, keepdims=True) + EPS) * gn_ref[...]
        ga_parts.append((gh * _silu(gz[:, vs])).astype(BF16))
        rh = ro[:, vs]
        mu = jnp.mean(rh, axis=-1, keepdims=True)
        cen = rh - mu
        var = jnp.mean(cen * cen, axis=-1, keepdims=True)
        rh = cen * lax.rsqrt(var + EPS) * rn_ref[:, vs]
        rb_parts.append((_silu(rg[:, vs]) * rh).astype(BF16))
    branch_a = _dot(jnp.concatenate(ga_parts, axis=1), wg_ref[...])
    branch_b = _dot(jnp.concatenate(rb_parts, axis=1), wr_ref[...])
    branch_c = _dot(att_ref[...], wa_ref[...])
    merged = (_sigmoid(g0_ref[...].astype(F32)) * branch_a
              + _sigmoid(g1_ref[...].astype(F32)) * branch_b
              + _sigmoid(g2_ref[...].astype(F32)) * branch_c)
    o_ref[...] = x_ref[...] + _dot(merged.astype(BF16), wo_ref[...])


def _merge(x, go_f, go_b, ro_a, ro_b, att, proj, gdn_norm, ret_norm, wg, wr, wa, wo):
    t = x.shape[0]
    tm = TM_MERGE
    d = D_MODEL

    def tok(col=0):
        return pl.BlockSpec((tm, d), lambda i: (i, col // d))

    return pl.pallas_call(
        _merge_kernel,
        grid=(t // tm,),
        in_specs=[
            tok(), tok(), tok(), tok(C_GZ), tok(), tok(), tok(C_RG), tok(),
            tok(C_GATE), tok(C_GATE + d), tok(C_GATE + 2 * d),
            _const_spec((1, DV)), _const_spec((1, d)),
        ] + [pl.BlockSpec((d, d), lambda i: (0, 0), pipeline_mode=pl.Buffered(1))] * 4,
        out_specs=tok(),
        out_shape=jax.ShapeDtypeStruct((t, d), F32),
        compiler_params=_cparams(("parallel",)),
        name="merge",
    )(x, go_f, go_b, proj, ro_a, ro_b, proj, att, proj, proj, proj,
      gdn_norm, ret_norm, wg, wr, wa, wo)


def _rope_tables(seq):
    pos = jnp.arange(seq, dtype=F32)
    inv = ROPE_THETA ** (-jnp.arange(0, DK, 2, dtype=F32) / DK)
    ang = pos[:, None] * inv[None, :]
    c, s = jnp.cos(ang), jnp.sin(ang)
    ret_cos = jnp.concatenate([c, c], axis=1)
    ret_sin = jnp.concatenate([-s, s], axis=1)
    half = ATT_HD // 2
    inv_a = ROPE_THETA ** (-jnp.arange(0, half, 2, dtype=F32) / half)
    rows = (jnp.arange(seq) // GRID_W).astype(F32)
    cols = (jnp.arange(seq) % GRID_W).astype(F32)
    ar, ac = rows[:, None] * inv_a[None, :], cols[:, None] * inv_a[None, :]
    cr, sr, cc, sc = jnp.cos(ar), jnp.sin(ar), jnp.cos(ac), jnp.sin(ac)
    att_cos = jnp.concatenate([cr, cr, cc, cc], axis=1)
    att_sin = jnp.concatenate([-sr, sr, -sc, sc], axis=1)
    return ret_cos, ret_sin, att_cos, att_sin


def _ret_tables(decay_logit):
    c = TR_RET
    lg = jax.nn.log_sigmoid(decay_logit.astype(F32))
    idx = jnp.arange(c, dtype=F32)
    rel = idx[:, None] - idx[None, :]
    lf, lb = lg[0][:, None, None], lg[1][:, None, None]
    dsum = (jnp.exp(jnp.where(rel >= 0, rel * lf, -jnp.inf))
            + jnp.exp(jnp.where(rel <= 0, -rel * lb, -jnp.inf)))

    def lanes(tab):
        return jnp.repeat(tab.T, DK, axis=1)

    xi_f = lanes(jnp.exp((idx + 1.0)[None, :] * lg[0][:, None]))
    zeta_f = lanes(jnp.exp((c - 1.0 - idx)[None, :] * lg[0][:, None]))
    xi_b = lanes(jnp.exp((c - idx)[None, :] * lg[1][:, None]))
    zeta_b = lanes(jnp.exp(idx[None, :] * lg[1][:, None]))
    gch = jnp.repeat(jnp.exp(c * lg), DK, axis=1)[:, None, :]
    return dsum, xi_f, zeta_f, xi_b, zeta_b, gch


def _layer_params(l, p):
    w_in = p["w_in"][l]
    n_gdn = 2 * N_HEAD * DK + 2 * N_HEAD * DV
    rest = w_in[:, n_gdn + 16:]
    n_ret = 2 * N_HEAD * DK + 2 * N_HEAD * DV
    n_att = ATT_HQ * ATT_HD + 2 * ATT_HKV * ATT_HD
    w_main = jnp.concatenate(
        [w_in[:, :n_gdn], rest[:, n_ret + n_att:], rest[:, :n_ret + n_att]], axis=1).astype(BF16)
    w_small = jnp.pad(w_in[:, n_gdn:n_gdn + 16], ((0, 0), (0, 112))).astype(BF16)
    conv_w = jnp.pad(p["gdn_conv"][l].astype(F32), ((0, 8 - CONV_W), (0, 0)))
    alog_row = jnp.pad(p["gdn_A_log"][l].astype(F32).reshape(1, 8), ((0, 0), (8, 112)))
    bias_row = jnp.pad(p["gdn_dt_bias"][l].astype(F32).reshape(1, 8), ((0, 0), (8, 112)))
    return dict(
        ffn1=(p["ffn1_norm"][l][None, :], p["ffn1_w1"][l].astype(BF16),
              p["ffn1_w3"][l].astype(BF16), p["ffn1_w2"][l].astype(BF16)),
        ffn2=(p["ffn2_norm"][l][None, :], p["ffn2_w1"][l].astype(BF16),
              p["ffn2_w3"][l].astype(BF16), p["ffn2_w2"][l].astype(BF16)),
        mix_norm=p["mix_norm"][l][None, :], w_main=w_main, w_small=w_small,
        conv_w=conv_w, alog_row=alog_row, bias_row=bias_row,
        ret_tabs=_ret_tables(p["ret_decay_logit"][l]),
        gq=p["attn_q_norm"][l][None, :], gk=p["attn_k_norm"][l][None, :],
        gdn_norm=p["gdn_norm"][l][None, :], ret_norm=p["ret_norm"][l][None, :],
        wg=p["w_branch_gdn"][l].astype(BF16), wr=p["w_branch_ret"][l].astype(BF16),
        wa=p["w_branch_attn"][l].astype(BF16), wo=p["w_out"][l].astype(BF16),
    )


def _encoder_layer(x, lp, tabs, nseq, seq):
    ret_cos, ret_sin, att_cos, att_sin = tabs
    x = _ffn(x, *lp["ffn1"])
    proj, small = _inproj(x, lp["mix_norm"], lp["w_main"], lp["w_small"])
    u, w, qkm, qd, kd, gt = _gdn_prep(proj, small, lp["conv_w"], lp["alog_row"], lp["bias_row"], seq)
    go_f, go_b = _gdn_scan(u, w, qkm, qd, kd, gt, nseq, seq)
    ro_a, ro_b = _retention(proj, ret_cos, ret_sin, *lp["ret_tabs"], nseq, seq)
    q_r, k_t, v_ext = _attn_prep(proj, lp["gq"], lp["gk"], att_cos, att_sin, seq)
    att = _flash(q_r, k_t, v_ext, nseq, seq)
    x = _merge(x, go_f, go_b, ro_a, ro_b, att, proj, lp["gdn_norm"], lp["ret_norm"],
               lp["wg"], lp["wr"], lp["wa"], lp["wo"])
    return _ffn(x, *lp["ffn2"])


def _trunk(x3, layers, tabs):
    nseq, seq, d = x3.shape
    x = x3.reshape(nseq * seq, d)
    for lp in layers:
        x = _encoder_layer(x, lp, tabs, nseq, seq)
    return x.reshape(nseq, seq, d)


def kernel(x_prompt, x_sample, ffn1_norm, ffn1_w1, ffn1_w3, ffn1_w2, mix_norm, w_in, gdn_conv, gdn_A_log, gdn_dt_bias, gdn_norm, ret_decay_logit, ret_norm, attn_q_norm, attn_k_norm, w_branch_gdn, w_branch_ret, w_branch_attn, w_out, ffn2_norm, ffn2_w1, ffn2_w3, ffn2_w2):
    p = dict(ffn1_norm=ffn1_norm, ffn1_w1=ffn1_w1, ffn1_w3=ffn1_w3, ffn1_w2=ffn1_w2, mix_norm=mix_norm,
             w_in=w_in, gdn_conv=gdn_conv, gdn_A_log=gdn_A_log, gdn_dt_bias=gdn_dt_bias, gdn_norm=gdn_norm,
             ret_decay_logit=ret_decay_logit, ret_norm=ret_norm, attn_q_norm=attn_q_norm,
             attn_k_norm=attn_k_norm, w_branch_gdn=w_branch_gdn, w_branch_ret=w_branch_ret,
             w_branch_attn=w_branch_attn, w_out=w_out, ffn2_norm=ffn2_norm, ffn2_w1=ffn2_w1,
             ffn2_w3=ffn2_w3, ffn2_w2=ffn2_w2)
    depth = w_in.shape[0]
    layers = [_layer_params(l, p) for l in range(depth)]
    assert x_prompt.shape[1] == x_sample.shape[1]
    tabs = _rope_tables(x_prompt.shape[1])
    return (_trunk(x_prompt, layers, tabs), _trunk(x_sample, layers, tabs))
```

```python
import functools
import math

import jax
import jax.numpy as jnp
import numpy as np
from jax import lax
from jax.experimental import pallas as pl
from jax.experimental.pallas import tpu as pltpu

F32 = jnp.float32
BF16 = jnp.bfloat16

D_MODEL = 1024
D_FF = 2816
EPS = 1e-6
ROPE_THETA = 10000.0
GRID_W = 64
LOG2E = math.log2(math.e)

N_HEAD = 4
DK = 128
DV = 256
CONV_W = 5
CHUNK = 64
ATT_HQ = 8
ATT_HKV = 2
ATT_HD = 128
ATT_GROUP = ATT_HQ // ATT_HKV

C_GQKV = 0
C_GZ = 2048
C_GATE = 3072
C_RQ = 6144
C_RK = 6656
C_RV = 7168
C_RG = 8192
C_AQ = 9216
C_AK = 10240
C_AV = 10496
N_PROJ = 10752

VMEM_LIMIT = 56 * 1024 * 1024

TM_FFN = 512
FF_CHUNKS = ((0, 768), (768, 1536), (1536, 2304), (2304, 2816))
TM_PROJ = 1024
TN_PROJ = 1536
TM_GDN = 256
TR_RET = 256
TM_ATT_PREP = 256
TQ_ATT = 256
TK_ATT = 2048
TM_MERGE = 512


def _cparams(sem):
    return pltpu.CompilerParams(dimension_semantics=sem, vmem_limit_bytes=VMEM_LIMIT)


def _dot(a, b):
    return jnp.dot(a, b, preferred_element_type=F32)


def _dot_nt(a, b):
    return lax.dot_general(a, b, (((1,), (1,)), ((), ())), preferred_element_type=F32)


def _dot_tn(a, b):
    return lax.dot_general(a, b, (((0,), (0,)), ((), ())), preferred_element_type=F32)


def _sigmoid(x):
    return 1.0 / (1.0 + jnp.exp2(x * (-LOG2E)))


def _silu(x):
    return x * _sigmoid(x)


def _rms(x, gain):
    ms = jnp.mean(x * x, axis=-1, keepdims=True)
    return x * lax.rsqrt(ms + EPS) * gain


def _const_spec(shape):
    n = len(shape)
    return pl.BlockSpec(shape, lambda *_: (0,) * n)


def _ffn_kernel(x_ref, g_ref, w1_ref, w3_ref, w2_ref, o_ref):
    x = x_ref[...]
    n = _rms(x, g_ref[...]).astype(BF16)
    acc = jnp.zeros(x.shape, F32)
    for lo, hi in FF_CHUNKS:
        h1 = _dot(n, w1_ref[:, lo:hi])
        h3 = _dot(n, w3_ref[:, lo:hi])
        a = (_silu(h1) * h3).astype(BF16)
        acc = acc + _dot(a, w2_ref[lo:hi, :])
    o_ref[...] = x + 0.5 * acc


def _ffn(x, gain, w1, w3, w2, l):
    t = x.shape[0]
    resident = dict(pipeline_mode=pl.Buffered(1))
    return pl.pallas_call(
        _ffn_kernel,
        grid=(t // TM_FFN,),
        in_specs=[
            pl.BlockSpec((TM_FFN, D_MODEL), lambda i: (i, 0)),
            pl.BlockSpec((1, D_MODEL), lambda i: (0, 0)),
            pl.BlockSpec((None, D_MODEL, D_FF), lambda i: (l, 0, 0), **resident),
            pl.BlockSpec((None, D_MODEL, D_FF), lambda i: (l, 0, 0), **resident),
            pl.BlockSpec((None, D_FF, D_MODEL), lambda i: (l, 0, 0), **resident),
        ],
        out_specs=pl.BlockSpec((TM_FFN, D_MODEL), lambda i: (i, 0)),
        out_shape=jax.ShapeDtypeStruct((t, D_MODEL), F32),
        compiler_params=_cparams(("parallel",)),
        name="ffn",
    )(x, gain, w1, w3, w2)


def _inproj_kernel(x_ref, g_ref, w_ref, ws_ref, o_ref, os_ref, n_sc):
    @pl.when(pl.program_id(1) == 0)
    def _():
        n = _rms(x_ref[...], g_ref[...]).astype(BF16)
        n_sc[...] = n
        os_ref[...] = _dot(n, ws_ref[...])

    o_ref[...] = _dot(n_sc[...], w_ref[...]).astype(BF16)


def _inproj(x, gain, w_main, w_small, l):
    t = x.shape[0]
    return pl.pallas_call(
        _inproj_kernel,
        grid=(t // TM_PROJ, N_PROJ // TN_PROJ),
        in_specs=[
            pl.BlockSpec((TM_PROJ, D_MODEL), lambda i, j: (i, 0)),
            pl.BlockSpec((1, D_MODEL), lambda i, j: (0, 0)),
            pl.BlockSpec((None, D_MODEL, TN_PROJ), lambda i, j: (l, 0, j)),
            pl.BlockSpec((None, D_MODEL, 128), lambda i, j: (l, 0, 0)),
        ],
        out_specs=[
            pl.BlockSpec((TM_PROJ, TN_PROJ), lambda i, j: (i, j)),
            pl.BlockSpec((TM_PROJ, 128), lambda i, j: (i, 0)),
        ],
        out_shape=[
            jax.ShapeDtypeStruct((t, N_PROJ), BF16),
            jax.ShapeDtypeStruct((t, 128), F32),
        ],
        scratch_shapes=[pltpu.VMEM((TM_PROJ, D_MODEL), BF16)],
        compiler_params=_cparams(("parallel", "arbitrary")),
        name="inproj",
    )(x, gain, w_main, w_small)


def _split_hi_lo(x):
    hi = x.astype(BF16)
    lo = (x - hi.astype(F32)).astype(BF16)
    return hi, lo


def _dot01(m01, x):
    hi, lo = _split_hi_lo(x)
    return _dot(m01, hi) + _dot(m01, lo)


def _block_diag(p, mask):
    return jnp.where(mask, jnp.concatenate([p] * 4, axis=0), jnp.zeros((), p.dtype))


def _neumann_inverse(l_mats, eye4, bd_mask):
    qs = [-l for l in l_mats]
    accs = [eye4 + q for q in qs]
    qbs = [q.astype(BF16) for q in qs]
    qs = [_dot(qb, _block_diag(qb, bd_mask)) for qb in qbs]
    for _ in range(4):
        qbs = [q.astype(BF16) for q in qs]
        rs = [_dot(jnp.concatenate([a.astype(BF16), qb], axis=0), _block_diag(qb, bd_mask))
              for a, qb in zip(accs, qbs)]
        accs = [a + r[:CHUNK] for a, r in zip(accs, rs)]
        qs = [r[CHUNK:] for r in rs]
    return [(a + _dot(a.astype(BF16), _block_diag(q.astype(BF16), bd_mask))).astype(BF16)
            for a, q in zip(accs, qs)]


def _gdn_prep_kernel(prev_ref, cur_ref, next_ref, sm_ref, cw_ref, alog_ref, bias_ref,
                     u_ref, w_ref, qkm_ref, qd_ref, kd_ref, gt_ref, ext_sc, *, tiles_per_seq):
    tm = TM_GDN
    i = pl.program_id(0)
    jt = i % tiles_per_seq
    m_prev = (jt != 0).astype(F32)
    m_next = (jt != tiles_per_seq - 1).astype(F32)

    ext_sc[0:16, :] = prev_ref[...].astype(F32) * m_prev
    ext_sc[16:16 + tm, :] = cur_ref[...].astype(F32)
    ext_sc[16 + tm:32 + tm, :] = next_ref[...].astype(F32) * m_next
    conv = jnp.zeros((tm, 2 * N_HEAD * DK + N_HEAD * DV), F32)
    for w in range(CONV_W):
        conv = conv + ext_sc[14 + w:14 + w + tm, :] * cw_ref[w:w + 1, :]
    act = _silu(conv)

    nqk = N_HEAD * DK
    q_parts, k_parts = [], []
    for h in range(N_HEAD):
        qh = act[:, h * DK:(h + 1) * DK]
        kh = act[:, nqk + h * DK:nqk + (h + 1) * DK]
        q_parts.append(qh * lax.rsqrt(jnp.sum(qh * qh, axis=-1, keepdims=True) + EPS) * (DK ** -0.5))
        k_parts.append(kh * lax.rsqrt(jnp.sum(kh * kh, axis=-1, keepdims=True) + EPS))
    q_n = jnp.concatenate(q_parts, axis=1)
    k_n = jnp.concatenate(k_parts, axis=1)
    v = act[:, 2 * nqk:]

    sm = sm_ref[...]
    beta8 = _sigmoid(sm)
    zb = sm + bias_ref[...]
    softplus = jnp.maximum(zb, 0.0) + jnp.log(1.0 + jnp.exp(-jnp.abs(zb)))
    g8 = -jnp.exp(alog_ref[...]) * softplus

    def expand_mat(width, row0):
        r = lax.broadcasted_iota(jnp.int32, (128, 8 * width), 0)
        c = lax.broadcasted_iota(jnp.int32, (128, 8 * width), 1) // width
        return (r == c + row0).astype(BF16)

    e64b, e128b = expand_mat(64, 0), expand_mat(128, 0)
    e64g, e128g = expand_mat(64, 8), expand_mat(128, 8)
    beta8b = beta8.astype(BF16)
    beta64 = _dot(beta8b, e64b)
    beta128 = _dot(beta8b, e128b)
    g64 = _dot01_rhs(g8, e64g)
    g128 = _dot01_rhs(g8, e128g)

    ri = lax.broadcasted_iota(jnp.int32, (tm, tm), 0)
    ci = lax.broadcasted_iota(jnp.int32, (tm, tm), 1)
    same = (ri // CHUNK) == (ci // CHUNK)
    m_le = (same & (ci <= ri)).astype(BF16)
    m_ge = (same & (ci >= ri)).astype(BF16)
    m_lt = (same & (ci < ri)).astype(BF16)
    m_gt = (same & (ci > ri)).astype(BF16)

    tl = lax.broadcasted_iota(jnp.int32, (tm, 4 * CHUNK), 0) % CHUNK
    jl = lax.broadcasted_iota(jnp.int32, (tm, 4 * CHUNK), 1) % CHUNK
    hw = N_HEAD * CHUNK
    delta_f = _dot01(m_le, jnp.where(tl > jl, g64[:, :hw], 0.0))
    delta_b = _dot01(m_ge, jnp.where(tl < jl, g64[:, hw:], 0.0))
    hk = N_HEAD * DK
    gc_f = _dot01(m_le, g128[:, :hk])
    rest_f = _dot01(m_gt, g128[:, :hk])
    gc_b = _dot01(m_ge, g128[:, hk:])
    rest_b = _dot01(m_lt, g128[:, hk:])

    il = lax.broadcasted_iota(jnp.int32, (CHUNK, 4 * CHUNK), 0)
    jc = lax.broadcasted_iota(jnp.int32, (CHUNK, 4 * CHUNK), 1) % CHUNK
    eye4 = (il == jc).astype(F32)
    bd_mask = (lax.broadcasted_iota(jnp.int32, (4 * CHUNK, 4 * CHUNK), 0) // CHUNK
               == lax.broadcasted_iota(jnp.int32, (4 * CHUNK, 4 * CHUNK), 1) // CHUNK)
    bdk_mask = (lax.broadcasted_iota(jnp.int32, (4 * CHUNK, hk), 0) // CHUNK
                == lax.broadcasted_iota(jnp.int32, (4 * CHUNK, hk), 1) // DK)

    k_nb = k_n.astype(BF16)
    q_nb = q_n.astype(BF16)
    per_dir = []
    for d, gc, rest, delta, incl, strict in (
            (0, gc_f, rest_f, delta_f, il >= jc, il > jc),
            (1, gc_b, rest_b, delta_b, il <= jc, il < jc)):
        e_gc = jnp.exp(gc)
        qd_ref[d] = (q_n * e_gc).astype(BF16)
        kd_ref[d] = (k_n * jnp.exp(rest)).astype(BF16)
        b128 = beta128[:, d * hk:(d + 1) * hk]
        kbe = (k_n * b128 * e_gc).astype(BF16)
        tot = jnp.exp(gc + rest)
        for c in range(tm // CHUNK):
            gt_ref[d, c] = tot[c * CHUNK:c * CHUNK + 1, :]
        per_dir.append((d, delta, incl, strict, b128, kbe, beta64[:, d * hw:(d + 1) * hw]))

    l_mats, keys = [], []
    for c in range(tm // CHUNK):
        r0 = c * CHUNK
        k_c = k_nb[r0:r0 + CHUNK]
        bdk = jnp.where(bdk_mask, jnp.concatenate([k_c] * 4, axis=0), jnp.zeros((), BF16))
        gq = _dot_nt(jnp.concatenate([k_c, q_nb[r0:r0 + CHUNK]], axis=0), bdk)
        kk, qk = gq[:CHUNK], gq[CHUNK:]
        for d, delta, incl, strict, b128, kbe, b64 in per_dir:
            dm = jnp.where(incl, jnp.exp(delta[r0:r0 + CHUNK]), 0.0)
            l_mats.append(jnp.where(strict, b64[r0:r0 + CHUNK] * kk * dm, 0.0))
            qkm_ref[d, r0:r0 + CHUNK, :] = (qk * dm).astype(BF16)
            keys.append((r0, d, b128, kbe))
    t_alls = _neumann_inverse(l_mats, eye4, bd_mask)
    for (r0, d, b128, kbe), t_all in zip(keys, t_alls):
        for h in range(N_HEAD):
            bh = b128[r0:r0 + CHUNK, h * DK:(h + 1) * DK]
            vh = v[r0:r0 + CHUNK, h * DV:(h + 1) * DV]
            vb = jnp.concatenate([vh[:, :DK] * bh, vh[:, DK:] * bh], axis=1).astype(BF16)
            rhs = jnp.concatenate([vb, kbe[r0:r0 + CHUNK, h * DK:(h + 1) * DK]], axis=1)
            uw = _dot(t_all[:, h * CHUNK:(h + 1) * CHUNK], rhs)
            u_ref[d, r0:r0 + CHUNK, h * DV:(h + 1) * DV] = uw[:, :DV].astype(BF16)
            w_ref[d, r0:r0 + CHUNK, h * DK:(h + 1) * DK] = uw[:, DV:].astype(BF16)


def _dot01_rhs(x, m01):
    hi, lo = _split_hi_lo(x)
    return _dot(hi, m01) + _dot(lo, m01)


def _gdn_prep(proj, small, conv_w, alog_row, bias_row, seq):
    t = proj.shape[0]
    tm = TM_GDN
    tps = seq // tm
    nqkv = 2 * N_HEAD * DK + N_HEAD * DV
    hb = tm // 16
    last16 = t // 16 - 1
    kern = functools.partial(_gdn_prep_kernel, tiles_per_seq=tps)
    return pl.pallas_call(
        kern,
        grid=(t // tm,),
        in_specs=[
            pl.BlockSpec((16, nqkv), lambda i: (jnp.maximum(i * hb - 1, 0), 0)),
            pl.BlockSpec((tm, nqkv), lambda i: (i, 0)),
            pl.BlockSpec((16, nqkv), lambda i: (jnp.minimum((i + 1) * hb, last16), 0)),
            pl.BlockSpec((tm, 128), lambda i: (i, 0)),
            _const_spec((8, nqkv)),
            _const_spec((1, 128)),
            _const_spec((1, 128)),
        ],
        out_specs=[
            pl.BlockSpec((2, tm, N_HEAD * DV), lambda i: (0, i, 0)),
            pl.BlockSpec((2, tm, N_HEAD * DK), lambda i: (0, i, 0)),
            pl.BlockSpec((2, tm, N_HEAD * CHUNK), lambda i: (0, i, 0)),
            pl.BlockSpec((2, tm, N_HEAD * DK), lambda i: (0, i, 0)),
            pl.BlockSpec((2, tm, N_HEAD * DK), lambda i: (0, i, 0)),
            pl.BlockSpec((2, tm // CHUNK, 1, N_HEAD * DK), lambda i: (0, i, 0, 0)),
        ],
        out_shape=[
            jax.ShapeDtypeStruct((2, t, N_HEAD * DV), BF16),
            jax.ShapeDtypeStruct((2, t, N_HEAD * DK), BF16),
            jax.ShapeDtypeStruct((2, t, N_HEAD * CHUNK), BF16),
            jax.ShapeDtypeStruct((2, t, N_HEAD * DK), BF16),
            jax.ShapeDtypeStruct((2, t, N_HEAD * DK), BF16),
            jax.ShapeDtypeStruct((2, t // CHUNK, 1, N_HEAD * DK), F32),
        ],
        scratch_shapes=[pltpu.VMEM((tm + 32, nqkv), F32)],
        compiler_params=_cparams(("parallel",)),
        name="gdn_prep",
    )(proj, proj, proj, small, conv_w, alog_row, bias_row)


def _gdn_scan_kernel(uf_ref, wf_ref, qkf_ref, qdf_ref, kdf_ref, gtf_ref,
                     ub_ref, wb_ref, qkb_ref, qdb_ref, kdb_ref, gtb_ref,
                     of_ref, ob_ref, s_sc):
    @pl.when(pl.program_id(1) == 0)
    def _():
        s_sc[...] = jnp.zeros(s_sc.shape, F32)

    nch = TM_GDN // CHUNK
    dirs = (
        (0, uf_ref, wf_ref, qkf_ref, qdf_ref, kdf_ref, gtf_ref, of_ref, range(nch)),
        (1, ub_ref, wb_ref, qkb_ref, qdb_ref, kdb_ref, gtb_ref, ob_ref, range(nch - 1, -1, -1)),
    )
    chains = [(dr, h) for dr in dirs for h in range(N_HEAD)]
    states = [s_sc[d * N_HEAD + h] for (d, *_), h in chains]
    for step in range(nch):
        wss = []
        for ((d, u_ref, w_ref, qk_ref, qd_ref, kd_ref, gt_ref, o_ref, order), h), s in zip(chains, states):
            r0 = order[step] * CHUNK
            wq = jnp.concatenate([w_ref[0, r0:r0 + CHUNK, h * DK:(h + 1) * DK],
                                  qd_ref[0, r0:r0 + CHUNK, h * DK:(h + 1) * DK]], axis=0)
            wss.append(_dot(wq, s.astype(BF16)))
        vbs = []
        for ((d, u_ref, w_ref, qk_ref, qd_ref, kd_ref, gt_ref, o_ref, order), h), ws in zip(chains, wss):
            r0 = order[step] * CHUNK
            v_new = u_ref[0, r0:r0 + CHUNK, h * DV:(h + 1) * DV].astype(F32) - ws[:CHUNK]
            vbs.append(v_new.astype(BF16))
        new_states = []
        for ((d, u_ref, w_ref, qk_ref, qd_ref, kd_ref, gt_ref, o_ref, order), h), s, vb in zip(chains, states, vbs):
            c = order[step]
            r0 = c * CHUNK
            gt = gt_ref[0, c, :, h * DK:(h + 1) * DK]
            gtb = jnp.concatenate([gt, gt], axis=1)
            new_states.append(s * gtb + _dot_tn(kd_ref[0, r0:r0 + CHUNK, h * DK:(h + 1) * DK], vb))
        for ((d, u_ref, w_ref, qk_ref, qd_ref, kd_ref, gt_ref, o_ref, order), h), ws, vb in zip(chains, wss, vbs):
            r0 = order[step] * CHUNK
            o = ws[CHUNK:] + _dot(qk_ref[0, r0:r0 + CHUNK, h * CHUNK:(h + 1) * CHUNK], vb)
            o_ref[r0:r0 + CHUNK, h * DV:(h + 1) * DV] = o.astype(BF16)
        states = new_states
    for ((d, *_), h), s in zip(chains, states):
        s_sc[d * N_HEAD + h] = s


def _gdn_scan(u, w, qkm, qd, kd, gt, nseq, seq):
    t = u.shape[1]
    tm = TM_GDN
    tps = seq // tm
    nch = tm // CHUNK

    def fwd(width):
        return pl.BlockSpec((1, tm, width), lambda b, j: (0, b * tps + j, 0))

    def bwd(width):
        return pl.BlockSpec((1, tm, width), lambda b, j: (1, b * tps + tps - 1 - j, 0))

    gt_f = pl.BlockSpec((1, nch, 1, N_HEAD * DK), lambda b, j: (0, b * tps + j, 0, 0))
    gt_b = pl.BlockSpec((1, nch, 1, N_HEAD * DK), lambda b, j: (1, b * tps + tps - 1 - j, 0, 0))
    widths = (N_HEAD * DV, N_HEAD * DK, N_HEAD * CHUNK, N_HEAD * DK, N_HEAD * DK)
    return pl.pallas_call(
        _gdn_scan_kernel,
        grid=(nseq, tps),
        in_specs=[fwd(x) for x in widths] + [gt_f] + [bwd(x) for x in widths] + [gt_b],
        out_specs=[
            pl.BlockSpec((tm, N_HEAD * DV), lambda b, j: (b * tps + j, 0)),
            pl.BlockSpec((tm, N_HEAD * DV), lambda b, j: (b * tps + tps - 1 - j, 0)),
        ],
        out_shape=[jax.ShapeDtypeStruct((t, N_HEAD * DV), BF16)] * 2,
        scratch_shapes=[pltpu.VMEM((2 * N_HEAD, DK, DV), F32)],
        compiler_params=_cparams(("parallel", "arbitrary")),
        name="gdn_scan",
    )(u, w, qkm, qd, kd, gt, u, w, qkm, qd, kd, gt)


def _rope_half(x, cos, sin_signed):
    return x * cos + pltpu.roll(x, 64, axis=1) * sin_signed


def _ret_kernel(qf_ref, kf_ref, vf_ref, cosf_ref, sinf_ref,
                qb_ref, kb_ref, vb_ref, cosb_ref, sinb_ref,
                dsum_ref, xif_ref, zf_ref, xib_ref, zb_ref, gch_ref,
                oa_ref, ob_ref, r_sc):
    @pl.when(pl.program_id(1) == 0)
    def _():
        r_sc[...] = jnp.zeros(r_sc.shape, F32)

    scale = DK ** -0.5
    cosf, sinf = cosf_ref[...], sinf_ref[...]
    cosb, sinb = cosb_ref[...], sinb_ref[...]
    for h in range(N_HEAD):
        sl = slice(h * DK, (h + 1) * DK)
        vs = slice(h * DV, (h + 1) * DV)
        q = _rope_half(qf_ref[:, sl].astype(F32), cosf, sinf)
        k = _rope_half(kf_ref[:, sl].astype(F32), cosf, sinf) * scale
        v = vf_ref[:, vs]
        s = (_dot_nt(q.astype(BF16), k.astype(BF16)) * dsum_ref[h]).astype(BF16)
        o = _dot(s, v)
        rf = r_sc[h]
        o = o + _dot((q * xif_ref[:, sl]).astype(BF16), rf.astype(BF16))
        gf = gch_ref[0, :, sl]
        r_sc[h] = rf * jnp.concatenate([gf, gf], axis=1) + _dot_tn((k * zf_ref[:, sl]).astype(BF16), v)
        oa_ref[:, vs] = o.astype(BF16)
        q2 = _rope_half(qb_ref[:, sl].astype(F32), cosb, sinb)
        k2 = _rope_half(kb_ref[:, sl].astype(F32), cosb, sinb) * scale
        v2 = vb_ref[:, vs]
        rb = r_sc[N_HEAD + h]
        ob_ref[:, vs] = _dot((q2 * xib_ref[:, sl]).astype(BF16), rb.astype(BF16)).astype(BF16)
        gb = gch_ref[1, :, sl]
        r_sc[N_HEAD + h] = rb * jnp.concatenate([gb, gb], axis=1) + _dot_tn((k2 * zb_ref[:, sl]).astype(BF16), v2)


def _retention(proj, cos_t, sin_t, dsum, xi_f, zeta_f, xi_b, zeta_b, gch, nseq, seq):
    t = proj.shape[0]
    tr = TR_RET
    tps = seq // tr
    hk, hv = N_HEAD * DK, N_HEAD * DV

    def tok(width, col, mirror):
        cb = col // width
        if mirror:
            return pl.BlockSpec((tr, width), lambda b, j: (b * tps + tps - 1 - j, cb))
        return pl.BlockSpec((tr, width), lambda b, j: (b * tps + j, cb))

    def pos(mirror):
        if mirror:
            return pl.BlockSpec((tr, DK), lambda b, j: (tps - 1 - j, 0))
        return pl.BlockSpec((tr, DK), lambda b, j: (j, 0))

    def side(mirror):
        return [tok(hk, C_RQ, mirror), tok(hk, C_RK, mirror), tok(hv, C_RV, mirror), pos(mirror), pos(mirror)]

    return pl.pallas_call(
        _ret_kernel,
        grid=(nseq, tps),
        in_specs=side(False) + side(True) + [
            _const_spec((N_HEAD, tr, tr)),
            _const_spec((tr, hk)), _const_spec((tr, hk)), _const_spec((tr, hk)), _const_spec((tr, hk)),
            _const_spec((2, 1, hk)),
        ],
        out_specs=[
            pl.BlockSpec((tr, hv), lambda b, j: (b * tps + j, 0)),
            pl.BlockSpec((tr, hv), lambda b, j: (b * tps + tps - 1 - j, 0)),
        ],
        out_shape=[jax.ShapeDtypeStruct((t, hv), BF16)] * 2,
        scratch_shapes=[pltpu.VMEM((2 * N_HEAD, DK, DV), F32)],
        compiler_params=_cparams(("parallel", "arbitrary")),
        name="retention",
    )(proj, proj, proj, cos_t, sin_t, proj, proj, proj, cos_t, sin_t,
      dsum, xi_f, zeta_f, xi_b, zeta_b, gch)


def _axial_rope(x, cos, sin_signed, first_quarter):
    partner = jnp.where(first_quarter, pltpu.roll(x, 96, axis=1), pltpu.roll(x, 32, axis=1))
    return x * cos + partner * sin_signed


def _attn_prep_kernel(q_ref, k_ref, v_ref, gq_ref, gk_ref, cos_ref, sin_ref, qo_ref, kt_ref, vx_ref):
    cos, sin = cos_ref[...], sin_ref[...]
    fq = (lax.broadcasted_iota(jnp.int32, cos.shape, 1) % 64) < 32
    scale = ATT_HD ** -0.5 * LOG2E
    heads = [(q_ref, h, gq_ref) for h in range(ATT_HQ)] + [(k_ref, h, gk_ref) for h in range(ATT_HKV)]
    xs = [ref[:, h * ATT_HD:(h + 1) * ATT_HD].astype(F32) for ref, h, _ in heads]
    xs = [_rms(x, g[...]) for x, (_, _, g) in zip(xs, heads)]
    xs = [_axial_rope(x, cos, sin, fq) for x in xs]
    for h in range(ATT_HQ):
        qo_ref[:, h * ATT_HD:(h + 1) * ATT_HD] = (xs[h] * scale).astype(BF16)
    for h in range(ATT_HKV):
        sl = slice(h * ATT_HD, (h + 1) * ATT_HD)
        kt_ref[sl, :] = xs[ATT_HQ + h].T.astype(BF16)
        vx_ref[:, 2 * h * ATT_HD:(2 * h + 1) * ATT_HD] = v_ref[:, sl]
        vx_ref[:, (2 * h + 1) * ATT_HD:(2 * h + 2) * ATT_HD] = jnp.ones((v_ref.shape[0], ATT_HD), BF16)


def _attn_prep(proj, gq, gk, cos_t, sin_t, seq):
    t = proj.shape[0]
    tm = TM_ATT_PREP
    tps = seq // tm
    nq, nkv = ATT_HQ * ATT_HD, ATT_HKV * ATT_HD
    return pl.pallas_call(
        _attn_prep_kernel,
        grid=(t // tm,),
        in_specs=[
            pl.BlockSpec((tm, nq), lambda i: (i, C_AQ // nq)),
            pl.BlockSpec((tm, nkv), lambda i: (i, C_AK // nkv)),
            pl.BlockSpec((tm, nkv), lambda i: (i, C_AV // nkv)),
            _const_spec((1, ATT_HD)),
            _const_spec((1, ATT_HD)),
            pl.BlockSpec((tm, ATT_HD), lambda i: (i % tps, 0)),
            pl.BlockSpec((tm, ATT_HD), lambda i: (i % tps, 0)),
        ],
        out_specs=[
            pl.BlockSpec((tm, nq), lambda i: (i, 0)),
            pl.BlockSpec((nkv, tm), lambda i: (0, i)),
            pl.BlockSpec((tm, 2 * nkv), lambda i: (i, 0)),
        ],
        out_shape=[
            jax.ShapeDtypeStruct((t, nq), BF16),
            jax.ShapeDtypeStruct((nkv, t), BF16),
            jax.ShapeDtypeStruct((t, 2 * nkv), BF16),
        ],
        compiler_params=_cparams(("parallel",)),
        name="attn_prep",
    )(proj, proj, proj, gq, gk, cos_t, sin_t)


def _flash_kernel(q_ref, kt_ref, v_ref, o_ref, *, seq):
    tq = TQ_ATT
    tk = min(TK_ATT, seq)
    rows = ATT_GROUP * tq
    nk = seq // tk
    q4 = jnp.concatenate([q_ref[:, h * ATT_HD:(h + 1) * ATT_HD] for h in range(ATT_GROUP)], axis=0)

    def scores(j):
        s = _dot(q4, kt_ref[:, j * tk:(j + 1) * tk])
        return s, jnp.max(s, axis=-1, keepdims=True)

    def update(j, s, smax, m, acc):
        m_new = jnp.maximum(m, smax)
        alpha = jnp.exp2(m - m_new)
        p = jnp.exp2(s - m_new).astype(BF16)
        return m_new, alpha * acc + _dot(p, v_ref[j * tk:(j + 1) * tk, :])

    m = jnp.full((rows, 1), -jnp.inf, F32)
    acc = jnp.zeros((rows, 2 * ATT_HD), F32)
    s, smax = scores(0)
    for j in range(nk):
        if j + 1 < nk:
            s_next, smax_next = scores(j + 1)
        m, acc = update(j, s, smax, m, acc)
        if j + 1 < nk:
            s, smax = s_next, smax_next
    out = acc[:, :ATT_HD] / acc[:, ATT_HD:]
    for h in range(ATT_GROUP):
        o_ref[:, h * ATT_HD:(h + 1) * ATT_HD] = out[h * tq:(h + 1) * tq].astype(BF16)


def _flash(q_r, k_t, v_ext, nseq, seq):
    t = q_r.shape[0]
    tq = TQ_ATT
    nq = seq // tq
    gw = ATT_GROUP * ATT_HD
    return pl.pallas_call(
        functools.partial(_flash_kernel, seq=seq),
        grid=(nseq, ATT_HKV, nq),
        in_specs=[
            pl.BlockSpec((tq, gw), lambda b, g, i: (b * nq + i, g)),
            pl.BlockSpec((ATT_HD, seq), lambda b, g, i: (g, b)),
            pl.BlockSpec((seq, 2 * ATT_HD), lambda b, g, i: (b, g)),
        ],
        out_specs=pl.BlockSpec((tq, gw), lambda b, g, i: (b * nq + i, g)),
        out_shape=jax.ShapeDtypeStruct((t, ATT_HQ * ATT_HD), BF16),
        compiler_params=_cparams(("parallel", "parallel", "arbitrary")),
        name="flash_gqa",
    )(q_r, k_t, v_ext)


def _merge_kernel(x_ref, gof_ref, gob_ref, gz_ref, roa_ref, rob_ref, rg_ref, att_ref,
                  g0_ref, g1_ref, g2_ref, gn_ref, rn_ref, wg_ref, wr_ref, wa_ref, wo_ref, o_ref):
    go = gof_ref[...].astype(F32) + gob_ref[...].astype(F32)
    ro = roa_ref[...].astype(F32) + rob_ref[...].astype(F32)
    gz = gz_ref[...].astype(F32)
    rg = rg_ref[...].astype(F32)
    ga_parts, rb_parts = [], []
    for h in range(N_HEAD):
        vs = slice(h * DV, (h + 1) * DV)
        gh = go[:, vs]
        gh = gh * lax.rsqrt(jnp.mean(gh * gh, axis=-1, keepdims=True) + EPS) * gn_ref[...]
        ga_parts.append((gh * _silu(gz[:, vs])).astype(BF16))
        rh = ro[:, vs]
        mu = jnp.mean(rh, axis=-1, keepdims=True)
        cen = rh - mu
        var = jnp.mean(cen * cen, axis=-1, keepdims=True)
        rh = cen * lax.rsqrt(var + EPS) * rn_ref[:, vs]
        rb_parts.append((_silu(rg[:, vs]) * rh).astype(BF16))
    branch_a = _dot(jnp.concatenate(ga_parts, axis=1), wg_ref[...])
    branch_b = _dot(jnp.concatenate(rb_parts, axis=1), wr_ref[...])
    branch_c = _dot(att_ref[...], wa_ref[...])
    merged = (_sigmoid(g0_ref[...].astype(F32)) * branch_a
              + _sigmoid(g1_ref[...].astype(F32)) * branch_b
              + _sigmoid(g2_ref[...].astype(F32)) * branch_c)
    o_ref[...] = x_ref[...] + _dot(merged.astype(BF16), wo_ref[...])


def _merge(x, go_f, go_b, ro_a, ro_b, att, proj, gdn_norm, ret_norm, wg, wr, wa, wo, l):
    t = x.shape[0]
    tm = TM_MERGE
    d = D_MODEL

    def tok(col=0):
        return pl.BlockSpec((tm, d), lambda i: (i, col // d))

    return pl.pallas_call(
        _merge_kernel,
        grid=(t // tm,),
        in_specs=[
            tok(), tok(), tok(), tok(C_GZ), tok(), tok(), tok(C_RG), tok(),
            tok(C_GATE), tok(C_GATE + d), tok(C_GATE + 2 * d),
            _const_spec((1, DV)), _const_spec((1, d)),
        ] + [pl.BlockSpec((None, d, d), lambda i: (l, 0, 0), pipeline_mode=pl.Buffered(1))] * 4,
        out_specs=tok(),
        out_shape=jax.ShapeDtypeStruct((t, d), F32),
        compiler_params=_cparams(("parallel",)),
        name="merge",
    )(x, go_f, go_b, proj, ro_a, ro_b, proj, att, proj, proj, proj,
      gdn_norm, ret_norm, wg, wr, wa, wo)


def _rope_tables(seq):
    f32 = np.float32
    pos = np.arange(seq, dtype=f32)
    inv = (f32(ROPE_THETA) ** (-np.arange(0, DK, 2, dtype=f32) / f32(DK))).astype(f32)
    ang = pos[:, None] * inv[None, :]
    c, s = np.cos(ang), np.sin(ang)
    ret_cos = np.concatenate([c, c], axis=1)
    ret_sin = np.concatenate([-s, s], axis=1)
    half = ATT_HD // 2
    inv_a = (f32(ROPE_THETA) ** (-np.arange(0, half, 2, dtype=f32) / f32(half))).astype(f32)
    rows = (np.arange(seq) // GRID_W).astype(f32)
    cols = (np.arange(seq) % GRID_W).astype(f32)
    ar, ac = rows[:, None] * inv_a[None, :], cols[:, None] * inv_a[None, :]
    cr, sr, cc, sc = np.cos(ar), np.sin(ar), np.cos(ac), np.sin(ac)
    att_cos = np.concatenate([cr, cr, cc, cc], axis=1)
    att_sin = np.concatenate([-sr, sr, -sc, sc], axis=1)
    return tuple(jnp.asarray(t, F32) for t in (ret_cos, ret_sin, att_cos, att_sin))


def _ret_tables(decay_logit):
    c = TR_RET
    lg = jax.nn.log_sigmoid(decay_logit.astype(F32))
    idx = jnp.arange(c, dtype=F32)
    rel = idx[:, None] - idx[None, :]
    lf, lb = lg[0][:, None, None], lg[1][:, None, None]
    dsum = (jnp.exp(jnp.where(rel >= 0, rel * lf, -jnp.inf))
            + jnp.exp(jnp.where(rel <= 0, -rel * lb, -jnp.inf)))

    def lanes(tab):
        return jnp.repeat(tab.T, DK, axis=1)

    xi_f = lanes(jnp.exp((idx + 1.0)[None, :] * lg[0][:, None]))
    zeta_f = lanes(jnp.exp((c - 1.0 - idx)[None, :] * lg[0][:, None]))
    xi_b = lanes(jnp.exp((c - idx)[None, :] * lg[1][:, None]))
    zeta_b = lanes(jnp.exp(idx[None, :] * lg[1][:, None]))
    gch = jnp.repeat(jnp.exp(c * lg), DK, axis=1)[:, None, :]
    return dsum, xi_f, zeta_f, xi_b, zeta_b, gch


def _stacked_weights(p):
    w_in = p["w_in"]
    n_gdn = 2 * N_HEAD * DK + 2 * N_HEAD * DV
    rest = w_in[:, :, n_gdn + 16:]
    n_ret = 2 * N_HEAD * DK + 2 * N_HEAD * DV
    n_att = ATT_HQ * ATT_HD + 2 * ATT_HKV * ATT_HD
    w_main = jnp.concatenate(
        [w_in[:, :, :n_gdn], rest[:, :, n_ret + n_att:], rest[:, :, :n_ret + n_att]], axis=2).astype(BF16)
    w_small = jnp.pad(w_in[:, :, n_gdn:n_gdn + 16], ((0, 0), (0, 0), (0, 112))).astype(BF16)
    cast = {k: p[k].astype(BF16) for k in (
        "ffn1_w1", "ffn1_w3", "ffn1_w2", "ffn2_w1", "ffn2_w3", "ffn2_w2",
        "w_branch_gdn", "w_branch_ret", "w_branch_attn", "w_out")}
    return dict(cast, w_main=w_main, w_small=w_small)


def _layer_params(l, p):
    conv_w = jnp.pad(p["gdn_conv"][l].astype(F32), ((0, 8 - CONV_W), (0, 0)))
    alog_row = jnp.pad(p["gdn_A_log"][l].astype(F32).reshape(1, 8), ((0, 0), (8, 112)))
    bias_row = jnp.pad(p["gdn_dt_bias"][l].astype(F32).reshape(1, 8), ((0, 0), (8, 112)))
    return dict(
        ffn1_norm=p["ffn1_norm"][l][None, :], ffn2_norm=p["ffn2_norm"][l][None, :],
        mix_norm=p["mix_norm"][l][None, :],
        conv_w=conv_w, alog_row=alog_row, bias_row=bias_row,
        ret_tabs=_ret_tables(p["ret_decay_logit"][l]),
        gq=p["attn_q_norm"][l][None, :], gk=p["attn_k_norm"][l][None, :],
        gdn_norm=p["gdn_norm"][l][None, :], ret_norm=p["ret_norm"][l][None, :],
    )


def _encoder_layer(x, l, lp, sw, tabs, nseq, seq):
    ret_cos, ret_sin, att_cos, att_sin = tabs
    x = _ffn(x, lp["ffn1_norm"], sw["ffn1_w1"], sw["ffn1_w3"], sw["ffn1_w2"], l)
    proj, small = _inproj(x, lp["mix_norm"], sw["w_main"], sw["w_small"], l)
    u, w, qkm, qd, kd, gt = _gdn_prep(proj, small, lp["conv_w"], lp["alog_row"], lp["bias_row"], seq)
    go_f, go_b = _gdn_scan(u, w, qkm, qd, kd, gt, nseq, seq)
    ro_a, ro_b = _retention(proj, ret_cos, ret_sin, *lp["ret_tabs"], nseq, seq)
    q_r, k_t, v_ext = _attn_prep(proj, lp["gq"], lp["gk"], att_cos, att_sin, seq)
    att = _flash(q_r, k_t, v_ext, nseq, seq)
    x = _merge(x, go_f, go_b, ro_a, ro_b, att, proj, lp["gdn_norm"], lp["ret_norm"],
               sw["w_branch_gdn"], sw["w_branch_ret"], sw["w_branch_attn"], sw["w_out"], l)
    return _ffn(x, lp["ffn2_norm"], sw["ffn2_w1"], sw["ffn2_w3"], sw["ffn2_w2"], l)


def _trunk(x3, layers, sw, tabs):
    nseq, seq, d = x3.shape
    x = x3.reshape(nseq * seq, d)
    for l, lp in enumerate(layers):
        x = _encoder_layer(x, l, lp, sw, tabs, nseq, seq)
    return x.reshape(nseq, seq, d)


def kernel(x_prompt, x_sample, ffn1_norm, ffn1_w1, ffn1_w3, ffn1_w2, mix_norm, w_in, gdn_conv, gdn_A_log, gdn_dt_bias, gdn_norm, ret_decay_logit, ret_norm, attn_q_norm, attn_k_norm, w_branch_gdn, w_branch_ret, w_branch_attn, w_out, ffn2_norm, ffn2_w1, ffn2_w3, ffn2_w2):
    p = dict(ffn1_norm=ffn1_norm, ffn1_w1=ffn1_w1, ffn1_w3=ffn1_w3, ffn1_w2=ffn1_w2, mix_norm=mix_norm,
             w_in=w_in, gdn_conv=gdn_conv, gdn_A_log=gdn_A_log, gdn_dt_bias=gdn_dt_bias, gdn_norm=gdn_norm,
             ret_decay_logit=ret_decay_logit, ret_norm=ret_norm, attn_q_norm=attn_q_norm,
             attn_k_norm=attn_k_norm, w_branch_gdn=w_branch_gdn, w_branch_ret=w_branch_ret,
             w_branch_attn=w_branch_attn, w_out=w_out, ffn2_norm=ffn2_norm, ffn2_w1=ffn2_w1,
             ffn2_w3=ffn2_w3, ffn2_w2=ffn2_w2)
    depth = w_in.shape[0]
    layers = [_layer_params(l, p) for l in range(depth)]
    sw = _stacked_weights(p)
    assert x_prompt.shape[1] == x_sample.shape[1]
    tabs = _rope_tables(x_prompt.shape[1])
    return (_trunk(x_prompt, layers, sw, tabs), _trunk(x_sample, layers, sw, tabs))
```

```python
import functools
import math

import jax
import jax.numpy as jnp
import numpy as np
from jax import lax
from jax.experimental import pallas as pl
from jax.experimental.pallas import tpu as pltpu

F32 = jnp.float32
BF16 = jnp.bfloat16

D_MODEL = 1024
D_FF = 2816
EPS = 1e-6
ROPE_THETA = 10000.0
GRID_W = 64
LOG2E = math.log2(math.e)

N_HEAD = 4
DK = 128
DV = 256
CONV_W = 5
CHUNK = 64
ATT_HQ = 8
ATT_HKV = 2
ATT_HD = 128
ATT_GROUP = ATT_HQ // ATT_HKV

C_GQKV = 0
C_GZ = 2048
C_GATE = 3072
C_RQ = 6144
C_RK = 6656
C_RV = 7168
C_RG = 8192
C_AQ = 9216
C_AK = 10240
C_AV = 10496
N_PROJ = 10752

VMEM_LIMIT = 56 * 1024 * 1024

TM_FFN = 512
FF_CHUNKS = ((0, 768), (768, 1536), (1536, 2304), (2304, 2816))
TM_PROJ = 1024
TN_PROJ = 1536
TM_GDN = 256
TR_RET = 256
TM_ATT_PREP = 256
TQ_ATT = 256
TK_ATT = 2048
TM_MERGE = 512


def _cparams(sem):
    return pltpu.CompilerParams(dimension_semantics=sem, vmem_limit_bytes=VMEM_LIMIT)


def _dot(a, b):
    return jnp.dot(a, b, preferred_element_type=F32)


def _dot_nt(a, b):
    return lax.dot_general(a, b, (((1,), (1,)), ((), ())), preferred_element_type=F32)


def _dot_tn(a, b):
    return lax.dot_general(a, b, (((0,), (0,)), ((), ())), preferred_element_type=F32)


def _sigmoid(x):
    return 1.0 / (1.0 + jnp.exp2(x * (-LOG2E)))


def _silu(x):
    return x * _sigmoid(x)


def _rms(x, gain):
    ms = jnp.mean(x * x, axis=-1, keepdims=True)
    return x * lax.rsqrt(ms + EPS) * gain


def _const_spec(shape):
    n = len(shape)
    return pl.BlockSpec(shape, lambda *_: (0,) * n)


def _ffn_kernel(x_ref, g_ref, w1_ref, w3_ref, w2_ref, o_ref):
    x = x_ref[...]
    n = _rms(x, g_ref[...]).astype(BF16)
    acc = jnp.zeros(x.shape, F32)
    for lo, hi in FF_CHUNKS:
        h1 = _dot(n, w1_ref[:, lo:hi])
        h3 = _dot(n, w3_ref[:, lo:hi])
        a = (_silu(h1) * h3).astype(BF16)
        acc = acc + _dot(a, w2_ref[lo:hi, :])
    o_ref[...] = x + 0.5 * acc


def _ffn(x, gain, w1, w3, w2, l):
    t = x.shape[0]
    resident = dict(pipeline_mode=pl.Buffered(1))
    return pl.pallas_call(
        _ffn_kernel,
        grid=(t // TM_FFN,),
        in_specs=[
            pl.BlockSpec((TM_FFN, D_MODEL), lambda i: (i, 0)),
            pl.BlockSpec((1, D_MODEL), lambda i: (0, 0)),
            pl.BlockSpec((None, D_MODEL, D_FF), lambda i: (l, 0, 0), **resident),
            pl.BlockSpec((None, D_MODEL, D_FF), lambda i: (l, 0, 0), **resident),
            pl.BlockSpec((None, D_FF, D_MODEL), lambda i: (l, 0, 0), **resident),
        ],
        out_specs=pl.BlockSpec((TM_FFN, D_MODEL), lambda i: (i, 0)),
        out_shape=jax.ShapeDtypeStruct((t, D_MODEL), F32),
        compiler_params=_cparams(("parallel",)),
        name="ffn",
    )(x, gain, w1, w3, w2)


def _inproj_kernel(x_ref, g_ref, w_ref, ws_ref, o_ref, os_ref, n_sc):
    @pl.when(pl.program_id(1) == 0)
    def _():
        n = _rms(x_ref[...], g_ref[...]).astype(BF16)
        n_sc[...] = n
        os_ref[...] = _dot(n, ws_ref[...])

    o_ref[...] = _dot(n_sc[...], w_ref[...]).astype(BF16)


def _inproj(x, gain, w_main, w_small, l):
    t = x.shape[0]
    return pl.pallas_call(
        _inproj_kernel,
        grid=(t // TM_PROJ, N_PROJ // TN_PROJ),
        in_specs=[
            pl.BlockSpec((TM_PROJ, D_MODEL), lambda i, j: (i, 0)),
            pl.BlockSpec((1, D_MODEL), lambda i, j: (0, 0)),
            pl.BlockSpec((None, D_MODEL, TN_PROJ), lambda i, j: (l, 0, j)),
            pl.BlockSpec((None, D_MODEL, 128), lambda i, j: (l, 0, 0)),
        ],
        out_specs=[
            pl.BlockSpec((TM_PROJ, TN_PROJ), lambda i, j: (i, j)),
            pl.BlockSpec((TM_PROJ, 128), lambda i, j: (i, 0)),
        ],
        out_shape=[
            jax.ShapeDtypeStruct((t, N_PROJ), BF16),
            jax.ShapeDtypeStruct((t, 128), F32),
        ],
        scratch_shapes=[pltpu.VMEM((TM_PROJ, D_MODEL), BF16)],
        compiler_params=_cparams(("parallel", "arbitrary")),
        name="inproj",
    )(x, gain, w_main, w_small)


def _split_hi_lo(x):
    hi = x.astype(BF16)
    lo = (x - hi.astype(F32)).astype(BF16)
    return hi, lo


def _dot01(m01, x):
    hi, lo = _split_hi_lo(x)
    return _dot(m01, hi) + _dot(m01, lo)


def _block_diag(p, mask):
    return jnp.where(mask, jnp.concatenate([p] * 4, axis=0), jnp.zeros((), p.dtype))


def _neumann_inverse(l_mats, eye4, bd_mask):
    qs = [-l for l in l_mats]
    accs = [eye4 + q for q in qs]
    qbs = [q.astype(BF16) for q in qs]
    qs = [_dot(qb, _block_diag(qb, bd_mask)) for qb in qbs]
    for _ in range(4):
        qbs = [q.astype(BF16) for q in qs]
        rs = [_dot(jnp.concatenate([a.astype(BF16), qb], axis=0), _block_diag(qb, bd_mask))
              for a, qb in zip(accs, qbs)]
        accs = [a + r[:CHUNK] for a, r in zip(accs, rs)]
        qs = [r[CHUNK:] for r in rs]
    return [(a + _dot(a.astype(BF16), _block_diag(q.astype(BF16), bd_mask))).astype(BF16)
            for a, q in zip(accs, qs)]


def _gdn_prep_kernel(prev_ref, cur_ref, next_ref, sm_ref, cw_ref, alog_ref, bias_ref,
                     u_ref, w_ref, qkm_ref, qd_ref, kd_ref, gt_ref, ext_sc, *, tiles_per_seq):
    tm = TM_GDN
    i = pl.program_id(0)
    jt = i % tiles_per_seq
    m_prev = (jt != 0).astype(F32)
    m_next = (jt != tiles_per_seq - 1).astype(F32)

    ext_sc[0:16, :] = prev_ref[...].astype(F32) * m_prev
    ext_sc[16:16 + tm, :] = cur_ref[...].astype(F32)
    ext_sc[16 + tm:32 + tm, :] = next_ref[...].astype(F32) * m_next
    conv = jnp.zeros((tm, 2 * N_HEAD * DK + N_HEAD * DV), F32)
    for w in range(CONV_W):
        conv = conv + ext_sc[14 + w:14 + w + tm, :] * cw_ref[w:w + 1, :]
    act = _silu(conv)

    nqk = N_HEAD * DK
    q_parts, k_parts = [], []
    for h in range(N_HEAD):
        qh = act[:, h * DK:(h + 1) * DK]
        kh = act[:, nqk + h * DK:nqk + (h + 1) * DK]
        q_parts.append(qh * lax.rsqrt(jnp.sum(qh * qh, axis=-1, keepdims=True) + EPS) * (DK ** -0.5))
        k_parts.append(kh * lax.rsqrt(jnp.sum(kh * kh, axis=-1, keepdims=True) + EPS))
    q_n = jnp.concatenate(q_parts, axis=1)
    k_n = jnp.concatenate(k_parts, axis=1)
    v = act[:, 2 * nqk:]

    sm = sm_ref[...]
    beta8 = _sigmoid(sm)
    zb = sm + bias_ref[...]
    softplus = jnp.maximum(zb, 0.0) + jnp.log(1.0 + jnp.exp(-jnp.abs(zb)))
    g8 = -jnp.exp(alog_ref[...]) * softplus

    def expand_mat(width, row0):
        r = lax.broadcasted_iota(jnp.int32, (128, 8 * width), 0)
        c = lax.broadcasted_iota(jnp.int32, (128, 8 * width), 1) // width
        return (r == c + row0).astype(BF16)

    e64b, e128b = expand_mat(64, 0), expand_mat(128, 0)
    e64g, e128g = expand_mat(64, 8), expand_mat(128, 8)
    beta8b = beta8.astype(BF16)
    beta64 = _dot(beta8b, e64b)
    beta128 = _dot(beta8b, e128b)
    g64 = _dot01_rhs(g8, e64g)
    g128 = _dot01_rhs(g8, e128g)

    ri = lax.broadcasted_iota(jnp.int32, (tm, tm), 0)
    ci = lax.broadcasted_iota(jnp.int32, (tm, tm), 1)
    same = (ri // CHUNK) == (ci // CHUNK)
    m_le = (same & (ci <= ri)).astype(BF16)
    m_ge = (same & (ci >= ri)).astype(BF16)
    m_lt = (same & (ci < ri)).astype(BF16)
    m_gt = (same & (ci > ri)).astype(BF16)

    tl = lax.broadcasted_iota(jnp.int32, (tm, 4 * CHUNK), 0) % CHUNK
    jl = lax.broadcasted_iota(jnp.int32, (tm, 4 * CHUNK), 1) % CHUNK
    hw = N_HEAD * CHUNK
    delta_f = _dot01(m_le, jnp.where(tl > jl, g64[:, :hw], 0.0))
    delta_b = _dot01(m_ge, jnp.where(tl < jl, g64[:, hw:], 0.0))
    hk = N_HEAD * DK
    gc_f = _dot01(m_le, g128[:, :hk])
    rest_f = _dot01(m_gt, g128[:, :hk])
    gc_b = _dot01(m_ge, g128[:, hk:])
    rest_b = _dot01(m_lt, g128[:, hk:])

    il = lax.broadcasted_iota(jnp.int32, (CHUNK, 4 * CHUNK), 0)
    jc = lax.broadcasted_iota(jnp.int32, (CHUNK, 4 * CHUNK), 1) % CHUNK
    eye4 = (il == jc).astype(F32)
    bd_mask = (lax.broadcasted_iota(jnp.int32, (4 * CHUNK, 4 * CHUNK), 0) // CHUNK
               == lax.broadcasted_iota(jnp.int32, (4 * CHUNK, 4 * CHUNK), 1) // CHUNK)
    bdk_mask = (lax.broadcasted_iota(jnp.int32, (4 * CHUNK, hk), 0) // CHUNK
                == lax.broadcasted_iota(jnp.int32, (4 * CHUNK, hk), 1) // DK)

    k_nb = k_n.astype(BF16)
    q_nb = q_n.astype(BF16)
    per_dir = []
    for d, gc, rest, delta, incl, strict in (
            (0, gc_f, rest_f, delta_f, il >= jc, il > jc),
            (1, gc_b, rest_b, delta_b, il <= jc, il < jc)):
        e_gc = jnp.exp(gc)
        qd_ref[d] = (q_n * e_gc).astype(BF16)
        kd_ref[d] = (k_n * jnp.exp(rest)).astype(BF16)
        b128 = beta128[:, d * hk:(d + 1) * hk]
        kbe = (k_n * b128 * e_gc).astype(BF16)
        for c in range(tm // CHUNK):
            r0 = c * CHUNK
            gt_ref[d, c] = jnp.exp(gc[r0:r0 + 1, :] + rest[r0:r0 + 1, :])
        per_dir.append((d, delta, incl, strict, b128, kbe, beta64[:, d * hw:(d + 1) * hw]))

    l_mats, keys = [], []
    for c in range(tm // CHUNK):
        r0 = c * CHUNK
        k_c = k_nb[r0:r0 + CHUNK]
        bdk = jnp.where(bdk_mask, jnp.concatenate([k_c] * 4, axis=0), jnp.zeros((), BF16))
        gq = _dot_nt(jnp.concatenate([k_c, q_nb[r0:r0 + CHUNK]], axis=0), bdk)
        kk, qk = gq[:CHUNK], gq[CHUNK:]
        for d, delta, incl, strict, b128, kbe, b64 in per_dir:
            dm = jnp.where(incl, jnp.exp(delta[r0:r0 + CHUNK]), 0.0)
            l_mats.append(jnp.where(strict, b64[r0:r0 + CHUNK] * kk * dm, 0.0))
            qkm_ref[d, r0:r0 + CHUNK, :] = (qk * dm).astype(BF16)
            keys.append((r0, d, b128, kbe))
    t_alls = _neumann_inverse(l_mats, eye4, bd_mask)
    for (r0, d, b128, kbe), t_all in zip(keys, t_alls):
        for h in range(N_HEAD):
            bh = b128[r0:r0 + CHUNK, h * DK:(h + 1) * DK]
            vh = v[r0:r0 + CHUNK, h * DV:(h + 1) * DV]
            vb = jnp.concatenate([vh[:, :DK] * bh, vh[:, DK:] * bh], axis=1).astype(BF16)
            rhs = jnp.concatenate([vb, kbe[r0:r0 + CHUNK, h * DK:(h + 1) * DK]], axis=1)
            uw = _dot(t_all[:, h * CHUNK:(h + 1) * CHUNK], rhs)
            u_ref[d, r0:r0 + CHUNK, h * DV:(h + 1) * DV] = uw[:, :DV].astype(BF16)
            w_ref[d, r0:r0 + CHUNK, h * DK:(h + 1) * DK] = uw[:, DV:].astype(BF16)


def _dot01_rhs(x, m01):
    hi, lo = _split_hi_lo(x)
    return _dot(hi, m01) + _dot(lo, m01)


def _gdn_prep(proj, small, conv_w, alog_row, bias_row, seq):
    t = proj.shape[0]
    tm = TM_GDN
    tps = seq // tm
    nqkv = 2 * N_HEAD * DK + N_HEAD * DV
    hb = tm // 16
    last16 = t // 16 - 1
    kern = functools.partial(_gdn_prep_kernel, tiles_per_seq=tps)
    return pl.pallas_call(
        kern,
        grid=(t // tm,),
        in_specs=[
            pl.BlockSpec((16, nqkv), lambda i: (jnp.maximum(i * hb - 1, 0), 0)),
            pl.BlockSpec((tm, nqkv), lambda i: (i, 0)),
            pl.BlockSpec((16, nqkv), lambda i: (jnp.minimum((i + 1) * hb, last16), 0)),
            pl.BlockSpec((tm, 128), lambda i: (i, 0)),
            _const_spec((8, nqkv)),
            _const_spec((1, 128)),
            _const_spec((1, 128)),
        ],
        out_specs=[
            pl.BlockSpec((2, tm, N_HEAD * DV), lambda i: (0, i, 0)),
            pl.BlockSpec((2, tm, N_HEAD * DK), lambda i: (0, i, 0)),
            pl.BlockSpec((2, tm, N_HEAD * CHUNK), lambda i: (0, i, 0)),
            pl.BlockSpec((2, tm, N_HEAD * DK), lambda i: (0, i, 0)),
            pl.BlockSpec((2, tm, N_HEAD * DK), lambda i: (0, i, 0)),
            pl.BlockSpec((2, tm // CHUNK, 1, N_HEAD * DK), lambda i: (0, i, 0, 0)),
        ],
        out_shape=[
            jax.ShapeDtypeStruct((2, t, N_HEAD * DV), BF16),
            jax.ShapeDtypeStruct((2, t, N_HEAD * DK), BF16),
            jax.ShapeDtypeStruct((2, t, N_HEAD * CHUNK), BF16),
            jax.ShapeDtypeStruct((2, t, N_HEAD * DK), BF16),
            jax.ShapeDtypeStruct((2, t, N_HEAD * DK), BF16),
            jax.ShapeDtypeStruct((2, t // CHUNK, 1, N_HEAD * DK), F32),
        ],
        scratch_shapes=[pltpu.VMEM((tm + 32, nqkv), F32)],
        compiler_params=_cparams(("parallel",)),
        name="gdn_prep",
    )(proj, proj, proj, small, conv_w, alog_row, bias_row)


def _gdn_scan_kernel(uf_ref, wf_ref, qkf_ref, qdf_ref, kdf_ref, gtf_ref,
                     ub_ref, wb_ref, qkb_ref, qdb_ref, kdb_ref, gtb_ref,
                     of_ref, ob_ref, s_sc, after_step):
    @pl.when(pl.program_id(1) == 0)
    def _():
        s_sc[...] = jnp.zeros(s_sc.shape, F32)

    nch = TM_GDN // CHUNK
    dirs = (
        (0, uf_ref, wf_ref, qkf_ref, qdf_ref, kdf_ref, gtf_ref, of_ref, range(nch)),
        (1, ub_ref, wb_ref, qkb_ref, qdb_ref, kdb_ref, gtb_ref, ob_ref, range(nch - 1, -1, -1)),
    )
    chains = [(dr, h) for dr in dirs for h in range(N_HEAD)]
    states = [s_sc[d * N_HEAD + h] for (d, *_), h in chains]
    for step in range(nch):
        wss = []
        for ((d, u_ref, w_ref, qk_ref, qd_ref, kd_ref, gt_ref, o_ref, order), h), s in zip(chains, states):
            r0 = order[step] * CHUNK
            wq = jnp.concatenate([w_ref[0, r0:r0 + CHUNK, h * DK:(h + 1) * DK],
                                  qd_ref[0, r0:r0 + CHUNK, h * DK:(h + 1) * DK]], axis=0)
            wss.append(_dot(wq, s.astype(BF16)))
        vbs = []
        for ((d, u_ref, w_ref, qk_ref, qd_ref, kd_ref, gt_ref, o_ref, order), h), ws in zip(chains, wss):
            r0 = order[step] * CHUNK
            v_new = u_ref[0, r0:r0 + CHUNK, h * DV:(h + 1) * DV].astype(F32) - ws[:CHUNK]
            vbs.append(v_new.astype(BF16))
        new_states = []
        for ((d, u_ref, w_ref, qk_ref, qd_ref, kd_ref, gt_ref, o_ref, order), h), s, vb in zip(chains, states, vbs):
            c = order[step]
            r0 = c * CHUNK
            gt = gt_ref[0, c, :, h * DK:(h + 1) * DK]
            gtb = jnp.concatenate([gt, gt], axis=1)
            new_states.append(s * gtb + _dot_tn(kd_ref[0, r0:r0 + CHUNK, h * DK:(h + 1) * DK], vb))
        for ((d, u_ref, w_ref, qk_ref, qd_ref, kd_ref, gt_ref, o_ref, order), h), ws, vb in zip(chains, wss, vbs):
            r0 = order[step] * CHUNK
            o = ws[CHUNK:] + _dot(qk_ref[0, r0:r0 + CHUNK, h * CHUNK:(h + 1) * CHUNK], vb)
            o_ref[r0:r0 + CHUNK, h * DV:(h + 1) * DV] = o.astype(BF16)
        states = new_states
        after_step(step)
    for ((d, *_), h), s in zip(chains, states):
        s_sc[d * N_HEAD + h] = s


def _gdn_scan_specs(u, w, qkm, qd, kd, gt, seq):
    tm = TM_GDN
    tps = seq // tm
    nch = tm // CHUNK

    def fwd(width):
        return pl.BlockSpec((1, tm, width), lambda b, j: (0, b * tps + j, 0))

    def bwd(width):
        return pl.BlockSpec((1, tm, width), lambda b, j: (1, b * tps + tps - 1 - j, 0))

    gt_f = pl.BlockSpec((1, nch, 1, N_HEAD * DK), lambda b, j: (0, b * tps + j, 0, 0))
    gt_b = pl.BlockSpec((1, nch, 1, N_HEAD * DK), lambda b, j: (1, b * tps + tps - 1 - j, 0, 0))
    widths = (N_HEAD * DV, N_HEAD * DK, N_HEAD * CHUNK, N_HEAD * DK, N_HEAD * DK)
    in_specs = [fwd(x) for x in widths] + [gt_f] + [bwd(x) for x in widths] + [gt_b]
    return in_specs, (u, w, qkm, qd, kd, gt, u, w, qkm, qd, kd, gt)


def _rope_half(x, cos, sin_signed):
    return x * cos + pltpu.roll(x, 64, axis=1) * sin_signed


def _ret_heads(qf_ref, kf_ref, vf_ref, cosf_ref, sinf_ref,
               qb_ref, kb_ref, vb_ref, cosb_ref, sinb_ref,
               dsum_ref, xif_ref, zf_ref, xib_ref, zb_ref, gch_ref,
               oa_ref, ob_ref, r_sc):
    @pl.when(pl.program_id(1) == 0)
    def _():
        r_sc[...] = jnp.zeros(r_sc.shape, F32)

    scale = DK ** -0.5
    cosf, sinf = cosf_ref[...], sinf_ref[...]
    cosb, sinb = cosb_ref[...], sinb_ref[...]

    def head(h):
        sl = slice(h * DK, (h + 1) * DK)
        vs = slice(h * DV, (h + 1) * DV)
        q = _rope_half(qf_ref[:, sl].astype(F32), cosf, sinf)
        k = _rope_half(kf_ref[:, sl].astype(F32), cosf, sinf) * scale
        v = vf_ref[:, vs]
        s = (_dot_nt(q.astype(BF16), k.astype(BF16)) * dsum_ref[h]).astype(BF16)
        o = _dot(s, v)
        rf = r_sc[h]
        o = o + _dot((q * xif_ref[:, sl]).astype(BF16), rf.astype(BF16))
        gf = gch_ref[0, :, sl]
        r_sc[h] = rf * jnp.concatenate([gf, gf], axis=1) + _dot_tn((k * zf_ref[:, sl]).astype(BF16), v)
        oa_ref[:, vs] = o.astype(BF16)
        q2 = _rope_half(qb_ref[:, sl].astype(F32), cosb, sinb)
        k2 = _rope_half(kb_ref[:, sl].astype(F32), cosb, sinb) * scale
        v2 = vb_ref[:, vs]
        rb = r_sc[N_HEAD + h]
        ob_ref[:, vs] = _dot((q2 * xib_ref[:, sl]).astype(BF16), rb.astype(BF16)).astype(BF16)
        gb = gch_ref[1, :, sl]
        r_sc[N_HEAD + h] = rb * jnp.concatenate([gb, gb], axis=1) + _dot_tn((k2 * zb_ref[:, sl]).astype(BF16), v2)

    return head


def _retention_specs(proj, cos_t, sin_t, dsum, xi_f, zeta_f, xi_b, zeta_b, gch, seq):
    tr = TR_RET
    tps = seq // tr
    hk, hv = N_HEAD * DK, N_HEAD * DV

    def tok(width, col, mirror):
        cb = col // width
        if mirror:
            return pl.BlockSpec((tr, width), lambda b, j: (b * tps + tps - 1 - j, cb))
        return pl.BlockSpec((tr, width), lambda b, j: (b * tps + j, cb))

    def pos(mirror):
        if mirror:
            return pl.BlockSpec((tr, DK), lambda b, j: (tps - 1 - j, 0))
        return pl.BlockSpec((tr, DK), lambda b, j: (j, 0))

    def side(mirror):
        return [tok(hk, C_RQ, mirror), tok(hk, C_RK, mirror), tok(hv, C_RV, mirror), pos(mirror), pos(mirror)]

    in_specs = side(False) + side(True) + [
        _const_spec((N_HEAD, tr, tr)),
        _const_spec((tr, hk)), _const_spec((tr, hk)), _const_spec((tr, hk)), _const_spec((tr, hk)),
        _const_spec((2, 1, hk)),
    ]
    args = (proj, proj, proj, cos_t, sin_t, proj, proj, proj, cos_t, sin_t,
            dsum, xi_f, zeta_f, xi_b, zeta_b, gch)
    return in_specs, args


N_SCAN_IN = 12
N_RET_IN = 16


def _recurrent_kernel(*refs):
    scan_in = refs[:N_SCAN_IN]
    ret_in = refs[N_SCAN_IN:N_SCAN_IN + N_RET_IN]
    gof_ref, gob_ref, roa_ref, rob_ref, s_sc, r_sc = refs[N_SCAN_IN + N_RET_IN:]
    assert TM_GDN // CHUNK == N_HEAD
    ret_head = _ret_heads(*ret_in, roa_ref, rob_ref, r_sc)
    _gdn_scan_kernel(*scan_in, gof_ref, gob_ref, s_sc, ret_head)


def _recurrences(scan_args, ret_args, nseq, seq):
    assert TM_GDN == TR_RET
    tm = TM_GDN
    tps = seq // tm
    hv = N_HEAD * DV
    scan_specs, scan_ops = _gdn_scan_specs(*scan_args, seq)
    ret_specs, ret_ops = _retention_specs(*ret_args, seq)
    assert len(scan_specs) == N_SCAN_IN and len(ret_specs) == N_RET_IN
    t = scan_ops[0].shape[1]
    fwd_out = pl.BlockSpec((tm, hv), lambda b, j: (b * tps + j, 0))
    bwd_out = pl.BlockSpec((tm, hv), lambda b, j: (b * tps + tps - 1 - j, 0))
    return pl.pallas_call(
        _recurrent_kernel,
        grid=(nseq, tps),
        in_specs=scan_specs + ret_specs,
        out_specs=[fwd_out, bwd_out, fwd_out, bwd_out],
        out_shape=[jax.ShapeDtypeStruct((t, hv), BF16)] * 4,
        scratch_shapes=[pltpu.VMEM((2 * N_HEAD, DK, DV), F32), pltpu.VMEM((2 * N_HEAD, DK, DV), F32)],
        compiler_params=_cparams(("parallel", "arbitrary")),
        name="recurrences",
    )(*scan_ops, *ret_ops)


def _axial_rope(x, cos, sin_signed, swap):
    return x * cos + _dot01_rhs(x, swap) * sin_signed


def _attn_prep_kernel(q_ref, k_ref, v_ref, gq_ref, gk_ref, cos_ref, sin_ref, qo_ref, kt_ref, vx_ref):
    cos, sin = cos_ref[...], sin_ref[...]
    src = lax.broadcasted_iota(jnp.int32, (ATT_HD, ATT_HD), 0)
    dst = lax.broadcasted_iota(jnp.int32, (ATT_HD, ATT_HD), 1)
    fq = (src == jnp.where((dst % 64) < 32, dst + 32, dst - 32)).astype(BF16)
    scale = ATT_HD ** -0.5 * LOG2E
    heads = [(q_ref, h, gq_ref) for h in range(ATT_HQ)] + [(k_ref, h, gk_ref) for h in range(ATT_HKV)]
    xs = [ref[:, h * ATT_HD:(h + 1) * ATT_HD].astype(F32) for ref, h, _ in heads]
    xs = [_rms(x, g[...]) for x, (_, _, g) in zip(xs, heads)]
    xs = [_axial_rope(x, cos, sin, fq) for x in xs]
    for h in range(ATT_HQ):
        qo_ref[:, h * ATT_HD:(h + 1) * ATT_HD] = (xs[h] * scale).astype(BF16)
    for h in range(ATT_HKV):
        sl = slice(h * ATT_HD, (h + 1) * ATT_HD)
        kt_ref[sl, :] = xs[ATT_HQ + h].T.astype(BF16)
        vx_ref[:, 2 * h * ATT_HD:(2 * h + 1) * ATT_HD] = v_ref[:, sl]
        vx_ref[:, (2 * h + 1) * ATT_HD:(2 * h + 2) * ATT_HD] = jnp.ones((v_ref.shape[0], ATT_HD), BF16)


def _attn_prep(proj, gq, gk, cos_t, sin_t, seq):
    t = proj.shape[0]
    tm = TM_ATT_PREP
    tps = seq // tm
    nq, nkv = ATT_HQ * ATT_HD, ATT_HKV * ATT_HD
    return pl.pallas_call(
        _attn_prep_kernel,
        grid=(t // tm,),
        in_specs=[
            pl.BlockSpec((tm, nq), lambda i: (i, C_AQ // nq)),
            pl.BlockSpec((tm, nkv), lambda i: (i, C_AK // nkv)),
            pl.BlockSpec((tm, nkv), lambda i: (i, C_AV // nkv)),
            _const_spec((1, ATT_HD)),
            _const_spec((1, ATT_HD)),
            pl.BlockSpec((tm, ATT_HD), lambda i: (i % tps, 0)),
            pl.BlockSpec((tm, ATT_HD), lambda i: (i % tps, 0)),
        ],
        out_specs=[
            pl.BlockSpec((tm, nq), lambda i: (i, 0)),
            pl.BlockSpec((nkv, tm), lambda i: (0, i)),
            pl.BlockSpec((tm, 2 * nkv), lambda i: (i, 0)),
        ],
        out_shape=[
            jax.ShapeDtypeStruct((t, nq), BF16),
            jax.ShapeDtypeStruct((nkv, t), BF16),
            jax.ShapeDtypeStruct((t, 2 * nkv), BF16),
        ],
        compiler_params=_cparams(("parallel",)),
        name="attn_prep",
    )(proj, proj, proj, gq, gk, cos_t, sin_t)


def _flash_kernel(q_ref, kt_ref, v_ref, o_ref, *, seq):
    tq = TQ_ATT
    tk = min(TK_ATT, seq)
    rows = ATT_GROUP * tq
    nk = seq // tk
    q4 = jnp.concatenate([q_ref[:, h * ATT_HD:(h + 1) * ATT_HD] for h in range(ATT_GROUP)], axis=0)

    def scores(j):
        s = _dot(q4, kt_ref[:, j * tk:(j + 1) * tk])
        return s, jnp.max(s, axis=-1, keepdims=True)

    def update(j, s, smax, m, acc):
        m_new = jnp.maximum(m, smax)
        alpha = jnp.exp2(m - m_new)
        p = jnp.exp2(s - m_new).astype(BF16)
        return m_new, alpha * acc + _dot(p, v_ref[j * tk:(j + 1) * tk, :])

    m = jnp.full((rows, 1), -jnp.inf, F32)
    acc = jnp.zeros((rows, 2 * ATT_HD), F32)
    s, smax = scores(0)
    for j in range(nk):
        if j + 1 < nk:
            s_next, smax_next = scores(j + 1)
        m, acc = update(j, s, smax, m, acc)
        if j + 1 < nk:
            s, smax = s_next, smax_next
    out = acc[:, :ATT_HD] / acc[:, ATT_HD:]
    for h in range(ATT_GROUP):
        o_ref[:, h * ATT_HD:(h + 1) * ATT_HD] = out[h * tq:(h + 1) * tq].astype(BF16)


def _flash(q_r, k_t, v_ext, nseq, seq):
    t = q_r.shape[0]
    tq = TQ_ATT
    nq = seq // tq
    gw = ATT_GROUP * ATT_HD
    return pl.pallas_call(
        functools.partial(_flash_kernel, seq=seq),
        grid=(nseq, ATT_HKV, nq),
        in_specs=[
            pl.BlockSpec((tq, gw), lambda b, g, i: (b * nq + i, g)),
            pl.BlockSpec((ATT_HD, seq), lambda b, g, i: (g, b)),
            pl.BlockSpec((seq, 2 * ATT_HD), lambda b, g, i: (b, g)),
        ],
        out_specs=pl.BlockSpec((tq, gw), lambda b, g, i: (b * nq + i, g)),
        out_shape=jax.ShapeDtypeStruct((t, ATT_HQ * ATT_HD), BF16),
        compiler_params=_cparams(("parallel", "parallel", "arbitrary")),
        name="flash_gqa",
    )(q_r, k_t, v_ext)


def _merge_kernel(x_ref, gof_ref, gob_ref, gz_ref, roa_ref, rob_ref, rg_ref, att_ref,
                  g0_ref, g1_ref, g2_ref, gn_ref, rn_ref, wg_ref, wr_ref, wa_ref, wo_ref, o_ref):
    go = gof_ref[...].astype(F32) + gob_ref[...].astype(F32)
    ro = roa_ref[...].astype(F32) + rob_ref[...].astype(F32)
    gz = gz_ref[...].astype(F32)
    rg = rg_ref[...].astype(F32)
    ga_parts, rb_parts = [], []
    for h in range(N_HEAD):
        vs = slice(h * DV, (h + 1) * DV)
        gh = go[:, vs]
        gh = gh * lax.rsqrt(jnp.mean(gh * gh, axis=-1, keepdims=True) + EPS) * gn_ref[...]
        ga_parts.append((gh * _silu(gz[:, vs])).astype(BF16))
        rh = ro[:, vs]
        mu = jnp.mean(rh, axis=-1, keepdims=True)
        cen = rh - mu
        var = jnp.mean(cen * cen, axis=-1, keepdims=True)
        rh = cen * lax.rsqrt(var + EPS) * rn_ref[:, vs]
        rb_parts.append((_silu(rg[:, vs]) * rh).astype(BF16))
    branch_a = _dot(jnp.concatenate(ga_parts, axis=1), wg_ref[...])
    branch_b = _dot(jnp.concatenate(rb_parts, axis=1), wr_ref[...])
    branch_c = _dot(att_ref[...], wa_ref[...])
    merged = (_sigmoid(g0_ref[...].astype(F32)) * branch_a
              + _sigmoid(g1_ref[...].astype(F32)) * branch_b
              + _sigmoid(g2_ref[...].astype(F32)) * branch_c)
    o_ref[...] = x_ref[...] + _dot(merged.astype(BF16), wo_ref[...])


def _merge(x, go_f, go_b, ro_a, ro_b, att, proj, gdn_norm, ret_norm, wg, wr, wa, wo, l):
    t = x.shape[0]
    tm = TM_MERGE
    d = D_MODEL

    def tok(col=0):
        return pl.BlockSpec((tm, d), lambda i: (i, col // d))

    return pl.pallas_call(
        _merge_kernel,
        grid=(t // tm,),
        in_specs=[
            tok(), tok(), tok(), tok(C_GZ), tok(), tok(), tok(C_RG), tok(),
            tok(C_GATE), tok(C_GATE + d), tok(C_GATE + 2 * d),
            _const_spec((1, DV)), _const_spec((1, d)),
        ] + [pl.BlockSpec((None, d, d), lambda i: (l, 0, 0), pipeline_mode=pl.Buffered(1))] * 4,
        out_specs=tok(),
        out_shape=jax.ShapeDtypeStruct((t, d), F32),
        compiler_params=_cparams(("parallel",)),
        name="merge",
    )(x, go_f, go_b, proj, ro_a, ro_b, proj, att, proj, proj, proj,
      gdn_norm, ret_norm, wg, wr, wa, wo)


def _rope_tables(seq):
    f32 = np.float32
    pos = np.arange(seq, dtype=f32)
    inv = (f32(ROPE_THETA) ** (-np.arange(0, DK, 2, dtype=f32) / f32(DK))).astype(f32)
    ang = pos[:, None] * inv[None, :]
    c, s = np.cos(ang), np.sin(ang)
    ret_cos = np.concatenate([c, c], axis=1)
    ret_sin = np.concatenate([-s, s], axis=1)
    half = ATT_HD // 2
    inv_a = (f32(ROPE_THETA) ** (-np.arange(0, half, 2, dtype=f32) / f32(half))).astype(f32)
    rows = (np.arange(seq) // GRID_W).astype(f32)
    cols = (np.arange(seq) % GRID_W).astype(f32)
    ar, ac = rows[:, None] * inv_a[None, :], cols[:, None] * inv_a[None, :]
    cr, sr, cc, sc = np.cos(ar), np.sin(ar), np.cos(ac), np.sin(ac)
    att_cos = np.concatenate([cr, cr, cc, cc], axis=1)
    att_sin = np.concatenate([-sr, sr, -sc, sc], axis=1)
    return tuple(jnp.asarray(t, F32) for t in (ret_cos, ret_sin, att_cos, att_sin))


def _ret_tables(decay_logit):
    c = TR_RET
    lg = jax.nn.log_sigmoid(decay_logit.astype(F32))
    idx = jnp.arange(c, dtype=F32)
    rel = idx[:, None] - idx[None, :]
    lf, lb = lg[0][:, None, None], lg[1][:, None, None]
    dsum = (jnp.exp(jnp.where(rel >= 0, rel * lf, -jnp.inf))
            + jnp.exp(jnp.where(rel <= 0, -rel * lb, -jnp.inf)))

    def lanes(tab):
        return jnp.repeat(tab.T, DK, axis=1)

    xi_f = lanes(jnp.exp((idx + 1.0)[None, :] * lg[0][:, None]))
    zeta_f = lanes(jnp.exp((c - 1.0 - idx)[None, :] * lg[0][:, None]))
    xi_b = lanes(jnp.exp((c - idx)[None, :] * lg[1][:, None]))
    zeta_b = lanes(jnp.exp(idx[None, :] * lg[1][:, None]))
    gch = jnp.repeat(jnp.exp(c * lg), DK, axis=1)[:, None, :]
    return dsum, xi_f, zeta_f, xi_b, zeta_b, gch


def _stacked_weights(p):
    w_in = p["w_in"]
    n_gdn = 2 * N_HEAD * DK + 2 * N_HEAD * DV
    rest = w_in[:, :, n_gdn + 16:]
    n_ret = 2 * N_HEAD * DK + 2 * N_HEAD * DV
    n_att = ATT_HQ * ATT_HD + 2 * ATT_HKV * ATT_HD
    w_main = jnp.concatenate(
        [w_in[:, :, :n_gdn], rest[:, :, n_ret + n_att:], rest[:, :, :n_ret + n_att]], axis=2).astype(BF16)
    w_small = jnp.pad(w_in[:, :, n_gdn:n_gdn + 16], ((0, 0), (0, 0), (0, 112))).astype(BF16)
    cast = {k: p[k].astype(BF16) for k in (
        "ffn1_w1", "ffn1_w3", "ffn1_w2", "ffn2_w1", "ffn2_w3", "ffn2_w2",
        "w_branch_gdn", "w_branch_ret", "w_branch_attn", "w_out")}
    return dict(cast, w_main=w_main, w_small=w_small)


def _layer_params(l, p):
    conv_w = jnp.pad(p["gdn_conv"][l].astype(F32), ((0, 8 - CONV_W), (0, 0)))
    alog_row = jnp.pad(p["gdn_A_log"][l].astype(F32).reshape(1, 8), ((0, 0), (8, 112)))
    bias_row = jnp.pad(p["gdn_dt_bias"][l].astype(F32).reshape(1, 8), ((0, 0), (8, 112)))
    return dict(
        ffn1_norm=p["ffn1_norm"][l][None, :], ffn2_norm=p["ffn2_norm"][l][None, :],
        mix_norm=p["mix_norm"][l][None, :],
        conv_w=conv_w, alog_row=alog_row, bias_row=bias_row,
        ret_tabs=_ret_tables(p["ret_decay_logit"][l]),
        gq=p["attn_q_norm"][l][None, :], gk=p["attn_k_norm"][l][None, :],
        gdn_norm=p["gdn_norm"][l][None, :], ret_norm=p["ret_norm"][l][None, :],
    )


def _encoder_layer(x, l, lp, sw, tabs, nseq, seq):
    ret_cos, ret_sin, att_cos, att_sin = tabs
    x = _ffn(x, lp["ffn1_norm"], sw["ffn1_w1"], sw["ffn1_w3"], sw["ffn1_w2"], l)
    proj, small = _inproj(x, lp["mix_norm"], sw["w_main"], sw["w_small"], l)
    u, w, qkm, qd, kd, gt = _gdn_prep(proj, small, lp["conv_w"], lp["alog_row"], lp["bias_row"], seq)
    go_f, go_b, ro_a, ro_b = _recurrences(
        (u, w, qkm, qd, kd, gt), (proj, ret_cos, ret_sin, *lp["ret_tabs"]), nseq, seq)
    q_r, k_t, v_ext = _attn_prep(proj, lp["gq"], lp["gk"], att_cos, att_sin, seq)
    att = _flash(q_r, k_t, v_ext, nseq, seq)
    x = _merge(x, go_f, go_b, ro_a, ro_b, att, proj, lp["gdn_norm"], lp["ret_norm"],
               sw["w_branch_gdn"], sw["w_branch_ret"], sw["w_branch_attn"], sw["w_out"], l)
    return _ffn(x, lp["ffn2_norm"], sw["ffn2_w1"], sw["ffn2_w3"], sw["ffn2_w2"], l)


def _trunk(x3, layers, sw, tabs):
    nseq, seq, d = x3.shape
    x = x3.reshape(nseq * seq, d)
    for l, lp in enumerate(layers):
        x = _encoder_layer(x, l, lp, sw, tabs, nseq, seq)
    return x.reshape(nseq, seq, d)


def kernel(x_prompt, x_sample, ffn1_norm, ffn1_w1, ffn1_w3, ffn1_w2, mix_norm, w_in, gdn_conv, gdn_A_log, gdn_dt_bias, gdn_norm, ret_decay_logit, ret_norm, attn_q_norm, attn_k_norm, w_branch_gdn, w_branch_ret, w_branch_attn, w_out, ffn2_norm, ffn2_w1, ffn2_w3, ffn2_w2):
    p = dict(ffn1_norm=ffn1_norm, ffn1_w1=ffn1_w1, ffn1_w3=ffn1_w3, ffn1_w2=ffn1_w2, mix_norm=mix_norm,
             w_in=w_in, gdn_conv=gdn_conv, gdn_A_log=gdn_A_log, gdn_dt_bias=gdn_dt_bias, gdn_norm=gdn_norm,
             ret_decay_logit=ret_decay_logit, ret_norm=ret_norm, attn_q_norm=attn_q_norm,
             attn_k_norm=attn_k_norm, w_branch_gdn=w_branch_gdn, w_branch_ret=w_branch_ret,
             w_branch_attn=w_branch_attn, w_out=w_out, ffn2_norm=ffn2_norm, ffn2_w1=ffn2_w1,
             ffn2_w3=ffn2_w3, ffn2_w2=ffn2_w2)
    depth = w_in.shape[0]
    layers = [_layer_params(l, p) for l in range(depth)]
    sw = _stacked_weights(p)
    assert x_prompt.shape[1] == x_sample.shape[1]
    tabs = _rope_tables(x_prompt.shape[1])
    return (_trunk(x_prompt, layers, sw, tabs), _trunk(x_sample, layers, sw, tabs))
```

```python
import functools
import math

import jax
import jax.numpy as jnp
import numpy as np
from jax import lax
from jax.experimental import pallas as pl
from jax.experimental.pallas import tpu as pltpu

F32 = jnp.float32
BF16 = jnp.bfloat16

D_MODEL = 1024
D_FF = 2816
EPS = 1e-6
ROPE_THETA = 10000.0
GRID_W = 64
LOG2E = math.log2(math.e)

N_HEAD = 4
DK = 128
DV = 256
CONV_W = 5
CHUNK = 64
ATT_HQ = 8
ATT_HKV = 2
ATT_HD = 128
ATT_GROUP = ATT_HQ // ATT_HKV

C_GQKV = 0
C_GZ = 2048
C_GATE = 3072
C_RQ = 6144
C_RK = 6656
C_RV = 7168
C_RG = 8192
C_AQ = 9216
C_AK = 10240
C_AV = 10496
N_PROJ = 10752

VMEM_LIMIT = 56 * 1024 * 1024

TM_FFN = 512
FF_CHUNKS = ((0, 768), (768, 1536), (1536, 2304), (2304, 2816))
TM_PROJ = 1024
TN_PROJ = 1536
TM_GDN = 256
TR_RET = 256
TM_ATT_PREP = 256
TQ_ATT = 256
TK_ATT = 2048
TM_MERGE = 512


def _cparams(sem):
    return pltpu.CompilerParams(dimension_semantics=sem, vmem_limit_bytes=VMEM_LIMIT)


def _dot(a, b):
    return jnp.dot(a, b, preferred_element_type=F32)


def _dot_nt(a, b):
    return lax.dot_general(a, b, (((1,), (1,)), ((), ())), preferred_element_type=F32)


def _dot_tn(a, b):
    return lax.dot_general(a, b, (((0,), (0,)), ((), ())), preferred_element_type=F32)


def _sigmoid(x):
    return 1.0 / (1.0 + jnp.exp2(x * (-LOG2E)))


def _silu(x):
    return x * _sigmoid(x)


def _rms(x, gain):
    ms = jnp.mean(x * x, axis=-1, keepdims=True)
    return x * lax.rsqrt(ms + EPS) * gain


def _const_spec(shape):
    n = len(shape)
    return pl.BlockSpec(shape, lambda *_: (0,) * n)


def _ffn_kernel(x_ref, g_ref, w1_ref, w3_ref, w2_ref, o_ref):
    x = x_ref[...]
    n = _rms(x, g_ref[...]).astype(BF16)
    acc = jnp.zeros(x.shape, F32)
    for lo, hi in FF_CHUNKS:
        h1 = _dot(n, w1_ref[:, lo:hi])
        h3 = _dot(n, w3_ref[:, lo:hi])
        a = (_silu(h1) * h3).astype(BF16)
        acc = acc + _dot(a, w2_ref[lo:hi, :])
    o_ref[...] = x + 0.5 * acc


def _ffn(x, gain, w1, w3, w2, l):
    t = x.shape[0]
    resident = dict(pipeline_mode=pl.Buffered(1))
    return pl.pallas_call(
        _ffn_kernel,
        grid=(t // TM_FFN,),
        in_specs=[
            pl.BlockSpec((TM_FFN, D_MODEL), lambda i: (i, 0)),
            pl.BlockSpec((1, D_MODEL), lambda i: (0, 0)),
            pl.BlockSpec((None, D_MODEL, D_FF), lambda i: (l, 0, 0), **resident),
            pl.BlockSpec((None, D_MODEL, D_FF), lambda i: (l, 0, 0), **resident),
            pl.BlockSpec((None, D_FF, D_MODEL), lambda i: (l, 0, 0), **resident),
        ],
        out_specs=pl.BlockSpec((TM_FFN, D_MODEL), lambda i: (i, 0)),
        out_shape=jax.ShapeDtypeStruct((t, D_MODEL), F32),
        compiler_params=_cparams(("parallel",)),
        name="ffn",
    )(x, gain, w1, w3, w2)


def _inproj_kernel(x_ref, g_ref, w_ref, ws_ref, o_ref, os_ref, n_sc):
    @pl.when(pl.program_id(1) == 0)
    def _():
        n = _rms(x_ref[...], g_ref[...]).astype(BF16)
        n_sc[...] = n
        os_ref[...] = _dot(n, ws_ref[...])

    o_ref[...] = _dot(n_sc[...], w_ref[...]).astype(BF16)


def _inproj(x, gain, w_main, w_small, l):
    t = x.shape[0]
    return pl.pallas_call(
        _inproj_kernel,
        grid=(t // TM_PROJ, N_PROJ // TN_PROJ),
        in_specs=[
            pl.BlockSpec((TM_PROJ, D_MODEL), lambda i, j: (i, 0)),
            pl.BlockSpec((1, D_MODEL), lambda i, j: (0, 0)),
            pl.BlockSpec((None, D_MODEL, TN_PROJ), lambda i, j: (l, 0, j)),
            pl.BlockSpec((None, D_MODEL, 128), lambda i, j: (l, 0, 0)),
        ],
        out_specs=[
            pl.BlockSpec((TM_PROJ, TN_PROJ), lambda i, j: (i, j)),
            pl.BlockSpec((TM_PROJ, 128), lambda i, j: (i, 0)),
        ],
        out_shape=[
            jax.ShapeDtypeStruct((t, N_PROJ), BF16),
            jax.ShapeDtypeStruct((t, 128), F32),
        ],
        scratch_shapes=[pltpu.VMEM((TM_PROJ, D_MODEL), BF16)],
        compiler_params=_cparams(("parallel", "arbitrary")),
        name="inproj",
    )(x, gain, w_main, w_small)


def _split_hi_lo(x):
    hi = x.astype(BF16)
    lo = (x - hi.astype(F32)).astype(BF16)
    return hi, lo


def _dot01(m01, x):
    hi, lo = _split_hi_lo(x)
    return _dot(m01, hi) + _dot(m01, lo)


def _block_diag(p, mask):
    return jnp.where(mask, jnp.concatenate([p] * 4, axis=0), jnp.zeros((), p.dtype))


def _neumann_inverse(l_mats, eye4, bd_mask):
    qs = [-l for l in l_mats]
    accs = [eye4 + q for q in qs]
    qbs = [q.astype(BF16) for q in qs]
    qs = [_dot(qb, _block_diag(qb, bd_mask)) for qb in qbs]
    for _ in range(4):
        qbs = [q.astype(BF16) for q in qs]
        rs = [_dot(jnp.concatenate([a.astype(BF16), qb], axis=0), _block_diag(qb, bd_mask))
              for a, qb in zip(accs, qbs)]
        accs = [a + r[:CHUNK] for a, r in zip(accs, rs)]
        qs = [r[CHUNK:] for r in rs]
    return [(a + _dot(a.astype(BF16), _block_diag(q.astype(BF16), bd_mask))).astype(BF16)
            for a, q in zip(accs, qs)]


def _gdn_prep_kernel(prev_ref, cur_ref, next_ref, sm_ref, cw_ref, alog_ref, bias_ref,
                     u_ref, w_ref, qkm_ref, qd_ref, kd_ref, gt_ref, ext_sc, *, tiles_per_seq):
    tm = TM_GDN
    i = pl.program_id(0)
    jt = i % tiles_per_seq
    m_prev = (jt != 0).astype(F32)
    m_next = (jt != tiles_per_seq - 1).astype(F32)

    ext_sc[0:16, :] = prev_ref[...].astype(F32) * m_prev
    ext_sc[16:16 + tm, :] = cur_ref[...].astype(F32)
    ext_sc[16 + tm:32 + tm, :] = next_ref[...].astype(F32) * m_next
    nch_all = 2 * N_HEAD * DK + N_HEAD * DV
    nt = tm // 8
    x3 = ext_sc[8:24 + tm, :].reshape(nt + 2, 8, nch_all)
    sub = lax.broadcasted_iota(jnp.int32, (nt, 8, nch_all), 1)
    conv = x3[1:1 + nt] * cw_ref[CONV_W // 2:CONV_W // 2 + 1, :]
    for w in range(CONV_W):
        s = w - CONV_W // 2
        if s == 0:
            continue
        rot = pltpu.roll(x3, (-s) % 8, axis=1)
        if s > 0:
            shifted = jnp.where(sub < 8 - s, rot[1:1 + nt], rot[2:2 + nt])
        else:
            shifted = jnp.where(sub >= -s, rot[1:1 + nt], rot[0:nt])
        conv = conv + shifted * cw_ref[w:w + 1, :]
    act = _silu(conv.reshape(tm, nch_all))

    nqk = N_HEAD * DK
    q_parts, k_parts = [], []
    for h in range(N_HEAD):
        qh = act[:, h * DK:(h + 1) * DK]
        kh = act[:, nqk + h * DK:nqk + (h + 1) * DK]
        q_parts.append(qh * lax.rsqrt(jnp.sum(qh * qh, axis=-1, keepdims=True) + EPS) * (DK ** -0.5))
        k_parts.append(kh * lax.rsqrt(jnp.sum(kh * kh, axis=-1, keepdims=True) + EPS))
    q_n = jnp.concatenate(q_parts, axis=1)
    k_n = jnp.concatenate(k_parts, axis=1)
    v = act[:, 2 * nqk:]

    sm = sm_ref[...]
    beta8 = _sigmoid(sm)
    zb = sm + bias_ref[...]
    softplus = jnp.maximum(zb, 0.0) + jnp.log(1.0 + jnp.exp(-jnp.abs(zb)))
    g8 = -jnp.exp(alog_ref[...]) * softplus

    def expand_mat(width, row0):
        r = lax.broadcasted_iota(jnp.int32, (128, 8 * width), 0)
        c = lax.broadcasted_iota(jnp.int32, (128, 8 * width), 1) // width
        return (r == c + row0).astype(BF16)

    e64b, e128b = expand_mat(64, 0), expand_mat(128, 0)
    e64g, e128g = expand_mat(64, 8), expand_mat(128, 8)
    beta8b = beta8.astype(BF16)
    beta64 = _dot(beta8b, e64b)
    beta128 = _dot(beta8b, e128b)
    g64 = _dot01_rhs(g8, e64g)
    g128 = _dot01_rhs(g8, e128g)

    ri = lax.broadcasted_iota(jnp.int32, (tm, tm), 0)
    ci = lax.broadcasted_iota(jnp.int32, (tm, tm), 1)
    same = (ri // CHUNK) == (ci // CHUNK)
    m_le = (same & (ci <= ri)).astype(BF16)
    m_ge = (same & (ci >= ri)).astype(BF16)
    m_lt = (same & (ci < ri)).astype(BF16)
    m_gt = (same & (ci > ri)).astype(BF16)

    tl = lax.broadcasted_iota(jnp.int32, (tm, 4 * CHUNK), 0) % CHUNK
    jl = lax.broadcasted_iota(jnp.int32, (tm, 4 * CHUNK), 1) % CHUNK
    hw = N_HEAD * CHUNK
    delta_f = _dot01(m_le, jnp.where(tl > jl, g64[:, :hw], 0.0))
    delta_b = _dot01(m_ge, jnp.where(tl < jl, g64[:, hw:], 0.0))
    hk = N_HEAD * DK
    sums_f = _dot01(jnp.concatenate([m_le, m_gt], axis=0), g128[:, :hk])
    sums_b = _dot01(jnp.concatenate([m_ge, m_lt], axis=0), g128[:, hk:])
    gc_f, rest_f = sums_f[:tm], sums_f[tm:]
    gc_b, rest_b = sums_b[:tm], sums_b[tm:]

    il = lax.broadcasted_iota(jnp.int32, (CHUNK, 4 * CHUNK), 0)
    jc = lax.broadcasted_iota(jnp.int32, (CHUNK, 4 * CHUNK), 1) % CHUNK
    eye4 = (il == jc).astype(F32)
    bd_mask = (lax.broadcasted_iota(jnp.int32, (4 * CHUNK, 4 * CHUNK), 0) // CHUNK
               == lax.broadcasted_iota(jnp.int32, (4 * CHUNK, 4 * CHUNK), 1) // CHUNK)
    bdk_mask = (lax.broadcasted_iota(jnp.int32, (4 * CHUNK, hk), 0) // CHUNK
                == lax.broadcasted_iota(jnp.int32, (4 * CHUNK, hk), 1) // DK)

    k_nb = k_n.astype(BF16)
    q_nb = q_n.astype(BF16)
    per_dir = []
    for d, gc, rest, delta, incl, strict in (
            (0, gc_f, rest_f, delta_f, il >= jc, il > jc),
            (1, gc_b, rest_b, delta_b, il <= jc, il < jc)):
        e_gc = jnp.exp(gc)
        qd_ref[d] = (q_n * e_gc).astype(BF16)
        kd_ref[d] = (k_n * jnp.exp(rest)).astype(BF16)
        b128 = beta128[:, d * hk:(d + 1) * hk]
        kbe = (k_n * b128 * e_gc).astype(BF16)
        for c in range(tm // CHUNK):
            r0 = c * CHUNK
            gt_ref[d, c] = jnp.exp(gc[r0:r0 + 1, :] + rest[r0:r0 + 1, :])
        per_dir.append((d, delta, incl, strict, b128, kbe, beta64[:, d * hw:(d + 1) * hw]))

    l_mats, keys = [], []
    for c in range(tm // CHUNK):
        r0 = c * CHUNK
        k_c = k_nb[r0:r0 + CHUNK]
        bdk = jnp.where(bdk_mask, jnp.concatenate([k_c] * 4, axis=0), jnp.zeros((), BF16))
        gq = _dot_nt(jnp.concatenate([k_c, q_nb[r0:r0 + CHUNK]], axis=0), bdk)
        kk, qk = gq[:CHUNK], gq[CHUNK:]
        for d, delta, incl, strict, b128, kbe, b64 in per_dir:
            dm = jnp.where(incl, jnp.exp(delta[r0:r0 + CHUNK]), 0.0)
            l_mats.append(jnp.where(strict, b64[r0:r0 + CHUNK] * kk * dm, 0.0))
            qkm_ref[d, r0:r0 + CHUNK, :] = (qk * dm).astype(BF16)
            keys.append((r0, d, b128, kbe))
    t_alls = _neumann_inverse(l_mats, eye4, bd_mask)
    for (r0, d, b128, kbe), t_all in zip(keys, t_alls):
        for h in range(N_HEAD):
            bh = b128[r0:r0 + CHUNK, h * DK:(h + 1) * DK]
            vh = v[r0:r0 + CHUNK, h * DV:(h + 1) * DV]
            vb = jnp.concatenate([vh[:, :DK] * bh, vh[:, DK:] * bh], axis=1).astype(BF16)
            rhs = jnp.concatenate([vb, kbe[r0:r0 + CHUNK, h * DK:(h + 1) * DK]], axis=1)
            uw = _dot(t_all[:, h * CHUNK:(h + 1) * CHUNK], rhs)
            u_ref[d, r0:r0 + CHUNK, h * DV:(h + 1) * DV] = uw[:, :DV].astype(BF16)
            w_ref[d, r0:r0 + CHUNK, h * DK:(h + 1) * DK] = uw[:, DV:].astype(BF16)


def _dot01_rhs(x, m01):
    hi, lo = _split_hi_lo(x)
    return _dot(hi, m01) + _dot(lo, m01)


def _gdn_prep(proj, small, conv_w, alog_row, bias_row, seq):
    t = proj.shape[0]
    tm = TM_GDN
    tps = seq // tm
    nqkv = 2 * N_HEAD * DK + N_HEAD * DV
    hb = tm // 16
    last16 = t // 16 - 1
    kern = functools.partial(_gdn_prep_kernel, tiles_per_seq=tps)
    return pl.pallas_call(
        kern,
        grid=(t // tm,),
        in_specs=[
            pl.BlockSpec((16, nqkv), lambda i: (jnp.maximum(i * hb - 1, 0), 0)),
            pl.BlockSpec((tm, nqkv), lambda i: (i, 0)),
            pl.BlockSpec((16, nqkv), lambda i: (jnp.minimum((i + 1) * hb, last16), 0)),
            pl.BlockSpec((tm, 128), lambda i: (i, 0)),
            _const_spec((8, nqkv)),
            _const_spec((1, 128)),
            _const_spec((1, 128)),
        ],
        out_specs=[
            pl.BlockSpec((2, tm, N_HEAD * DV), lambda i: (0, i, 0)),
            pl.BlockSpec((2, tm, N_HEAD * DK), lambda i: (0, i, 0)),
            pl.BlockSpec((2, tm, N_HEAD * CHUNK), lambda i: (0, i, 0)),
            pl.BlockSpec((2, tm, N_HEAD * DK), lambda i: (0, i, 0)),
            pl.BlockSpec((2, tm, N_HEAD * DK), lambda i: (0, i, 0)),
            pl.BlockSpec((2, tm // CHUNK, 1, N_HEAD * DK), lambda i: (0, i, 0, 0)),
        ],
        out_shape=[
            jax.ShapeDtypeStruct((2, t, N_HEAD * DV), BF16),
            jax.ShapeDtypeStruct((2, t, N_HEAD * DK), BF16),
            jax.ShapeDtypeStruct((2, t, N_HEAD * CHUNK), BF16),
            jax.ShapeDtypeStruct((2, t, N_HEAD * DK), BF16),
            jax.ShapeDtypeStruct((2, t, N_HEAD * DK), BF16),
            jax.ShapeDtypeStruct((2, t // CHUNK, 1, N_HEAD * DK), F32),
        ],
        scratch_shapes=[pltpu.VMEM((tm + 32, nqkv), F32)],
        compiler_params=_cparams(("parallel",)),
        name="gdn_prep",
    )(proj, proj, proj, small, conv_w, alog_row, bias_row)


def _gdn_scan_kernel(uf_ref, wf_ref, qkf_ref, qdf_ref, kdf_ref, gtf_ref,
                     ub_ref, wb_ref, qkb_ref, qdb_ref, kdb_ref, gtb_ref,
                     of_ref, ob_ref, s_sc, after_step):
    @pl.when(pl.program_id(1) == 0)
    def _():
        s_sc[...] = jnp.zeros(s_sc.shape, F32)

    nch = TM_GDN // CHUNK
    dirs = (
        (0, uf_ref, wf_ref, qkf_ref, qdf_ref, kdf_ref, gtf_ref, of_ref, range(nch)),
        (1, ub_ref, wb_ref, qkb_ref, qdb_ref, kdb_ref, gtb_ref, ob_ref, range(nch - 1, -1, -1)),
    )
    chains = [(dr, h) for dr in dirs for h in range(N_HEAD)]
    states = [s_sc[d * N_HEAD + h] for (d, *_), h in chains]
    for step in range(nch):
        wss = []
        for ((d, u_ref, w_ref, qk_ref, qd_ref, kd_ref, gt_ref, o_ref, order), h), s in zip(chains, states):
            r0 = order[step] * CHUNK
            wq = jnp.concatenate([w_ref[0, r0:r0 + CHUNK, h * DK:(h + 1) * DK],
                                  qd_ref[0, r0:r0 + CHUNK, h * DK:(h + 1) * DK]], axis=0)
            wss.append(_dot(wq, s.astype(BF16)))
        vbs = []
        for ((d, u_ref, w_ref, qk_ref, qd_ref, kd_ref, gt_ref, o_ref, order), h), ws in zip(chains, wss):
            r0 = order[step] * CHUNK
            v_new = u_ref[0, r0:r0 + CHUNK, h * DV:(h + 1) * DV].astype(F32) - ws[:CHUNK]
            vbs.append(v_new.astype(BF16))
        new_states = []
        for ((d, u_ref, w_ref, qk_ref, qd_ref, kd_ref, gt_ref, o_ref, order), h), s, vb in zip(chains, states, vbs):
            c = order[step]
            r0 = c * CHUNK
            gt = gt_ref[0, c, :, h * DK:(h + 1) * DK]
            gtb = jnp.concatenate([gt, gt], axis=1)
            new_states.append(s * gtb + _dot_tn(kd_ref[0, r0:r0 + CHUNK, h * DK:(h + 1) * DK], vb))
        for ((d, u_ref, w_ref, qk_ref, qd_ref, kd_ref, gt_ref, o_ref, order), h), ws, vb in zip(chains, wss, vbs):
            r0 = order[step] * CHUNK
            o = ws[CHUNK:] + _dot(qk_ref[0, r0:r0 + CHUNK, h * CHUNK:(h + 1) * CHUNK], vb)
            o_ref[r0:r0 + CHUNK, h * DV:(h + 1) * DV] = o.astype(BF16)
        states = new_states
        after_step(step)
    for ((d, *_), h), s in zip(chains, states):
        s_sc[d * N_HEAD + h] = s


def _gdn_scan_specs(u, w, qkm, qd, kd, gt, seq):
    tm = TM_GDN
    tps = seq // tm
    nch = tm // CHUNK

    def fwd(width):
        return pl.BlockSpec((1, tm, width), lambda b, j: (0, b * tps + j, 0))

    def bwd(width):
        return pl.BlockSpec((1, tm, width), lambda b, j: (1, b * tps + tps - 1 - j, 0))

    gt_f = pl.BlockSpec((1, nch, 1, N_HEAD * DK), lambda b, j: (0, b * tps + j, 0, 0))
    gt_b = pl.BlockSpec((1, nch, 1, N_HEAD * DK), lambda b, j: (1, b * tps + tps - 1 - j, 0, 0))
    widths = (N_HEAD * DV, N_HEAD * DK, N_HEAD * CHUNK, N_HEAD * DK, N_HEAD * DK)
    in_specs = [fwd(x) for x in widths] + [gt_f] + [bwd(x) for x in widths] + [gt_b]
    return in_specs, (u, w, qkm, qd, kd, gt, u, w, qkm, qd, kd, gt)


def _rope_half(x, cos, sin_signed):
    return x * cos + pltpu.roll(x, 64, axis=1) * sin_signed


def _ret_heads(qf_ref, kf_ref, vf_ref, cosf_ref, sinf_ref,
               qb_ref, kb_ref, vb_ref, cosb_ref, sinb_ref,
               dsum_ref, xif_ref, zf_ref, xib_ref, zb_ref, gch_ref,
               oa_ref, ob_ref, r_sc):
    @pl.when(pl.program_id(1) == 0)
    def _():
        r_sc[...] = jnp.zeros(r_sc.shape, F32)

    scale = DK ** -0.5
    cosf, sinf = cosf_ref[...], sinf_ref[...]
    cosb, sinb = cosb_ref[...], sinb_ref[...]

    def head(h):
        sl = slice(h * DK, (h + 1) * DK)
        vs = slice(h * DV, (h + 1) * DV)
        q = _rope_half(qf_ref[:, sl].astype(F32), cosf, sinf)
        k = _rope_half(kf_ref[:, sl].astype(F32), cosf, sinf) * scale
        v = vf_ref[:, vs]
        s = (_dot_nt(q.astype(BF16), k.astype(BF16)) * dsum_ref[h]).astype(BF16)
        o = _dot(s, v)
        rf = r_sc[h]
        o = o + _dot((q * xif_ref[:, sl]).astype(BF16), rf.astype(BF16))
        gf = gch_ref[0, :, sl]
        r_sc[h] = rf * jnp.concatenate([gf, gf], axis=1) + _dot_tn((k * zf_ref[:, sl]).astype(BF16), v)
        oa_ref[:, vs] = o.astype(BF16)
        q2 = _rope_half(qb_ref[:, sl].astype(F32), cosb, sinb)
        k2 = _rope_half(kb_ref[:, sl].astype(F32), cosb, sinb) * scale
        v2 = vb_ref[:, vs]
        rb = r_sc[N_HEAD + h]
        ob_ref[:, vs] = _dot((q2 * xib_ref[:, sl]).astype(BF16), rb.astype(BF16)).astype(BF16)
        gb = gch_ref[1, :, sl]
        r_sc[N_HEAD + h] = rb * jnp.concatenate([gb, gb], axis=1) + _dot_tn((k2 * zb_ref[:, sl]).astype(BF16), v2)

    return head


def _retention_specs(proj, cos_t, sin_t, dsum, xi_f, zeta_f, xi_b, zeta_b, gch, seq):
    tr = TR_RET
    tps = seq // tr
    hk, hv = N_HEAD * DK, N_HEAD * DV

    def tok(width, col, mirror):
        cb = col // width
        if mirror:
            return pl.BlockSpec((tr, width), lambda b, j: (b * tps + tps - 1 - j, cb))
        return pl.BlockSpec((tr, width), lambda b, j: (b * tps + j, cb))

    def pos(mirror):
        if mirror:
            return pl.BlockSpec((tr, DK), lambda b, j: (tps - 1 - j, 0))
        return pl.BlockSpec((tr, DK), lambda b, j: (j, 0))

    def side(mirror):
        return [tok(hk, C_RQ, mirror), tok(hk, C_RK, mirror), tok(hv, C_RV, mirror), pos(mirror), pos(mirror)]

    in_specs = side(False) + side(True) + [
        _const_spec((N_HEAD, tr, tr)),
        _const_spec((tr, hk)), _const_spec((tr, hk)), _const_spec((tr, hk)), _const_spec((tr, hk)),
        _const_spec((2, 1, hk)),
    ]
    args = (proj, proj, proj, cos_t, sin_t, proj, proj, proj, cos_t, sin_t,
            dsum, xi_f, zeta_f, xi_b, zeta_b, gch)
    return in_specs, args


N_SCAN_IN = 12
N_RET_IN = 16


def _recurrent_kernel(*refs):
    scan_in = refs[:N_SCAN_IN]
    ret_in = refs[N_SCAN_IN:N_SCAN_IN + N_RET_IN]
    gof_ref, gob_ref, roa_ref, rob_ref, s_sc, r_sc = refs[N_SCAN_IN + N_RET_IN:]
    assert TM_GDN // CHUNK == N_HEAD
    ret_head = _ret_heads(*ret_in, roa_ref, rob_ref, r_sc)
    _gdn_scan_kernel(*scan_in, gof_ref, gob_ref, s_sc, ret_head)


def _recurrences(scan_args, ret_args, nseq, seq):
    assert TM_GDN == TR_RET
    tm = TM_GDN
    tps = seq // tm
    hv = N_HEAD * DV
    scan_specs, scan_ops = _gdn_scan_specs(*scan_args, seq)
    ret_specs, ret_ops = _retention_specs(*ret_args, seq)
    assert len(scan_specs) == N_SCAN_IN and len(ret_specs) == N_RET_IN
    t = scan_ops[0].shape[1]
    fwd_out = pl.BlockSpec((tm, hv), lambda b, j: (b * tps + j, 0))
    bwd_out = pl.BlockSpec((tm, hv), lambda b, j: (b * tps + tps - 1 - j, 0))
    return pl.pallas_call(
        _recurrent_kernel,
        grid=(nseq, tps),
        in_specs=scan_specs + ret_specs,
        out_specs=[fwd_out, bwd_out, fwd_out, bwd_out],
        out_shape=[jax.ShapeDtypeStruct((t, hv), BF16)] * 4,
        scratch_shapes=[pltpu.VMEM((2 * N_HEAD, DK, DV), F32), pltpu.VMEM((2 * N_HEAD, DK, DV), F32)],
        compiler_params=_cparams(("parallel", "arbitrary")),
        name="recurrences",
    )(*scan_ops, *ret_ops)


def _axial_rope(x, cos, sin_signed, swap):
    return x * cos + _dot01_rhs(x, swap) * sin_signed


def _attn_prep_kernel(q_ref, k_ref, v_ref, gq_ref, gk_ref, cos_ref, sin_ref, qo_ref, kt_ref, vx_ref):
    cos, sin = cos_ref[...], sin_ref[...]
    src = lax.broadcasted_iota(jnp.int32, (ATT_HD, ATT_HD), 0)
    dst = lax.broadcasted_iota(jnp.int32, (ATT_HD, ATT_HD), 1)
    fq = (src == jnp.where((dst % 64) < 32, dst + 32, dst - 32)).astype(BF16)
    scale = ATT_HD ** -0.5 * LOG2E
    heads = [(q_ref, h, gq_ref) for h in range(ATT_HQ)] + [(k_ref, h, gk_ref) for h in range(ATT_HKV)]
    xs = [ref[:, h * ATT_HD:(h + 1) * ATT_HD].astype(F32) for ref, h, _ in heads]
    xs = [_rms(x, g[...]) for x, (_, _, g) in zip(xs, heads)]
    xs = [_axial_rope(x, cos, sin, fq) for x in xs]
    for h in range(ATT_HQ):
        qo_ref[:, h * ATT_HD:(h + 1) * ATT_HD] = (xs[h] * scale).astype(BF16)
    for h in range(ATT_HKV):
        sl = slice(h * ATT_HD, (h + 1) * ATT_HD)
        kt_ref[sl, :] = xs[ATT_HQ + h].T.astype(BF16)
        vx_ref[:, 2 * h * ATT_HD:(2 * h + 1) * ATT_HD] = v_ref[:, sl]
        vx_ref[:, (2 * h + 1) * ATT_HD:(2 * h + 2) * ATT_HD] = jnp.ones((v_ref.shape[0], ATT_HD), BF16)


def _attn_prep(proj, gq, gk, cos_t, sin_t, seq):
    t = proj.shape[0]
    tm = TM_ATT_PREP
    tps = seq // tm
    nq, nkv = ATT_HQ * ATT_HD, ATT_HKV * ATT_HD
    return pl.pallas_call(
        _attn_prep_kernel,
        grid=(t // tm,),
        in_specs=[
            pl.BlockSpec((tm, nq), lambda i: (i, C_AQ // nq)),
            pl.BlockSpec((tm, nkv), lambda i: (i, C_AK // nkv)),
            pl.BlockSpec((tm, nkv), lambda i: (i, C_AV // nkv)),
            _const_spec((1, ATT_HD)),
            _const_spec((1, ATT_HD)),
            pl.BlockSpec((tm, ATT_HD), lambda i: (i % tps, 0)),
            pl.BlockSpec((tm, ATT_HD), lambda i: (i % tps, 0)),
        ],
        out_specs=[
            pl.BlockSpec((tm, nq), lambda i: (i, 0)),
            pl.BlockSpec((nkv, tm), lambda i: (0, i)),
            pl.BlockSpec((tm, 2 * nkv), lambda i: (i, 0)),
        ],
        out_shape=[
            jax.ShapeDtypeStruct((t, nq), BF16),
            jax.ShapeDtypeStruct((nkv, t), BF16),
            jax.ShapeDtypeStruct((t, 2 * nkv), BF16),
        ],
        compiler_params=_cparams(("parallel",)),
        name="attn_prep",
    )(proj, proj, proj, gq, gk, cos_t, sin_t)


def _flash_kernel(q_ref, kt_ref, v_ref, o_ref, *, seq):
    tq = TQ_ATT
    tk = min(TK_ATT, seq)
    rows = ATT_GROUP * tq
    nk = seq // tk
    q4 = jnp.concatenate([q_ref[:, h * ATT_HD:(h + 1) * ATT_HD] for h in range(ATT_GROUP)], axis=0)

    def scores(j):
        s = _dot(q4, kt_ref[:, j * tk:(j + 1) * tk])
        return s, jnp.max(s, axis=-1, keepdims=True)

    def update(j, s, smax, m, acc):
        m_new = jnp.maximum(m, smax)
        alpha = jnp.exp2(m - m_new)
        p = jnp.exp2(s - m_new).astype(BF16)
        return m_new, alpha * acc + _dot(p, v_ref[j * tk:(j + 1) * tk, :])

    m = jnp.full((rows, 1), -jnp.inf, F32)
    acc = jnp.zeros((rows, 2 * ATT_HD), F32)
    s, smax = scores(0)
    for j in range(nk):
        if j + 1 < nk:
            s_next, smax_next = scores(j + 1)
        m, acc = update(j, s, smax, m, acc)
        if j + 1 < nk:
            s, smax = s_next, smax_next
    out = acc[:, :ATT_HD] / acc[:, ATT_HD:]
    for h in range(ATT_GROUP):
        o_ref[:, h * ATT_HD:(h + 1) * ATT_HD] = out[h * tq:(h + 1) * tq].astype(BF16)


def _flash(q_r, k_t, v_ext, nseq, seq):
    t = q_r.shape[0]
    tq = TQ_ATT
    nq = seq // tq
    gw = ATT_GROUP * ATT_HD
    return pl.pallas_call(
        functools.partial(_flash_kernel, seq=seq),
        grid=(nseq, ATT_HKV, nq),
        in_specs=[
            pl.BlockSpec((tq, gw), lambda b, g, i: (b * nq + i, g)),
            pl.BlockSpec((ATT_HD, seq), lambda b, g, i: (g, b)),
            pl.BlockSpec((seq, 2 * ATT_HD), lambda b, g, i: (b, g)),
        ],
        out_specs=pl.BlockSpec((tq, gw), lambda b, g, i: (b * nq + i, g)),
        out_shape=jax.ShapeDtypeStruct((t, ATT_HQ * ATT_HD), BF16),
        compiler_params=_cparams(("parallel", "parallel", "arbitrary")),
        name="flash_gqa",
    )(q_r, k_t, v_ext)


def _merge_kernel(x_ref, gof_ref, gob_ref, gz_ref, roa_ref, rob_ref, rg_ref, att_ref,
                  g0_ref, g1_ref, g2_ref, gn_ref, rn_ref, wg_ref, wr_ref, wa_ref, wo_ref, o_ref):
    go = gof_ref[...].astype(F32) + gob_ref[...].astype(F32)
    ro = roa_ref[...].astype(F32) + rob_ref[...].astype(F32)
    gz = gz_ref[...].astype(F32)
    rg = rg_ref[...].astype(F32)
    ga_parts, rb_parts = [], []
    for h in range(N_HEAD):
        vs = slice(h * DV, (h + 1) * DV)
        gh = go[:, vs]
        gh = gh * lax.rsqrt(jnp.mean(gh * gh, axis=-1, keepdims=True) + EPS) * gn_ref[...]
        ga_parts.append((gh * _silu(gz[:, vs])).astype(BF16))
        rh = ro[:, vs]
        mu = jnp.mean(rh, axis=-1, keepdims=True)
        cen = rh - mu
        var = jnp.mean(cen * cen, axis=-1, keepdims=True)
        rh = cen * lax.rsqrt(var + EPS) * rn_ref[:, vs]
        rb_parts.append((_silu(rg[:, vs]) * rh).astype(BF16))
    branch_a = _dot(jnp.concatenate(ga_parts, axis=1), wg_ref[...])
    branch_b = _dot(jnp.concatenate(rb_parts, axis=1), wr_ref[...])
    branch_c = _dot(att_ref[...], wa_ref[...])
    merged = (_sigmoid(g0_ref[...].astype(F32)) * branch_a
              + _sigmoid(g1_ref[...].astype(F32)) * branch_b
              + _sigmoid(g2_ref[...].astype(F32)) * branch_c)
    o_ref[...] = x_ref[...] + _dot(merged.astype(BF16), wo_ref[...])


def _merge(x, go_f, go_b, ro_a, ro_b, att, proj, gdn_norm, ret_norm, wg, wr, wa, wo, l):
    t = x.shape[0]
    tm = TM_MERGE
    d = D_MODEL

    def tok(col=0):
        return pl.BlockSpec((tm, d), lambda i: (i, col // d))

    return pl.pallas_call(
        _merge_kernel,
        grid=(t // tm,),
        in_specs=[
            tok(), tok(), tok(), tok(C_GZ), tok(), tok(), tok(C_RG), tok(),
            tok(C_GATE), tok(C_GATE + d), tok(C_GATE + 2 * d),
            _const_spec((1, DV)), _const_spec((1, d)),
        ] + [pl.BlockSpec((None, d, d), lambda i: (l, 0, 0), pipeline_mode=pl.Buffered(1))] * 4,
        out_specs=tok(),
        out_shape=jax.ShapeDtypeStruct((t, d), F32),
        compiler_params=_cparams(("parallel",)),
        name="merge",
    )(x, go_f, go_b, proj, ro_a, ro_b, proj, att, proj, proj, proj,
      gdn_norm, ret_norm, wg, wr, wa, wo)


def _rope_tables(seq):
    f32 = np.float32
    pos = np.arange(seq, dtype=f32)
    inv = (f32(ROPE_THETA) ** (-np.arange(0, DK, 2, dtype=f32) / f32(DK))).astype(f32)
    ang = pos[:, None] * inv[None, :]
    c, s = np.cos(ang), np.sin(ang)
    ret_cos = np.concatenate([c, c], axis=1)
    ret_sin = np.concatenate([-s, s], axis=1)
    half = ATT_HD // 2
    inv_a = (f32(ROPE_THETA) ** (-np.arange(0, half, 2, dtype=f32) / f32(half))).astype(f32)
    rows = (np.arange(seq) // GRID_W).astype(f32)
    cols = (np.arange(seq) % GRID_W).astype(f32)
    ar, ac = rows[:, None] * inv_a[None, :], cols[:, None] * inv_a[None, :]
    cr, sr, cc, sc = np.cos(ar), np.sin(ar), np.cos(ac), np.sin(ac)
    att_cos = np.concatenate([cr, cr, cc, cc], axis=1)
    att_sin = np.concatenate([-sr, sr, -sc, sc], axis=1)
    return tuple(jnp.asarray(t, F32) for t in (ret_cos, ret_sin, att_cos, att_sin))


def _ret_tables(decay_logit):
    c = TR_RET
    lg = jax.nn.log_sigmoid(decay_logit.astype(F32))
    idx = jnp.arange(c, dtype=F32)
    rel = idx[:, None] - idx[None, :]
    lf, lb = lg[0][:, None, None], lg[1][:, None, None]
    dsum = (jnp.exp(jnp.where(rel >= 0, rel * lf, -jnp.inf))
            + jnp.exp(jnp.where(rel <= 0, -rel * lb, -jnp.inf)))

    def lanes(tab):
        return jnp.repeat(tab.T, DK, axis=1)

    xi_f = lanes(jnp.exp((idx + 1.0)[None, :] * lg[0][:, None]))
    zeta_f = lanes(jnp.exp((c - 1.0 - idx)[None, :] * lg[0][:, None]))
    xi_b = lanes(jnp.exp((c - idx)[None, :] * lg[1][:, None]))
    zeta_b = lanes(jnp.exp(idx[None, :] * lg[1][:, None]))
    gch = jnp.repeat(jnp.exp(c * lg), DK, axis=1)[:, None, :]
    return dsum, xi_f, zeta_f, xi_b, zeta_b, gch


def _stacked_weights(p):
    w_in = p["w_in"]
    n_gdn = 2 * N_HEAD * DK + 2 * N_HEAD * DV
    rest = w_in[:, :, n_gdn + 16:]
    n_ret = 2 * N_HEAD * DK + 2 * N_HEAD * DV
    n_att = ATT_HQ * ATT_HD + 2 * ATT_HKV * ATT_HD
    w_main = jnp.concatenate(
        [w_in[:, :, :n_gdn], rest[:, :, n_ret + n_att:], rest[:, :, :n_ret + n_att]], axis=2).astype(BF16)
    w_small = jnp.pad(w_in[:, :, n_gdn:n_gdn + 16], ((0, 0), (0, 0), (0, 112))).astype(BF16)
    cast = {k: p[k].astype(BF16) for k in (
        "ffn1_w1", "ffn1_w3", "ffn1_w2", "ffn2_w1", "ffn2_w3", "ffn2_w2",
        "w_branch_gdn", "w_branch_ret", "w_branch_attn", "w_out")}
    return dict(cast, w_main=w_main, w_small=w_small)


def _layer_params(l, p):
    conv_w = jnp.pad(p["gdn_conv"][l].astype(F32), ((0, 8 - CONV_W), (0, 0)))
    alog_row = jnp.pad(p["gdn_A_log"][l].astype(F32).reshape(1, 8), ((0, 0), (8, 112)))
    bias_row = jnp.pad(p["gdn_dt_bias"][l].astype(F32).reshape(1, 8), ((0, 0), (8, 112)))
    return dict(
        ffn1_norm=p["ffn1_norm"][l][None, :], ffn2_norm=p["ffn2_norm"][l][None, :],
        mix_norm=p["mix_norm"][l][None, :],
        conv_w=conv_w, alog_row=alog_row, bias_row=bias_row,
        ret_tabs=_ret_tables(p["ret_decay_logit"][l]),
        gq=p["attn_q_norm"][l][None, :], gk=p["attn_k_norm"][l][None, :],
        gdn_norm=p["gdn_norm"][l][None, :], ret_norm=p["ret_norm"][l][None, :],
    )


def _encoder_layer(x, l, lp, sw, tabs, nseq, seq):
    ret_cos, ret_sin, att_cos, att_sin = tabs
    x = _ffn(x, lp["ffn1_norm"], sw["ffn1_w1"], sw["ffn1_w3"], sw["ffn1_w2"], l)
    proj, small = _inproj(x, lp["mix_norm"], sw["w_main"], sw["w_small"], l)
    u, w, qkm, qd, kd, gt = _gdn_prep(proj, small, lp["conv_w"], lp["alog_row"], lp["bias_row"], seq)
    go_f, go_b, ro_a, ro_b = _recurrences(
        (u, w, qkm, qd, kd, gt), (proj, ret_cos, ret_sin, *lp["ret_tabs"]), nseq, seq)
    q_r, k_t, v_ext = _attn_prep(proj, lp["gq"], lp["gk"], att_cos, att_sin, seq)
    att = _flash(q_r, k_t, v_ext, nseq, seq)
    x = _merge(x, go_f, go_b, ro_a, ro_b, att, proj, lp["gdn_norm"], lp["ret_norm"],
               sw["w_branch_gdn"], sw["w_branch_ret"], sw["w_branch_attn"], sw["w_out"], l)
    return _ffn(x, lp["ffn2_norm"], sw["ffn2_w1"], sw["ffn2_w3"], sw["ffn2_w2"], l)


def _trunk(x3, layers, sw, tabs):
    nseq, seq, d = x3.shape
    x = x3.reshape(nseq * seq, d)
    for l, lp in enumerate(layers):
        x = _encoder_layer(x, l, lp, sw, tabs, nseq, seq)
    return x.reshape(nseq, seq, d)


def kernel(x_prompt, x_sample, ffn1_norm, ffn1_w1, ffn1_w3, ffn1_w2, mix_norm, w_in, gdn_conv, gdn_A_log, gdn_dt_bias, gdn_norm, ret_decay_logit, ret_norm, attn_q_norm, attn_k_norm, w_branch_gdn, w_branch_ret, w_branch_attn, w_out, ffn2_norm, ffn2_w1, ffn2_w3, ffn2_w2):
    p = dict(ffn1_norm=ffn1_norm, ffn1_w1=ffn1_w1, ffn1_w3=ffn1_w3, ffn1_w2=ffn1_w2, mix_norm=mix_norm,
             w_in=w_in, gdn_conv=gdn_conv, gdn_A_log=gdn_A_log, gdn_dt_bias=gdn_dt_bias, gdn_norm=gdn_norm,
             ret_decay_logit=ret_decay_logit, ret_norm=ret_norm, attn_q_norm=attn_q_norm,
             attn_k_norm=attn_k_norm, w_branch_gdn=w_branch_gdn, w_branch_ret=w_branch_ret,
             w_branch_attn=w_branch_attn, w_out=w_out, ffn2_norm=ffn2_norm, ffn2_w1=ffn2_w1,
             ffn2_w3=ffn2_w3, ffn2_w2=ffn2_w2)
    depth = w_in.shape[0]
    layers = [_layer_params(l, p) for l in range(depth)]
    sw = _stacked_weights(p)
    assert x_prompt.shape[1] == x_sample.shape[1]
    tabs = _rope_tables(x_prompt.shape[1])
    return (_trunk(x_prompt, layers, sw, tabs), _trunk(x_sample, layers, sw, tabs))
```

```python
import functools
import math

import jax
import jax.numpy as jnp
import numpy as np
from jax import lax
from jax.experimental import pallas as pl
from jax.experimental.pallas import tpu as pltpu

F32 = jnp.float32
BF16 = jnp.bfloat16

D_MODEL = 1024
D_FF = 2816
EPS = 1e-6
ROPE_THETA = 10000.0
GRID_W = 64
LOG2E = math.log2(math.e)

N_HEAD = 4
DK = 128
DV = 256
CONV_W = 5
CHUNK = 64
ATT_HQ = 8
ATT_HKV = 2
ATT_HD = 128
ATT_GROUP = ATT_HQ // ATT_HKV

C_GQKV = 0
C_GZ = 2048
C_GATE = 3072
C_RQ = 6144
C_RK = 6656
C_RV = 7168
C_RG = 8192
C_AQ = 9216
C_AK = 10240
C_AV = 10496
N_PROJ = 10752

VMEM_LIMIT = 56 * 1024 * 1024

TM_FFN = 512
FF_CHUNKS = ((0, 768), (768, 1536), (1536, 2304), (2304, 2816))
TM_PROJ = 1024
TN_PROJ = 3584
TM_GDN = 256
TR_RET = 256
TM_ATT_PREP = 256
TQ_ATT = 256
TK_ATT = 2048
TM_MERGE = 512


def _cparams(sem):
    return pltpu.CompilerParams(dimension_semantics=sem, vmem_limit_bytes=VMEM_LIMIT)


def _dot(a, b):
    return jnp.dot(a, b, preferred_element_type=F32)


def _dot_nt(a, b):
    return lax.dot_general(a, b, (((1,), (1,)), ((), ())), preferred_element_type=F32)


def _dot_tn(a, b):
    return lax.dot_general(a, b, (((0,), (0,)), ((), ())), preferred_element_type=F32)


def _sigmoid(x):
    return 1.0 / (1.0 + jnp.exp2(x * (-LOG2E)))


def _silu(x):
    return x * _sigmoid(x)


def _rms(x, gain):
    ms = jnp.mean(x * x, axis=-1, keepdims=True)
    return x * lax.rsqrt(ms + EPS) * gain


def _const_spec(shape):
    n = len(shape)
    return pl.BlockSpec(shape, lambda *_: (0,) * n)


def _ffn_kernel(x_ref, g_ref, w1_ref, w3_ref, w2_ref, o_ref):
    x = x_ref[...]
    n = _rms(x, g_ref[...]).astype(BF16)
    acc = jnp.zeros(x.shape, F32)
    for lo, hi in FF_CHUNKS:
        h1 = _dot(n, w1_ref[:, lo:hi])
        h3 = _dot(n, w3_ref[:, lo:hi])
        a = (_silu(h1) * h3).astype(BF16)
        acc = acc + _dot(a, w2_ref[lo:hi, :])
    o_ref[...] = x + 0.5 * acc


def _ffn(x, gain, w1, w3, w2, l):
    t = x.shape[0]
    resident = dict(pipeline_mode=pl.Buffered(1))
    return pl.pallas_call(
        _ffn_kernel,
        grid=(t // TM_FFN,),
        in_specs=[
            pl.BlockSpec((TM_FFN, D_MODEL), lambda i: (i, 0)),
            pl.BlockSpec((1, D_MODEL), lambda i: (0, 0)),
            pl.BlockSpec((None, D_MODEL, D_FF), lambda i: (l, 0, 0), **resident),
            pl.BlockSpec((None, D_MODEL, D_FF), lambda i: (l, 0, 0), **resident),
            pl.BlockSpec((None, D_FF, D_MODEL), lambda i: (l, 0, 0), **resident),
        ],
        out_specs=pl.BlockSpec((TM_FFN, D_MODEL), lambda i: (i, 0)),
        out_shape=jax.ShapeDtypeStruct((t, D_MODEL), F32),
        compiler_params=_cparams(("parallel",)),
        name="ffn",
    )(x, gain, w1, w3, w2)


def _inproj_kernel(x_ref, g_ref, w_ref, ws_ref, o_ref, os_ref, n_sc):
    @pl.when(pl.program_id(1) == 0)
    def _():
        n = _rms(x_ref[...], g_ref[...]).astype(BF16)
        n_sc[...] = n
        os_ref[...] = _dot(n, ws_ref[...])

    o_ref[...] = _dot(n_sc[...], w_ref[...]).astype(BF16)


def _inproj(x, gain, w_main, w_small, l):
    t = x.shape[0]
    return pl.pallas_call(
        _inproj_kernel,
        grid=(t // TM_PROJ, N_PROJ // TN_PROJ),
        in_specs=[
            pl.BlockSpec((TM_PROJ, D_MODEL), lambda i, j: (i, 0)),
            pl.BlockSpec((1, D_MODEL), lambda i, j: (0, 0)),
            pl.BlockSpec((None, D_MODEL, TN_PROJ), lambda i, j: (l, 0, j)),
            pl.BlockSpec((None, D_MODEL, 128), lambda i, j: (l, 0, 0)),
        ],
        out_specs=[
            pl.BlockSpec((TM_PROJ, TN_PROJ), lambda i, j: (i, j)),
            pl.BlockSpec((TM_PROJ, 128), lambda i, j: (i, 0)),
        ],
        out_shape=[
            jax.ShapeDtypeStruct((t, N_PROJ), BF16),
            jax.ShapeDtypeStruct((t, 128), F32),
        ],
        scratch_shapes=[pltpu.VMEM((TM_PROJ, D_MODEL), BF16)],
        compiler_params=_cparams(("parallel", "arbitrary")),
        name="inproj",
    )(x, gain, w_main, w_small)


def _split_hi_lo(x):
    hi = x.astype(BF16)
    lo = (x - hi.astype(F32)).astype(BF16)
    return hi, lo


def _dot01(m01, x):
    hi, lo = _split_hi_lo(x)
    return _dot(m01, hi) + _dot(m01, lo)


def _block_diag(p, mask):
    return jnp.where(mask, jnp.concatenate([p] * 4, axis=0), jnp.zeros((), p.dtype))


def _neumann_inverse(l_mats, eye4, bd_mask):
    qs = [-l for l in l_mats]
    accs = [eye4 + q for q in qs]
    qbs = [q.astype(BF16) for q in qs]
    qs = [_dot(qb, _block_diag(qb, bd_mask)) for qb in qbs]
    for _ in range(4):
        qbs = [q.astype(BF16) for q in qs]
        rs = [_dot(jnp.concatenate([a.astype(BF16), qb], axis=0), _block_diag(qb, bd_mask))
              for a, qb in zip(accs, qbs)]
        accs = [a + r[:CHUNK] for a, r in zip(accs, rs)]
        qs = [r[CHUNK:] for r in rs]
    return [(a + _dot(a.astype(BF16), _block_diag(q.astype(BF16), bd_mask))).astype(BF16)
            for a, q in zip(accs, qs)]


def _gdn_prep_kernel(prev_ref, cur_ref, next_ref, sm_ref, cw_ref, alog_ref, bias_ref,
                     u_ref, w_ref, qkm_ref, qd_ref, kd_ref, gt_ref, ext_sc, *, tiles_per_seq):
    tm = TM_GDN
    i = pl.program_id(0)
    jt = i % tiles_per_seq
    m_prev = (jt != 0).astype(F32)
    m_next = (jt != tiles_per_seq - 1).astype(F32)

    ext_sc[0:16, :] = prev_ref[...].astype(F32) * m_prev
    ext_sc[16:16 + tm, :] = cur_ref[...].astype(F32)
    ext_sc[16 + tm:32 + tm, :] = next_ref[...].astype(F32) * m_next
    nch_all = 2 * N_HEAD * DK + N_HEAD * DV
    nt = tm // 8
    x3 = ext_sc[8:24 + tm, :].reshape(nt + 2, 8, nch_all)
    sub = lax.broadcasted_iota(jnp.int32, (nt, 8, nch_all), 1)
    conv = x3[1:1 + nt] * cw_ref[CONV_W // 2:CONV_W // 2 + 1, :]
    for w in range(CONV_W):
        s = w - CONV_W // 2
        if s == 0:
            continue
        rot = pltpu.roll(x3, (-s) % 8, axis=1)
        if s > 0:
            shifted = jnp.where(sub < 8 - s, rot[1:1 + nt], rot[2:2 + nt])
        else:
            shifted = jnp.where(sub >= -s, rot[1:1 + nt], rot[0:nt])
        conv = conv + shifted * cw_ref[w:w + 1, :]
    act = _silu(conv.reshape(tm, nch_all))

    nqk = N_HEAD * DK
    q_parts, k_parts = [], []
    for h in range(N_HEAD):
        qh = act[:, h * DK:(h + 1) * DK]
        kh = act[:, nqk + h * DK:nqk + (h + 1) * DK]
        q_parts.append(qh * lax.rsqrt(jnp.sum(qh * qh, axis=-1, keepdims=True) + EPS) * (DK ** -0.5))
        k_parts.append(kh * lax.rsqrt(jnp.sum(kh * kh, axis=-1, keepdims=True) + EPS))
    q_n = jnp.concatenate(q_parts, axis=1)
    k_n = jnp.concatenate(k_parts, axis=1)
    v = act[:, 2 * nqk:]

    sm = sm_ref[...]
    beta8 = _sigmoid(sm)
    zb = sm + bias_ref[...]
    softplus = jnp.maximum(zb, 0.0) + jnp.log(1.0 + jnp.exp(-jnp.abs(zb)))
    g8 = -jnp.exp(alog_ref[...]) * softplus

    def expand_mat(width, row0):
        r = lax.broadcasted_iota(jnp.int32, (128, 8 * width), 0)
        c = lax.broadcasted_iota(jnp.int32, (128, 8 * width), 1) // width
        return (r == c + row0).astype(BF16)

    e64b, e128b = expand_mat(64, 0), expand_mat(128, 0)
    e64g, e128g = expand_mat(64, 8), expand_mat(128, 8)
    beta8b = beta8.astype(BF16)
    beta64 = _dot(beta8b, e64b)
    beta128 = _dot(beta8b, e128b)
    g64 = _dot01_rhs(g8, e64g)
    g128 = _dot01_rhs(g8, e128g)

    ri = lax.broadcasted_iota(jnp.int32, (tm, tm), 0)
    ci = lax.broadcasted_iota(jnp.int32, (tm, tm), 1)
    same = (ri // CHUNK) == (ci // CHUNK)
    m_le = (same & (ci <= ri)).astype(BF16)
    m_ge = (same & (ci >= ri)).astype(BF16)
    m_lt = (same & (ci < ri)).astype(BF16)
    m_gt = (same & (ci > ri)).astype(BF16)

    tl = lax.broadcasted_iota(jnp.int32, (tm, 4 * CHUNK), 0) % CHUNK
    jl = lax.broadcasted_iota(jnp.int32, (tm, 4 * CHUNK), 1) % CHUNK
    hw = N_HEAD * CHUNK
    delta_f = _dot01(m_le, jnp.where(tl > jl, g64[:, :hw], 0.0))
    delta_b = _dot01(m_ge, jnp.where(tl < jl, g64[:, hw:], 0.0))
    hk = N_HEAD * DK
    sums_f = _dot01(jnp.concatenate([m_le, m_gt], axis=0), g128[:, :hk])
    sums_b = _dot01(jnp.concatenate([m_ge, m_lt], axis=0), g128[:, hk:])
    gc_f, rest_f = sums_f[:tm], sums_f[tm:]
    gc_b, rest_b = sums_b[:tm], sums_b[tm:]

    il = lax.broadcasted_iota(jnp.int32, (CHUNK, 4 * CHUNK), 0)
    jc = lax.broadcasted_iota(jnp.int32, (CHUNK, 4 * CHUNK), 1) % CHUNK
    eye4 = (il == jc).astype(F32)
    bd_mask = (lax.broadcasted_iota(jnp.int32, (4 * CHUNK, 4 * CHUNK), 0) // CHUNK
               == lax.broadcasted_iota(jnp.int32, (4 * CHUNK, 4 * CHUNK), 1) // CHUNK)
    bdk_mask = (lax.broadcasted_iota(jnp.int32, (4 * CHUNK, hk), 0) // CHUNK
                == lax.broadcasted_iota(jnp.int32, (4 * CHUNK, hk), 1) // DK)

    k_nb = k_n.astype(BF16)
    q_nb = q_n.astype(BF16)
    per_dir = []
    for d, gc, rest, delta, incl, strict in (
            (0, gc_f, rest_f, delta_f, il >= jc, il > jc),
            (1, gc_b, rest_b, delta_b, il <= jc, il < jc)):
        e_gc = jnp.exp(gc)
        qd_ref[d] = (q_n * e_gc).astype(BF16)
        kd_ref[d] = (k_n * jnp.exp(rest)).astype(BF16)
        b128 = beta128[:, d * hk:(d + 1) * hk]
        kbe = (k_n * b128 * e_gc).astype(BF16)
        for c in range(tm // CHUNK):
            r0 = c * CHUNK
            gt_ref[d, c] = jnp.exp(gc[r0:r0 + 1, :] + rest[r0:r0 + 1, :])
        per_dir.append((d, delta, incl, strict, b128, kbe, beta64[:, d * hw:(d + 1) * hw]))

    l_mats, keys = [], []
    for c in range(tm // CHUNK):
        r0 = c * CHUNK
        k_c = k_nb[r0:r0 + CHUNK]
        bdk = jnp.where(bdk_mask, jnp.concatenate([k_c] * 4, axis=0), jnp.zeros((), BF16))
        gq = _dot_nt(jnp.concatenate([k_c, q_nb[r0:r0 + CHUNK]], axis=0), bdk)
        kk, qk = gq[:CHUNK], gq[CHUNK:]
        for d, delta, incl, strict, b128, kbe, b64 in per_dir:
            dm = jnp.where(incl, jnp.exp(delta[r0:r0 + CHUNK]), 0.0)
            l_mats.append(jnp.where(strict, b64[r0:r0 + CHUNK] * kk * dm, 0.0))
            qkm_ref[d, r0:r0 + CHUNK, :] = (qk * dm).astype(BF16)
            keys.append((r0, d, b128, kbe))
    t_alls = _neumann_inverse(l_mats, eye4, bd_mask)
    for (r0, d, b128, kbe), t_all in zip(keys, t_alls):
        for h in range(N_HEAD):
            bh = b128[r0:r0 + CHUNK, h * DK:(h + 1) * DK]
            vh = v[r0:r0 + CHUNK, h * DV:(h + 1) * DV]
            vb = jnp.concatenate([vh[:, :DK] * bh, vh[:, DK:] * bh], axis=1).astype(BF16)
            rhs = jnp.concatenate([vb, kbe[r0:r0 + CHUNK, h * DK:(h + 1) * DK]], axis=1)
            uw = _dot(t_all[:, h * CHUNK:(h + 1) * CHUNK], rhs)
            u_ref[d, r0:r0 + CHUNK, h * DV:(h + 1) * DV] = uw[:, :DV].astype(BF16)
            w_ref[d, r0:r0 + CHUNK, h * DK:(h + 1) * DK] = uw[:, DV:].astype(BF16)


def _dot01_rhs(x, m01):
    hi, lo = _split_hi_lo(x)
    return _dot(hi, m01) + _dot(lo, m01)


def _gdn_prep(proj, small, conv_w, alog_row, bias_row, seq):
    t = proj.shape[0]
    tm = TM_GDN
    tps = seq // tm
    nqkv = 2 * N_HEAD * DK + N_HEAD * DV
    hb = tm // 16
    last16 = t // 16 - 1
    kern = functools.partial(_gdn_prep_kernel, tiles_per_seq=tps)
    return pl.pallas_call(
        kern,
        grid=(t // tm,),
        in_specs=[
            pl.BlockSpec((16, nqkv), lambda i: (jnp.maximum(i * hb - 1, 0), 0)),
            pl.BlockSpec((tm, nqkv), lambda i: (i, 0)),
            pl.BlockSpec((16, nqkv), lambda i: (jnp.minimum((i + 1) * hb, last16), 0)),
            pl.BlockSpec((tm, 128), lambda i: (i, 0)),
            _const_spec((8, nqkv)),
            _const_spec((1, 128)),
            _const_spec((1, 128)),
        ],
        out_specs=[
            pl.BlockSpec((2, tm, N_HEAD * DV), lambda i: (0, i, 0)),
            pl.BlockSpec((2, tm, N_HEAD * DK), lambda i: (0, i, 0)),
            pl.BlockSpec((2, tm, N_HEAD * CHUNK), lambda i: (0, i, 0)),
            pl.BlockSpec((2, tm, N_HEAD * DK), lambda i: (0, i, 0)),
            pl.BlockSpec((2, tm, N_HEAD * DK), lambda i: (0, i, 0)),
            pl.BlockSpec((2, tm // CHUNK, 1, N_HEAD * DK), lambda i: (0, i, 0, 0)),
        ],
        out_shape=[
            jax.ShapeDtypeStruct((2, t, N_HEAD * DV), BF16),
            jax.ShapeDtypeStruct((2, t, N_HEAD * DK), BF16),
            jax.ShapeDtypeStruct((2, t, N_HEAD * CHUNK), BF16),
            jax.ShapeDtypeStruct((2, t, N_HEAD * DK), BF16),
            jax.ShapeDtypeStruct((2, t, N_HEAD * DK), BF16),
            jax.ShapeDtypeStruct((2, t // CHUNK, 1, N_HEAD * DK), F32),
        ],
        scratch_shapes=[pltpu.VMEM((tm + 32, nqkv), F32)],
        compiler_params=_cparams(("parallel",)),
        name="gdn_prep",
    )(proj, proj, proj, small, conv_w, alog_row, bias_row)


def _gdn_scan_kernel(uf_ref, wf_ref, qkf_ref, qdf_ref, kdf_ref, gtf_ref,
                     ub_ref, wb_ref, qkb_ref, qdb_ref, kdb_ref, gtb_ref,
                     of_ref, ob_ref, s_sc, after_step):
    @pl.when(pl.program_id(1) == 0)
    def _():
        s_sc[...] = jnp.zeros(s_sc.shape, F32)

    nch = TM_GDN // CHUNK
    dirs = (
        (0, uf_ref, wf_ref, qkf_ref, qdf_ref, kdf_ref, gtf_ref, of_ref, range(nch)),
        (1, ub_ref, wb_ref, qkb_ref, qdb_ref, kdb_ref, gtb_ref, ob_ref, range(nch - 1, -1, -1)),
    )
    chains = [(dr, h) for dr in dirs for h in range(N_HEAD)]
    states = [s_sc[d * N_HEAD + h] for (d, *_), h in chains]
    for step in range(nch):
        wss = []
        for ((d, u_ref, w_ref, qk_ref, qd_ref, kd_ref, gt_ref, o_ref, order), h), s in zip(chains, states):
            r0 = order[step] * CHUNK
            wq = jnp.concatenate([w_ref[0, r0:r0 + CHUNK, h * DK:(h + 1) * DK],
                                  qd_ref[0, r0:r0 + CHUNK, h * DK:(h + 1) * DK]], axis=0)
            wss.append(_dot(wq, s.astype(BF16)))
        vbs = []
        for ((d, u_ref, w_ref, qk_ref, qd_ref, kd_ref, gt_ref, o_ref, order), h), ws in zip(chains, wss):
            r0 = order[step] * CHUNK
            v_new = u_ref[0, r0:r0 + CHUNK, h * DV:(h + 1) * DV].astype(F32) - ws[:CHUNK]
            vbs.append(v_new.astype(BF16))
        new_states = []
        for ((d, u_ref, w_ref, qk_ref, qd_ref, kd_ref, gt_ref, o_ref, order), h), s, vb in zip(chains, states, vbs):
            c = order[step]
            r0 = c * CHUNK
            gt = gt_ref[0, c, :, h * DK:(h + 1) * DK]
            gtb = jnp.concatenate([gt, gt], axis=1)
            new_states.append(s * gtb + _dot_tn(kd_ref[0, r0:r0 + CHUNK, h * DK:(h + 1) * DK], vb))
        for ((d, u_ref, w_ref, qk_ref, qd_ref, kd_ref, gt_ref, o_ref, order), h), ws, vb in zip(chains, wss, vbs):
            r0 = order[step] * CHUNK
            o = ws[CHUNK:] + _dot(qk_ref[0, r0:r0 + CHUNK, h * CHUNK:(h + 1) * CHUNK], vb)
            o_ref[r0:r0 + CHUNK, h * DV:(h + 1) * DV] = o.astype(BF16)
        states = new_states
        after_step(step)
    for ((d, *_), h), s in zip(chains, states):
        s_sc[d * N_HEAD + h] = s


def _gdn_scan_specs(u, w, qkm, qd, kd, gt, seq):
    tm = TM_GDN
    tps = seq // tm
    nch = tm // CHUNK

    def fwd(width):
        return pl.BlockSpec((1, tm, width), lambda b, j: (0, b * tps + j, 0))

    def bwd(width):
        return pl.BlockSpec((1, tm, width), lambda b, j: (1, b * tps + tps - 1 - j, 0))

    gt_f = pl.BlockSpec((1, nch, 1, N_HEAD * DK), lambda b, j: (0, b * tps + j, 0, 0))
    gt_b = pl.BlockSpec((1, nch, 1, N_HEAD * DK), lambda b, j: (1, b * tps + tps - 1 - j, 0, 0))
    widths = (N_HEAD * DV, N_HEAD * DK, N_HEAD * CHUNK, N_HEAD * DK, N_HEAD * DK)
    in_specs = [fwd(x) for x in widths] + [gt_f] + [bwd(x) for x in widths] + [gt_b]
    return in_specs, (u, w, qkm, qd, kd, gt, u, w, qkm, qd, kd, gt)


def _rope_half(x, cos, sin_signed):
    return x * cos + pltpu.roll(x, 64, axis=1) * sin_signed


def _ret_heads(qf_ref, kf_ref, vf_ref, cosf_ref, sinf_ref,
               qb_ref, kb_ref, vb_ref, cosb_ref, sinb_ref,
               dsum_ref, xif_ref, zf_ref, xib_ref, zb_ref, gch_ref,
               oa_ref, ob_ref, r_sc):
    @pl.when(pl.program_id(1) == 0)
    def _():
        r_sc[...] = jnp.zeros(r_sc.shape, F32)

    scale = DK ** -0.5
    cosf, sinf = cosf_ref[...], sinf_ref[...]
    cosb, sinb = cosb_ref[...], sinb_ref[...]

    def head(h):
        sl = slice(h * DK, (h + 1) * DK)
        vs = slice(h * DV, (h + 1) * DV)
        q = _rope_half(qf_ref[:, sl].astype(F32), cosf, sinf)
        k = _rope_half(kf_ref[:, sl].astype(F32), cosf, sinf) * scale
        v = vf_ref[:, vs]
        s = (_dot_nt(q.astype(BF16), k.astype(BF16)) * dsum_ref[h]).astype(BF16)
        o = _dot(s, v)
        rf = r_sc[h]
        o = o + _dot((q * xif_ref[:, sl]).astype(BF16), rf.astype(BF16))
        gf = gch_ref[0, :, sl]
        r_sc[h] = rf * jnp.concatenate([gf, gf], axis=1) + _dot_tn((k * zf_ref[:, sl]).astype(BF16), v)
        oa_ref[:, vs] = o.astype(BF16)
        q2 = _rope_half(qb_ref[:, sl].astype(F32), cosb, sinb)
        k2 = _rope_half(kb_ref[:, sl].astype(F32), cosb, sinb) * scale
        v2 = vb_ref[:, vs]
        rb = r_sc[N_HEAD + h]
        ob_ref[:, vs] = _dot((q2 * xib_ref[:, sl]).astype(BF16), rb.astype(BF16)).astype(BF16)
        gb = gch_ref[1, :, sl]
        r_sc[N_HEAD + h] = rb * jnp.concatenate([gb, gb], axis=1) + _dot_tn((k2 * zb_ref[:, sl]).astype(BF16), v2)

    return head


def _retention_specs(proj, cos_t, sin_t, dsum, xi_f, zeta_f, xi_b, zeta_b, gch, seq):
    tr = TR_RET
    tps = seq // tr
    hk, hv = N_HEAD * DK, N_HEAD * DV

    def tok(width, col, mirror):
        cb = col // width
        if mirror:
            return pl.BlockSpec((tr, width), lambda b, j: (b * tps + tps - 1 - j, cb))
        return pl.BlockSpec((tr, width), lambda b, j: (b * tps + j, cb))

    def pos(mirror):
        if mirror:
            return pl.BlockSpec((tr, DK), lambda b, j: (tps - 1 - j, 0))
        return pl.BlockSpec((tr, DK), lambda b, j: (j, 0))

    def side(mirror):
        return [tok(hk, C_RQ, mirror), tok(hk, C_RK, mirror), tok(hv, C_RV, mirror), pos(mirror), pos(mirror)]

    in_specs = side(False) + side(True) + [
        _const_spec((N_HEAD, tr, tr)),
        _const_spec((tr, hk)), _const_spec((tr, hk)), _const_spec((tr, hk)), _const_spec((tr, hk)),
        _const_spec((2, 1, hk)),
    ]
    args = (proj, proj, proj, cos_t, sin_t, proj, proj, proj, cos_t, sin_t,
            dsum, xi_f, zeta_f, xi_b, zeta_b, gch)
    return in_specs, args


N_SCAN_IN = 12
N_RET_IN = 16


def _recurrent_kernel(*refs):
    scan_in = refs[:N_SCAN_IN]
    ret_in = refs[N_SCAN_IN:N_SCAN_IN + N_RET_IN]
    gof_ref, gob_ref, roa_ref, rob_ref, s_sc, r_sc = refs[N_SCAN_IN + N_RET_IN:]
    assert TM_GDN // CHUNK == N_HEAD
    ret_head = _ret_heads(*ret_in, roa_ref, rob_ref, r_sc)
    _gdn_scan_kernel(*scan_in, gof_ref, gob_ref, s_sc, ret_head)


def _recurrences(scan_args, ret_args, nseq, seq):
    assert TM_GDN == TR_RET
    tm = TM_GDN
    tps = seq // tm
    hv = N_HEAD * DV
    scan_specs, scan_ops = _gdn_scan_specs(*scan_args, seq)
    ret_specs, ret_ops = _retention_specs(*ret_args, seq)
    assert len(scan_specs) == N_SCAN_IN and len(ret_specs) == N_RET_IN
    t = scan_ops[0].shape[1]
    fwd_out = pl.BlockSpec((tm, hv), lambda b, j: (b * tps + j, 0))
    bwd_out = pl.BlockSpec((tm, hv), lambda b, j: (b * tps + tps - 1 - j, 0))
    return pl.pallas_call(
        _recurrent_kernel,
        grid=(nseq, tps),
        in_specs=scan_specs + ret_specs,
        out_specs=[fwd_out, bwd_out, fwd_out, bwd_out],
        out_shape=[jax.ShapeDtypeStruct((t, hv), BF16)] * 4,
        scratch_shapes=[pltpu.VMEM((2 * N_HEAD, DK, DV), F32), pltpu.VMEM((2 * N_HEAD, DK, DV), F32)],
        compiler_params=_cparams(("parallel", "arbitrary")),
        name="recurrences",
    )(*scan_ops, *ret_ops)


def _axial_rope(x, cos, sin_signed, swap):
    return x * cos + _dot01_rhs(x, swap) * sin_signed


def _attn_prep_kernel(q_ref, k_ref, v_ref, gq_ref, gk_ref, cos_ref, sin_ref, qo_ref, kt_ref, vx_ref):
    cos, sin = cos_ref[...], sin_ref[...]
    src = lax.broadcasted_iota(jnp.int32, (ATT_HD, ATT_HD), 0)
    dst = lax.broadcasted_iota(jnp.int32, (ATT_HD, ATT_HD), 1)
    fq = (src == jnp.where((dst % 64) < 32, dst + 32, dst - 32)).astype(BF16)
    scale = ATT_HD ** -0.5 * LOG2E
    heads = [(q_ref, h, gq_ref) for h in range(ATT_HQ)] + [(k_ref, h, gk_ref) for h in range(ATT_HKV)]
    xs = [ref[:, h * ATT_HD:(h + 1) * ATT_HD].astype(F32) for ref, h, _ in heads]
    xs = [_rms(x, g[...]) for x, (_, _, g) in zip(xs, heads)]
    xs = [_axial_rope(x, cos, sin, fq) for x in xs]
    for h in range(ATT_HQ):
        qo_ref[:, h * ATT_HD:(h + 1) * ATT_HD] = (xs[h] * scale).astype(BF16)
    for h in range(ATT_HKV):
        sl = slice(h * ATT_HD, (h + 1) * ATT_HD)
        kt_ref[sl, :] = xs[ATT_HQ + h].T.astype(BF16)
        vx_ref[:, 2 * h * ATT_HD:(2 * h + 1) * ATT_HD] = v_ref[:, sl]
        vx_ref[:, (2 * h + 1) * ATT_HD:(2 * h + 2) * ATT_HD] = jnp.ones((v_ref.shape[0], ATT_HD), BF16)


def _attn_prep(proj, gq, gk, cos_t, sin_t, seq):
    t = proj.shape[0]
    tm = TM_ATT_PREP
    tps = seq // tm
    nq, nkv = ATT_HQ * ATT_HD, ATT_HKV * ATT_HD
    return pl.pallas_call(
        _attn_prep_kernel,
        grid=(t // tm,),
        in_specs=[
            pl.BlockSpec((tm, nq), lambda i: (i, C_AQ // nq)),
            pl.BlockSpec((tm, nkv), lambda i: (i, C_AK // nkv)),
            pl.BlockSpec((tm, nkv), lambda i: (i, C_AV // nkv)),
            _const_spec((1, ATT_HD)),
            _const_spec((1, ATT_HD)),
            pl.BlockSpec((tm, ATT_HD), lambda i: (i % tps, 0)),
            pl.BlockSpec((tm, ATT_HD), lambda i: (i % tps, 0)),
        ],
        out_specs=[
            pl.BlockSpec((tm, nq), lambda i: (i, 0)),
            pl.BlockSpec((nkv, tm), lambda i: (0, i)),
            pl.BlockSpec((tm, 2 * nkv), lambda i: (i, 0)),
        ],
        out_shape=[
            jax.ShapeDtypeStruct((t, nq), BF16),
            jax.ShapeDtypeStruct((nkv, t), BF16),
            jax.ShapeDtypeStruct((t, 2 * nkv), BF16),
        ],
        compiler_params=_cparams(("parallel",)),
        name="attn_prep",
    )(proj, proj, proj, gq, gk, cos_t, sin_t)


def _flash_kernel(q_ref, kt_ref, v_ref, o_ref, *, seq):
    tq = TQ_ATT
    tk = min(TK_ATT, seq)
    rows = ATT_GROUP * tq
    nk = seq // tk
    q4 = jnp.concatenate([q_ref[:, h * ATT_HD:(h + 1) * ATT_HD] for h in range(ATT_GROUP)], axis=0)

    def scores(j):
        s = _dot(q4, kt_ref[:, j * tk:(j + 1) * tk])
        return s, jnp.max(s, axis=-1, keepdims=True)

    def update(j, s, smax, m, acc):
        m_new = jnp.maximum(m, smax)
        alpha = jnp.exp2(m - m_new)
        p = jnp.exp2(s - m_new).astype(BF16)
        return m_new, alpha * acc + _dot(p, v_ref[j * tk:(j + 1) * tk, :])

    m = jnp.full((rows, 1), -jnp.inf, F32)
    acc = jnp.zeros((rows, 2 * ATT_HD), F32)
    s, smax = scores(0)
    for j in range(nk):
        if j + 1 < nk:
            s_next, smax_next = scores(j + 1)
        m, acc = update(j, s, smax, m, acc)
        if j + 1 < nk:
            s, smax = s_next, smax_next
    out = acc[:, :ATT_HD] / acc[:, ATT_HD:]
    for h in range(ATT_GROUP):
        o_ref[:, h * ATT_HD:(h + 1) * ATT_HD] = out[h * tq:(h + 1) * tq].astype(BF16)


def _flash(q_r, k_t, v_ext, nseq, seq):
    t = q_r.shape[0]
    tq = TQ_ATT
    nq = seq // tq
    gw = ATT_GROUP * ATT_HD
    return pl.pallas_call(
        functools.partial(_flash_kernel, seq=seq),
        grid=(nseq, ATT_HKV, nq),
        in_specs=[
            pl.BlockSpec((tq, gw), lambda b, g, i: (b * nq + i, g)),
            pl.BlockSpec((ATT_HD, seq), lambda b, g, i: (g, b)),
            pl.BlockSpec((seq, 2 * ATT_HD), lambda b, g, i: (b, g)),
        ],
        out_specs=pl.BlockSpec((tq, gw), lambda b, g, i: (b * nq + i, g)),
        out_shape=jax.ShapeDtypeStruct((t, ATT_HQ * ATT_HD), BF16),
        compiler_params=_cparams(("parallel", "parallel", "arbitrary")),
        name="flash_gqa",
    )(q_r, k_t, v_ext)


def _merge_kernel(x_ref, gof_ref, gob_ref, gz_ref, roa_ref, rob_ref, rg_ref, att_ref,
                  g0_ref, g1_ref, g2_ref, gn_ref, rn_ref, wg_ref, wr_ref, wa_ref, wo_ref, o_ref):
    go = gof_ref[...].astype(F32) + gob_ref[...].astype(F32)
    ro = roa_ref[...].astype(F32) + rob_ref[...].astype(F32)
    gz = gz_ref[...].astype(F32)
    rg = rg_ref[...].astype(F32)
    ga_parts, rb_parts = [], []
    for h in range(N_HEAD):
        vs = slice(h * DV, (h + 1) * DV)
        gh = go[:, vs]
        gh = gh * lax.rsqrt(jnp.mean(gh * gh, axis=-1, keepdims=True) + EPS) * gn_ref[...]
        ga_parts.append((gh * _silu(gz[:, vs])).astype(BF16))
        rh = ro[:, vs]
        mu = jnp.mean(rh, axis=-1, keepdims=True)
        cen = rh - mu
        var = jnp.mean(cen * cen, axis=-1, keepdims=True)
        rh = cen * lax.rsqrt(var + EPS) * rn_ref[:, vs]
        rb_parts.append((_silu(rg[:, vs]) * rh).astype(BF16))
    branch_a = _dot(jnp.concatenate(ga_parts, axis=1), wg_ref[...])
    branch_b = _dot(jnp.concatenate(rb_parts, axis=1), wr_ref[...])
    branch_c = _dot(att_ref[...], wa_ref[...])
    merged = (_sigmoid(g0_ref[...].astype(F32)) * branch_a
              + _sigmoid(g1_ref[...].astype(F32)) * branch_b
              + _sigmoid(g2_ref[...].astype(F32)) * branch_c)
    o_ref[...] = x_ref[...] + _dot(merged.astype(BF16), wo_ref[...])


def _merge(x, go_f, go_b, ro_a, ro_b, att, proj, gdn_norm, ret_norm, wg, wr, wa, wo, l):
    t = x.shape[0]
    tm = TM_MERGE
    d = D_MODEL

    def tok(col=0):
        return pl.BlockSpec((tm, d), lambda i: (i, col // d))

    return pl.pallas_call(
        _merge_kernel,
        grid=(t // tm,),
        in_specs=[
            tok(), tok(), tok(), tok(C_GZ), tok(), tok(), tok(C_RG), tok(),
            tok(C_GATE), tok(C_GATE + d), tok(C_GATE + 2 * d),
            _const_spec((1, DV)), _const_spec((1, d)),
        ] + [pl.BlockSpec((None, d, d), lambda i: (l, 0, 0), pipeline_mode=pl.Buffered(1))] * 4,
        out_specs=tok(),
        out_shape=jax.ShapeDtypeStruct((t, d), F32),
        compiler_params=_cparams(("parallel",)),
        name="merge",
    )(x, go_f, go_b, proj, ro_a, ro_b, proj, att, proj, proj, proj,
      gdn_norm, ret_norm, wg, wr, wa, wo)


def _rope_tables(seq):
    f32 = np.float32
    pos = np.arange(seq, dtype=f32)
    inv = (f32(ROPE_THETA) ** (-np.arange(0, DK, 2, dtype=f32) / f32(DK))).astype(f32)
    ang = pos[:, None] * inv[None, :]
    c, s = np.cos(ang), np.sin(ang)
    ret_cos = np.concatenate([c, c], axis=1)
    ret_sin = np.concatenate([-s, s], axis=1)
    half = ATT_HD // 2
    inv_a = (f32(ROPE_THETA) ** (-np.arange(0, half, 2, dtype=f32) / f32(half))).astype(f32)
    rows = (np.arange(seq) // GRID_W).astype(f32)
    cols = (np.arange(seq) % GRID_W).astype(f32)
    ar, ac = rows[:, None] * inv_a[None, :], cols[:, None] * inv_a[None, :]
    cr, sr, cc, sc = np.cos(ar), np.sin(ar), np.cos(ac), np.sin(ac)
    att_cos = np.concatenate([cr, cr, cc, cc], axis=1)
    att_sin = np.concatenate([-sr, sr, -sc, sc], axis=1)
    return tuple(jnp.asarray(t, F32) for t in (ret_cos, ret_sin, att_cos, att_sin))


def _ret_tables(decay_logit):
    c = TR_RET
    lg = jax.nn.log_sigmoid(decay_logit.astype(F32))
    idx = jnp.arange(c, dtype=F32)
    rel = idx[:, None] - idx[None, :]
    lf, lb = lg[0][:, None, None], lg[1][:, None, None]
    dsum = (jnp.exp(jnp.where(rel >= 0, rel * lf, -jnp.inf))
            + jnp.exp(jnp.where(rel <= 0, -rel * lb, -jnp.inf)))

    def lanes(tab):
        return jnp.repeat(tab.T, DK, axis=1)

    xi_f = lanes(jnp.exp((idx + 1.0)[None, :] * lg[0][:, None]))
    zeta_f = lanes(jnp.exp((c - 1.0 - idx)[None, :] * lg[0][:, None]))
    xi_b = lanes(jnp.exp((c - idx)[None, :] * lg[1][:, None]))
    zeta_b = lanes(jnp.exp(idx[None, :] * lg[1][:, None]))
    gch = jnp.repeat(jnp.exp(c * lg), DK, axis=1)[:, None, :]
    return dsum, xi_f, zeta_f, xi_b, zeta_b, gch


def _stacked_weights(p):
    w_in = p["w_in"]
    n_gdn = 2 * N_HEAD * DK + 2 * N_HEAD * DV
    rest = w_in[:, :, n_gdn + 16:]
    n_ret = 2 * N_HEAD * DK + 2 * N_HEAD * DV
    n_att = ATT_HQ * ATT_HD + 2 * ATT_HKV * ATT_HD
    w_main = jnp.concatenate(
        [w_in[:, :, :n_gdn], rest[:, :, n_ret + n_att:], rest[:, :, :n_ret + n_att]], axis=2).astype(BF16)
    w_small = jnp.pad(w_in[:, :, n_gdn:n_gdn + 16], ((0, 0), (0, 0), (0, 112))).astype(BF16)
    cast = {k: p[k].astype(BF16) for k in (
        "ffn1_w1", "ffn1_w3", "ffn1_w2", "ffn2_w1", "ffn2_w3", "ffn2_w2",
        "w_branch_gdn", "w_branch_ret", "w_branch_attn", "w_out")}
    return dict(cast, w_main=w_main, w_small=w_small)


def _layer_params(l, p):
    conv_w = jnp.pad(p["gdn_conv"][l].astype(F32), ((0, 8 - CONV_W), (0, 0)))
    alog_row = jnp.pad(p["gdn_A_log"][l].astype(F32).reshape(1, 8), ((0, 0), (8, 112)))
    bias_row = jnp.pad(p["gdn_dt_bias"][l].astype(F32).reshape(1, 8), ((0, 0), (8, 112)))
    return dict(
        ffn1_norm=p["ffn1_norm"][l][None, :], ffn2_norm=p["ffn2_norm"][l][None, :],
        mix_norm=p["mix_norm"][l][None, :],
        conv_w=conv_w, alog_row=alog_row, bias_row=bias_row,
        ret_tabs=_ret_tables(p["ret_decay_logit"][l]),
        gq=p["attn_q_norm"][l][None, :], gk=p["attn_k_norm"][l][None, :],
        gdn_norm=p["gdn_norm"][l][None, :], ret_norm=p["ret_norm"][l][None, :],
    )


def _encoder_layer(x, l, lp, sw, tabs, nseq, seq):
    ret_cos, ret_sin, att_cos, att_sin = tabs
    x = _ffn(x, lp["ffn1_norm"], sw["ffn1_w1"], sw["ffn1_w3"], sw["ffn1_w2"], l)
    proj, small = _inproj(x, lp["mix_norm"], sw["w_main"], sw["w_small"], l)
    u, w, qkm, qd, kd, gt = _gdn_prep(proj, small, lp["conv_w"], lp["alog_row"], lp["bias_row"], seq)
    go_f, go_b, ro_a, ro_b = _recurrences(
        (u, w, qkm, qd, kd, gt), (proj, ret_cos, ret_sin, *lp["ret_tabs"]), nseq, seq)
    q_r, k_t, v_ext = _attn_prep(proj, lp["gq"], lp["gk"], att_cos, att_sin, seq)
    att = _flash(q_r, k_t, v_ext, nseq, seq)
    x = _merge(x, go_f, go_b, ro_a, ro_b, att, proj, lp["gdn_norm"], lp["ret_norm"],
               sw["w_branch_gdn"], sw["w_branch_ret"], sw["w_branch_attn"], sw["w_out"], l)
    return _ffn(x, lp["ffn2_norm"], sw["ffn2_w1"], sw["ffn2_w3"], sw["ffn2_w2"], l)


def _trunk(x3, layers, sw, tabs):
    nseq, seq, d = x3.shape
    x = x3.reshape(nseq * seq, d)
    for l, lp in enumerate(layers):
        x = _encoder_layer(x, l, lp, sw, tabs, nseq, seq)
    return x.reshape(nseq, seq, d)


def kernel(x_prompt, x_sample, ffn1_norm, ffn1_w1, ffn1_w3, ffn1_w2, mix_norm, w_in, gdn_conv, gdn_A_log, gdn_dt_bias, gdn_norm, ret_decay_logit, ret_norm, attn_q_norm, attn_k_norm, w_branch_gdn, w_branch_ret, w_branch_attn, w_out, ffn2_norm, ffn2_w1, ffn2_w3, ffn2_w2):
    p = dict(ffn1_norm=ffn1_norm, ffn1_w1=ffn1_w1, ffn1_w3=ffn1_w3, ffn1_w2=ffn1_w2, mix_norm=mix_norm,
             w_in=w_in, gdn_conv=gdn_conv, gdn_A_log=gdn_A_log, gdn_dt_bias=gdn_dt_bias, gdn_norm=gdn_norm,
             ret_decay_logit=ret_decay_logit, ret_norm=ret_norm, attn_q_norm=attn_q_norm,
             attn_k_norm=attn_k_norm, w_branch_gdn=w_branch_gdn, w_branch_ret=w_branch_ret,
             w_branch_attn=w_branch_attn, w_out=w_out, ffn2_norm=ffn2_norm, ffn2_w1=ffn2_w1,
             ffn2_w3=ffn2_w3, ffn2_w2=ffn2_w2)
    depth = w_in.shape[0]
    layers = [_layer_params(l, p) for l in range(depth)]
    sw = _stacked_weights(p)
    assert x_prompt.shape[1] == x_sample.shape[1]
    tabs = _rope_tables(x_prompt.shape[1])
    return (_trunk(x_prompt, layers, sw, tabs), _trunk(x_sample, layers, sw, tabs))
```

```python
import functools
import math

import jax
import jax.numpy as jnp
import numpy as np
from jax import lax
from jax.experimental import pallas as pl
from jax.experimental.pallas import tpu as pltpu

F32 = jnp.float32
BF16 = jnp.bfloat16

D_MODEL = 1024
D_FF = 2816
EPS = 1e-6
ROPE_THETA = 10000.0
GRID_W = 64
LOG2E = math.log2(math.e)

N_HEAD = 4
DK = 128
DV = 256
CONV_W = 5
CHUNK = 64
ATT_HQ = 8
ATT_HKV = 2
ATT_HD = 128
ATT_GROUP = ATT_HQ // ATT_HKV

C_GQKV = 0
C_GZ = 2048
C_GATE = 3072
C_RQ = 6144
C_RK = 6656
C_RV = 7168
C_RG = 8192
C_AQ = 9216
C_AK = 10240
C_AV = 10496
N_PROJ = 10752

VMEM_LIMIT = 56 * 1024 * 1024

TM_FFN = 512
FF_CHUNKS = ((0, 768), (768, 1536), (1536, 2304), (2304, 2816))
TM_PROJ = 1024
TN_PROJ = 3584
TM_GDN = 256
TR_RET = 256
TM_ATT_PREP = 1024
TQ_ATT = 256
TK_ATT = 2048
TM_MERGE = 512


def _cparams(sem):
    return pltpu.CompilerParams(dimension_semantics=sem, vmem_limit_bytes=VMEM_LIMIT)


def _dot(a, b):
    return jnp.dot(a, b, preferred_element_type=F32)


def _dot_nt(a, b):
    return lax.dot_general(a, b, (((1,), (1,)), ((), ())), preferred_element_type=F32)


def _dot_tn(a, b):
    return lax.dot_general(a, b, (((0,), (0,)), ((), ())), preferred_element_type=F32)


def _sigmoid(x):
    return 1.0 / (1.0 + jnp.exp2(x * (-LOG2E)))


def _silu(x):
    return x * _sigmoid(x)


def _rms(x, gain):
    ms = jnp.mean(x * x, axis=-1, keepdims=True)
    return x * lax.rsqrt(ms + EPS) * gain


def _const_spec(shape):
    n = len(shape)
    return pl.BlockSpec(shape, lambda *_: (0,) * n)


def _ffn_kernel(x_ref, g_ref, w1_ref, w3_ref, w2_ref, o_ref):
    x = x_ref[...]
    n = _rms(x, g_ref[...]).astype(BF16)
    acc = jnp.zeros(x.shape, F32)
    for lo, hi in FF_CHUNKS:
        h1 = _dot(n, w1_ref[:, lo:hi])
        h3 = _dot(n, w3_ref[:, lo:hi])
        a = (_silu(h1) * h3).astype(BF16)
        acc = acc + _dot(a, w2_ref[lo:hi, :])
    o_ref[...] = x + 0.5 * acc


def _ffn(x, gain, w1, w3, w2, l):
    t = x.shape[0]
    resident = dict(pipeline_mode=pl.Buffered(1))
    return pl.pallas_call(
        _ffn_kernel,
        grid=(t // TM_FFN,),
        in_specs=[
            pl.BlockSpec((TM_FFN, D_MODEL), lambda i: (i, 0)),
            pl.BlockSpec((1, D_MODEL), lambda i: (0, 0)),
            pl.BlockSpec((None, D_MODEL, D_FF), lambda i: (l, 0, 0), **resident),
            pl.BlockSpec((None, D_MODEL, D_FF), lambda i: (l, 0, 0), **resident),
            pl.BlockSpec((None, D_FF, D_MODEL), lambda i: (l, 0, 0), **resident),
        ],
        out_specs=pl.BlockSpec((TM_FFN, D_MODEL), lambda i: (i, 0)),
        out_shape=jax.ShapeDtypeStruct((t, D_MODEL), F32),
        compiler_params=_cparams(("parallel",)),
        name="ffn",
    )(x, gain, w1, w3, w2)


def _inproj_kernel(x_ref, g_ref, w_ref, ws_ref, o_ref, os_ref, n_sc):
    @pl.when(pl.program_id(1) == 0)
    def _():
        n = _rms(x_ref[...], g_ref[...]).astype(BF16)
        n_sc[...] = n
        os_ref[...] = _dot(n, ws_ref[...])

    o_ref[...] = _dot(n_sc[...], w_ref[...]).astype(BF16)


def _inproj(x, gain, w_main, w_small, l):
    t = x.shape[0]
    return pl.pallas_call(
        _inproj_kernel,
        grid=(t // TM_PROJ, N_PROJ // TN_PROJ),
        in_specs=[
            pl.BlockSpec((TM_PROJ, D_MODEL), lambda i, j: (i, 0)),
            pl.BlockSpec((1, D_MODEL), lambda i, j: (0, 0)),
            pl.BlockSpec((None, D_MODEL, TN_PROJ), lambda i, j: (l, 0, j)),
            pl.BlockSpec((None, D_MODEL, 128), lambda i, j: (l, 0, 0)),
        ],
        out_specs=[
            pl.BlockSpec((TM_PROJ, TN_PROJ), lambda i, j: (i, j)),
            pl.BlockSpec((TM_PROJ, 128), lambda i, j: (i, 0)),
        ],
        out_shape=[
            jax.ShapeDtypeStruct((t, N_PROJ), BF16),
            jax.ShapeDtypeStruct((t, 128), F32),
        ],
        scratch_shapes=[pltpu.VMEM((TM_PROJ, D_MODEL), BF16)],
        compiler_params=_cparams(("parallel", "arbitrary")),
        name="inproj",
    )(x, gain, w_main, w_small)


def _split_hi_lo(x):
    hi = x.astype(BF16)
    lo = (x - hi.astype(F32)).astype(BF16)
    return hi, lo


def _dot01(m01, x):
    hi, lo = _split_hi_lo(x)
    return _dot(m01, hi) + _dot(m01, lo)


def _block_diag(p, mask):
    return jnp.where(mask, jnp.concatenate([p] * 4, axis=0), jnp.zeros((), p.dtype))


def _neumann_inverse(l_mats, eye4, bd_mask):
    qs = [-l for l in l_mats]
    accs = [eye4 + q for q in qs]
    qbs = [q.astype(BF16) for q in qs]
    qs = [_dot(qb, _block_diag(qb, bd_mask)) for qb in qbs]
    for _ in range(4):
        qbs = [q.astype(BF16) for q in qs]
        rs = [_dot(jnp.concatenate([a.astype(BF16), qb], axis=0), _block_diag(qb, bd_mask))
              for a, qb in zip(accs, qbs)]
        accs = [a + r[:CHUNK] for a, r in zip(accs, rs)]
        qs = [r[CHUNK:] for r in rs]
    return [(a + _dot(a.astype(BF16), _block_diag(q.astype(BF16), bd_mask))).astype(BF16)
            for a, q in zip(accs, qs)]


def _gdn_prep_kernel(prev_ref, cur_ref, next_ref, sm_ref, cw_ref, alog_ref, bias_ref,
                     u_ref, w_ref, qkm_ref, qd_ref, kd_ref, gt_ref, ext_sc, *, tiles_per_seq):
    tm = TM_GDN
    i = pl.program_id(0)
    jt = i % tiles_per_seq
    m_prev = (jt != 0).astype(F32)
    m_next = (jt != tiles_per_seq - 1).astype(F32)

    ext_sc[0:16, :] = prev_ref[...].astype(F32) * m_prev
    ext_sc[16:16 + tm, :] = cur_ref[...].astype(F32)
    ext_sc[16 + tm:32 + tm, :] = next_ref[...].astype(F32) * m_next
    nch_all = 2 * N_HEAD * DK + N_HEAD * DV
    nt = tm // 8
    x3 = ext_sc[8:24 + tm, :].reshape(nt + 2, 8, nch_all)
    sub = lax.broadcasted_iota(jnp.int32, (nt, 8, nch_all), 1)
    conv = x3[1:1 + nt] * cw_ref[CONV_W // 2:CONV_W // 2 + 1, :]
    for w in range(CONV_W):
        s = w - CONV_W // 2
        if s == 0:
            continue
        rot = pltpu.roll(x3, (-s) % 8, axis=1)
        if s > 0:
            shifted = jnp.where(sub < 8 - s, rot[1:1 + nt], rot[2:2 + nt])
        else:
            shifted = jnp.where(sub >= -s, rot[1:1 + nt], rot[0:nt])
        conv = conv + shifted * cw_ref[w:w + 1, :]
    act = _silu(conv.reshape(tm, nch_all))

    nqk = N_HEAD * DK
    q_parts, k_parts = [], []
    for h in range(N_HEAD):
        qh = act[:, h * DK:(h + 1) * DK]
        kh = act[:, nqk + h * DK:nqk + (h + 1) * DK]
        q_parts.append(qh * lax.rsqrt(jnp.sum(qh * qh, axis=-1, keepdims=True) + EPS) * (DK ** -0.5))
        k_parts.append(kh * lax.rsqrt(jnp.sum(kh * kh, axis=-1, keepdims=True) + EPS))
    q_n = jnp.concatenate(q_parts, axis=1)
    k_n = jnp.concatenate(k_parts, axis=1)
    v = act[:, 2 * nqk:]

    sm = sm_ref[...]
    beta8 = _sigmoid(sm)
    zb = sm + bias_ref[...]
    softplus = jnp.maximum(zb, 0.0) + jnp.log(1.0 + jnp.exp(-jnp.abs(zb)))
    g8 = -jnp.exp(alog_ref[...]) * softplus

    def expand_mat(width, row0):
        r = lax.broadcasted_iota(jnp.int32, (128, 8 * width), 0)
        c = lax.broadcasted_iota(jnp.int32, (128, 8 * width), 1) // width
        return (r == c + row0).astype(BF16)

    e64b, e128b = expand_mat(64, 0), expand_mat(128, 0)
    e64g, e128g = expand_mat(64, 8), expand_mat(128, 8)
    beta8b = beta8.astype(BF16)
    beta64 = _dot(beta8b, e64b)
    beta128 = _dot(beta8b, e128b)
    g64 = _dot01_rhs(g8, e64g)
    g128 = _dot01_rhs(g8, e128g)

    ri = lax.broadcasted_iota(jnp.int32, (tm, tm), 0)
    ci = lax.broadcasted_iota(jnp.int32, (tm, tm), 1)
    same = (ri // CHUNK) == (ci // CHUNK)
    m_le = (same & (ci <= ri)).astype(BF16)
    m_ge = (same & (ci >= ri)).astype(BF16)
    m_lt = (same & (ci < ri)).astype(BF16)
    m_gt = (same & (ci > ri)).astype(BF16)

    tl = lax.broadcasted_iota(jnp.int32, (tm, 4 * CHUNK), 0) % CHUNK
    jl = lax.broadcasted_iota(jnp.int32, (tm, 4 * CHUNK), 1) % CHUNK
    hw = N_HEAD * CHUNK
    delta_f = _dot01(m_le, jnp.where(tl > jl, g64[:, :hw], 0.0))
    delta_b = _dot01(m_ge, jnp.where(tl < jl, g64[:, hw:], 0.0))
    hk = N_HEAD * DK
    sums_f = _dot01(jnp.concatenate([m_le, m_gt], axis=0), g128[:, :hk])
    sums_b = _dot01(jnp.concatenate([m_ge, m_lt], axis=0), g128[:, hk:])
    gc_f, rest_f = sums_f[:tm], sums_f[tm:]
    gc_b, rest_b = sums_b[:tm], sums_b[tm:]

    il = lax.broadcasted_iota(jnp.int32, (CHUNK, 4 * CHUNK), 0)
    jc = lax.broadcasted_iota(jnp.int32, (CHUNK, 4 * CHUNK), 1) % CHUNK
    eye4 = (il == jc).astype(F32)
    bd_mask = (lax.broadcasted_iota(jnp.int32, (4 * CHUNK, 4 * CHUNK), 0) // CHUNK
               == lax.broadcasted_iota(jnp.int32, (4 * CHUNK, 4 * CHUNK), 1) // CHUNK)
    bdk_mask = (lax.broadcasted_iota(jnp.int32, (4 * CHUNK, hk), 0) // CHUNK
                == lax.broadcasted_iota(jnp.int32, (4 * CHUNK, hk), 1) // DK)

    k_nb = k_n.astype(BF16)
    q_nb = q_n.astype(BF16)
    per_dir = []
    for d, gc, rest, delta, incl, strict in (
            (0, gc_f, rest_f, delta_f, il >= jc, il > jc),
            (1, gc_b, rest_b, delta_b, il <= jc, il < jc)):
        e_gc = jnp.exp(gc)
        qd_ref[d] = (q_n * e_gc).astype(BF16)
        kd_ref[d] = (k_n * jnp.exp(rest)).astype(BF16)
        b128 = beta128[:, d * hk:(d + 1) * hk]
        kbe = (k_n * b128 * e_gc).astype(BF16)
        for c in range(tm // CHUNK):
            r0 = c * CHUNK
            gt_ref[d, c] = jnp.exp(gc[r0:r0 + 1, :] + rest[r0:r0 + 1, :])
        per_dir.append((d, delta, incl, strict, b128, kbe, beta64[:, d * hw:(d + 1) * hw]))

    l_mats, keys = [], []
    for c in range(tm // CHUNK):
        r0 = c * CHUNK
        k_c = k_nb[r0:r0 + CHUNK]
        bdk = jnp.where(bdk_mask, jnp.concatenate([k_c] * 4, axis=0), jnp.zeros((), BF16))
        gq = _dot_nt(jnp.concatenate([k_c, q_nb[r0:r0 + CHUNK]], axis=0), bdk)
        kk, qk = gq[:CHUNK], gq[CHUNK:]
        for d, delta, incl, strict, b128, kbe, b64 in per_dir:
            dm = jnp.where(incl, jnp.exp(delta[r0:r0 + CHUNK]), 0.0)
            l_mats.append(jnp.where(strict, b64[r0:r0 + CHUNK] * kk * dm, 0.0))
            qkm_ref[d, r0:r0 + CHUNK, :] = (qk * dm).astype(BF16)
            keys.append((r0, d, b128, kbe))
    t_alls = _neumann_inverse(l_mats, eye4, bd_mask)
    for (r0, d, b128, kbe), t_all in zip(keys, t_alls):
        for h in range(N_HEAD):
            bh = b128[r0:r0 + CHUNK, h * DK:(h + 1) * DK]
            vh = v[r0:r0 + CHUNK, h * DV:(h + 1) * DV]
            vb = jnp.concatenate([vh[:, :DK] * bh, vh[:, DK:] * bh], axis=1).astype(BF16)
            rhs = jnp.concatenate([vb, kbe[r0:r0 + CHUNK, h * DK:(h + 1) * DK]], axis=1)
            uw = _dot(t_all[:, h * CHUNK:(h + 1) * CHUNK], rhs)
            u_ref[d, r0:r0 + CHUNK, h * DV:(h + 1) * DV] = uw[:, :DV].astype(BF16)
            w_ref[d, r0:r0 + CHUNK, h * DK:(h + 1) * DK] = uw[:, DV:].astype(BF16)


def _dot01_rhs(x, m01):
    hi, lo = _split_hi_lo(x)
    return _dot(hi, m01) + _dot(lo, m01)


def _gdn_prep(proj, small, conv_w, alog_row, bias_row, seq):
    t = proj.shape[0]
    tm = TM_GDN
    tps = seq // tm
    nqkv = 2 * N_HEAD * DK + N_HEAD * DV
    hb = tm // 16
    last16 = t // 16 - 1
    kern = functools.partial(_gdn_prep_kernel, tiles_per_seq=tps)
    return pl.pallas_call(
        kern,
        grid=(t // tm,),
        in_specs=[
            pl.BlockSpec((16, nqkv), lambda i: (jnp.maximum(i * hb - 1, 0), 0)),
            pl.BlockSpec((tm, nqkv), lambda i: (i, 0)),
            pl.BlockSpec((16, nqkv), lambda i: (jnp.minimum((i + 1) * hb, last16), 0)),
            pl.BlockSpec((tm, 128), lambda i: (i, 0)),
            _const_spec((8, nqkv)),
            _const_spec((1, 128)),
            _const_spec((1, 128)),
        ],
        out_specs=[
            pl.BlockSpec((2, tm, N_HEAD * DV), lambda i: (0, i, 0)),
            pl.BlockSpec((2, tm, N_HEAD * DK), lambda i: (0, i, 0)),
            pl.BlockSpec((2, tm, N_HEAD * CHUNK), lambda i: (0, i, 0)),
            pl.BlockSpec((2, tm, N_HEAD * DK), lambda i: (0, i, 0)),
            pl.BlockSpec((2, tm, N_HEAD * DK), lambda i: (0, i, 0)),
            pl.BlockSpec((2, tm // CHUNK, 1, N_HEAD * DK), lambda i: (0, i, 0, 0)),
        ],
        out_shape=[
            jax.ShapeDtypeStruct((2, t, N_HEAD * DV), BF16),
            jax.ShapeDtypeStruct((2, t, N_HEAD * DK), BF16),
            jax.ShapeDtypeStruct((2, t, N_HEAD * CHUNK), BF16),
            jax.ShapeDtypeStruct((2, t, N_HEAD * DK), BF16),
            jax.ShapeDtypeStruct((2, t, N_HEAD * DK), BF16),
            jax.ShapeDtypeStruct((2, t // CHUNK, 1, N_HEAD * DK), F32),
        ],
        scratch_shapes=[pltpu.VMEM((tm + 32, nqkv), F32)],
        compiler_params=_cparams(("parallel",)),
        name="gdn_prep",
    )(proj, proj, proj, small, conv_w, alog_row, bias_row)


def _gdn_scan_kernel(uf_ref, wf_ref, qkf_ref, qdf_ref, kdf_ref, gtf_ref,
                     ub_ref, wb_ref, qkb_ref, qdb_ref, kdb_ref, gtb_ref,
                     of_ref, ob_ref, s_sc, after_step):
    @pl.when(pl.program_id(1) == 0)
    def _():
        s_sc[...] = jnp.zeros(s_sc.shape, F32)

    nch = TM_GDN // CHUNK
    dirs = (
        (0, uf_ref, wf_ref, qkf_ref, qdf_ref, kdf_ref, gtf_ref, of_ref, range(nch)),
        (1, ub_ref, wb_ref, qkb_ref, qdb_ref, kdb_ref, gtb_ref, ob_ref, range(nch - 1, -1, -1)),
    )
    chains = [(dr, h) for dr in dirs for h in range(N_HEAD)]
    states = [s_sc[d * N_HEAD + h] for (d, *_), h in chains]
    for step in range(nch):
        wss = []
        for ((d, u_ref, w_ref, qk_ref, qd_ref, kd_ref, gt_ref, o_ref, order), h), s in zip(chains, states):
            r0 = order[step] * CHUNK
            wq = jnp.concatenate([w_ref[0, r0:r0 + CHUNK, h * DK:(h + 1) * DK],
                                  qd_ref[0, r0:r0 + CHUNK, h * DK:(h + 1) * DK]], axis=0)
            wss.append(_dot(wq, s.astype(BF16)))
        vbs = []
        for ((d, u_ref, w_ref, qk_ref, qd_ref, kd_ref, gt_ref, o_ref, order), h), ws in zip(chains, wss):
            r0 = order[step] * CHUNK
            v_new = u_ref[0, r0:r0 + CHUNK, h * DV:(h + 1) * DV].astype(F32) - ws[:CHUNK]
            vbs.append(v_new.astype(BF16))
        new_states = []
        for ((d, u_ref, w_ref, qk_ref, qd_ref, kd_ref, gt_ref, o_ref, order), h), s, vb in zip(chains, states, vbs):
            c = order[step]
            r0 = c * CHUNK
            gt = gt_ref[0, c, :, h * DK:(h + 1) * DK]
            gtb = jnp.concatenate([gt, gt], axis=1)
            new_states.append(s * gtb + _dot_tn(kd_ref[0, r0:r0 + CHUNK, h * DK:(h + 1) * DK], vb))
        for ((d, u_ref, w_ref, qk_ref, qd_ref, kd_ref, gt_ref, o_ref, order), h), ws, vb in zip(chains, wss, vbs):
            r0 = order[step] * CHUNK
            o = ws[CHUNK:] + _dot(qk_ref[0, r0:r0 + CHUNK, h * CHUNK:(h + 1) * CHUNK], vb)
            o_ref[r0:r0 + CHUNK, h * DV:(h + 1) * DV] = o.astype(BF16)
        states = new_states
        after_step(step)
    for ((d, *_), h), s in zip(chains, states):
        s_sc[d * N_HEAD + h] = s


def _gdn_scan_specs(u, w, qkm, qd, kd, gt, seq):
    tm = TM_GDN
    tps = seq // tm
    nch = tm // CHUNK

    def fwd(width):
        return pl.BlockSpec((1, tm, width), lambda b, j: (0, b * tps + j, 0))

    def bwd(width):
        return pl.BlockSpec((1, tm, width), lambda b, j: (1, b * tps + tps - 1 - j, 0))

    gt_f = pl.BlockSpec((1, nch, 1, N_HEAD * DK), lambda b, j: (0, b * tps + j, 0, 0))
    gt_b = pl.BlockSpec((1, nch, 1, N_HEAD * DK), lambda b, j: (1, b * tps + tps - 1 - j, 0, 0))
    widths = (N_HEAD * DV, N_HEAD * DK, N_HEAD * CHUNK, N_HEAD * DK, N_HEAD * DK)
    in_specs = [fwd(x) for x in widths] + [gt_f] + [bwd(x) for x in widths] + [gt_b]
    return in_specs, (u, w, qkm, qd, kd, gt, u, w, qkm, qd, kd, gt)


def _rope_half(x, cos, sin_signed):
    return x * cos + pltpu.roll(x, 64, axis=1) * sin_signed


def _ret_heads(qf_ref, kf_ref, vf_ref, cosf_ref, sinf_ref,
               qb_ref, kb_ref, vb_ref, cosb_ref, sinb_ref,
               dsum_ref, xif_ref, zf_ref, xib_ref, zb_ref, gch_ref,
               oa_ref, ob_ref, r_sc):
    @pl.when(pl.program_id(1) == 0)
    def _():
        r_sc[...] = jnp.zeros(r_sc.shape, F32)

    scale = DK ** -0.5
    cosf, sinf = cosf_ref[...], sinf_ref[...]
    cosb, sinb = cosb_ref[...], sinb_ref[...]

    def head(h):
        sl = slice(h * DK, (h + 1) * DK)
        vs = slice(h * DV, (h + 1) * DV)
        q = _rope_half(qf_ref[:, sl].astype(F32), cosf, sinf)
        k = _rope_half(kf_ref[:, sl].astype(F32), cosf, sinf) * scale
        v = vf_ref[:, vs]
        s = (_dot_nt(q.astype(BF16), k.astype(BF16)) * dsum_ref[h]).astype(BF16)
        o = _dot(s, v)
        rf = r_sc[h]
        o = o + _dot((q * xif_ref[:, sl]).astype(BF16), rf.astype(BF16))
        gf = gch_ref[0, :, sl]
        r_sc[h] = rf * jnp.concatenate([gf, gf], axis=1) + _dot_tn((k * zf_ref[:, sl]).astype(BF16), v)
        oa_ref[:, vs] = o.astype(BF16)
        q2 = _rope_half(qb_ref[:, sl].astype(F32), cosb, sinb)
        k2 = _rope_half(kb_ref[:, sl].astype(F32), cosb, sinb) * scale
        v2 = vb_ref[:, vs]
        rb = r_sc[N_HEAD + h]
        ob_ref[:, vs] = _dot((q2 * xib_ref[:, sl]).astype(BF16), rb.astype(BF16)).astype(BF16)
        gb = gch_ref[1, :, sl]
        r_sc[N_HEAD + h] = rb * jnp.concatenate([gb, gb], axis=1) + _dot_tn((k2 * zb_ref[:, sl]).astype(BF16), v2)

    return head


def _retention_specs(proj, cos_t, sin_t, dsum, xi_f, zeta_f, xi_b, zeta_b, gch, seq):
    tr = TR_RET
    tps = seq // tr
    hk, hv = N_HEAD * DK, N_HEAD * DV

    def tok(width, col, mirror):
        cb = col // width
        if mirror:
            return pl.BlockSpec((tr, width), lambda b, j: (b * tps + tps - 1 - j, cb))
        return pl.BlockSpec((tr, width), lambda b, j: (b * tps + j, cb))

    def pos(mirror):
        if mirror:
            return pl.BlockSpec((tr, DK), lambda b, j: (tps - 1 - j, 0))
        return pl.BlockSpec((tr, DK), lambda b, j: (j, 0))

    def side(mirror):
        return [tok(hk, C_RQ, mirror), tok(hk, C_RK, mirror), tok(hv, C_RV, mirror), pos(mirror), pos(mirror)]

    in_specs = side(False) + side(True) + [
        _const_spec((N_HEAD, tr, tr)),
        _const_spec((tr, hk)), _const_spec((tr, hk)), _const_spec((tr, hk)), _const_spec((tr, hk)),
        _const_spec((2, 1, hk)),
    ]
    args = (proj, proj, proj, cos_t, sin_t, proj, proj, proj, cos_t, sin_t,
            dsum, xi_f, zeta_f, xi_b, zeta_b, gch)
    return in_specs, args


N_SCAN_IN = 12
N_RET_IN = 16


def _recurrent_kernel(*refs):
    scan_in = refs[:N_SCAN_IN]
    ret_in = refs[N_SCAN_IN:N_SCAN_IN + N_RET_IN]
    gof_ref, gob_ref, roa_ref, rob_ref, s_sc, r_sc = refs[N_SCAN_IN + N_RET_IN:]
    assert TM_GDN // CHUNK == N_HEAD
    ret_head = _ret_heads(*ret_in, roa_ref, rob_ref, r_sc)
    _gdn_scan_kernel(*scan_in, gof_ref, gob_ref, s_sc, ret_head)


def _recurrences(scan_args, ret_args, nseq, seq):
    assert TM_GDN == TR_RET
    tm = TM_GDN
    tps = seq // tm
    hv = N_HEAD * DV
    scan_specs, scan_ops = _gdn_scan_specs(*scan_args, seq)
    ret_specs, ret_ops = _retention_specs(*ret_args, seq)
    assert len(scan_specs) == N_SCAN_IN and len(ret_specs) == N_RET_IN
    t = scan_ops[0].shape[1]
    fwd_out = pl.BlockSpec((tm, hv), lambda b, j: (b * tps + j, 0))
    bwd_out = pl.BlockSpec((tm, hv), lambda b, j: (b * tps + tps - 1 - j, 0))
    return pl.pallas_call(
        _recurrent_kernel,
        grid=(nseq, tps),
        in_specs=scan_specs + ret_specs,
        out_specs=[fwd_out, bwd_out, fwd_out, bwd_out],
        out_shape=[jax.ShapeDtypeStruct((t, hv), BF16)] * 4,
        scratch_shapes=[pltpu.VMEM((2 * N_HEAD, DK, DV), F32), pltpu.VMEM((2 * N_HEAD, DK, DV), F32)],
        compiler_params=_cparams(("parallel", "arbitrary")),
        name="recurrences",
    )(*scan_ops, *ret_ops)


def _axial_rope(x, cos, sin_signed, swap):
    return x * cos + _dot01_rhs(x, swap) * sin_signed


def _attn_prep_kernel(q_ref, k_ref, v_ref, gq_ref, gk_ref, cos_ref, sin_ref, qo_ref, kt_ref, vx_ref):
    cos, sin = cos_ref[...], sin_ref[...]
    src = lax.broadcasted_iota(jnp.int32, (ATT_HD, ATT_HD), 0)
    dst = lax.broadcasted_iota(jnp.int32, (ATT_HD, ATT_HD), 1)
    fq = (src == jnp.where((dst % 64) < 32, dst + 32, dst - 32)).astype(BF16)
    scale = ATT_HD ** -0.5 * LOG2E
    heads = [(q_ref, h, gq_ref) for h in range(ATT_HQ)] + [(k_ref, h, gk_ref) for h in range(ATT_HKV)]
    xs = [ref[:, h * ATT_HD:(h + 1) * ATT_HD].astype(F32) for ref, h, _ in heads]
    xs = [_rms(x, g[...]) for x, (_, _, g) in zip(xs, heads)]
    xs = [_axial_rope(x, cos, sin, fq) for x in xs]
    for h in range(ATT_HQ):
        qo_ref[:, h * ATT_HD:(h + 1) * ATT_HD] = (xs[h] * scale).astype(BF16)
    for h in range(ATT_HKV):
        sl = slice(h * ATT_HD, (h + 1) * ATT_HD)
        kt_ref[sl, :] = xs[ATT_HQ + h].T.astype(BF16)
        vx_ref[:, 2 * h * ATT_HD:(2 * h + 1) * ATT_HD] = v_ref[:, sl]
        vx_ref[:, (2 * h + 1) * ATT_HD:(2 * h + 2) * ATT_HD] = jnp.ones((v_ref.shape[0], ATT_HD), BF16)


def _attn_prep(proj, gq, gk, cos_t, sin_t, seq):
    t = proj.shape[0]
    tm = TM_ATT_PREP
    tps = seq // tm
    nq, nkv = ATT_HQ * ATT_HD, ATT_HKV * ATT_HD
    return pl.pallas_call(
        _attn_prep_kernel,
        grid=(t // tm,),
        in_specs=[
            pl.BlockSpec((tm, nq), lambda i: (i, C_AQ // nq)),
            pl.BlockSpec((tm, nkv), lambda i: (i, C_AK // nkv)),
            pl.BlockSpec((tm, nkv), lambda i: (i, C_AV // nkv)),
            _const_spec((1, ATT_HD)),
            _const_spec((1, ATT_HD)),
            pl.BlockSpec((tm, ATT_HD), lambda i: (i % tps, 0)),
            pl.BlockSpec((tm, ATT_HD), lambda i: (i % tps, 0)),
        ],
        out_specs=[
            pl.BlockSpec((tm, nq), lambda i: (i, 0)),
            pl.BlockSpec((nkv, tm), lambda i: (0, i)),
            pl.BlockSpec((tm, 2 * nkv), lambda i: (i, 0)),
        ],
        out_shape=[
            jax.ShapeDtypeStruct((t, nq), BF16),
            jax.ShapeDtypeStruct((nkv, t), BF16),
            jax.ShapeDtypeStruct((t, 2 * nkv), BF16),
        ],
        compiler_params=_cparams(("parallel",)),
        name="attn_prep",
    )(proj, proj, proj, gq, gk, cos_t, sin_t)


def _flash_kernel(q_ref, kt_ref, v_ref, o_ref, *, seq):
    tq = TQ_ATT
    tk = min(TK_ATT, seq)
    rows = ATT_GROUP * tq
    nk = seq // tk
    q4 = jnp.concatenate([q_ref[:, h * ATT_HD:(h + 1) * ATT_HD] for h in range(ATT_GROUP)], axis=0)

    def scores(j):
        s = _dot(q4, kt_ref[:, j * tk:(j + 1) * tk])
        return s, jnp.max(s, axis=-1, keepdims=True)

    def update(j, s, smax, m, acc):
        m_new = jnp.maximum(m, smax)
        alpha = jnp.exp2(m - m_new)
        p = jnp.exp2(s - m_new).astype(BF16)
        return m_new, alpha * acc + _dot(p, v_ref[j * tk:(j + 1) * tk, :])

    m = jnp.full((rows, 1), -jnp.inf, F32)
    acc = jnp.zeros((rows, 2 * ATT_HD), F32)
    s, smax = scores(0)
    for j in range(nk):
        if j + 1 < nk:
            s_next, smax_next = scores(j + 1)
        m, acc = update(j, s, smax, m, acc)
        if j + 1 < nk:
            s, smax = s_next, smax_next
    out = acc[:, :ATT_HD] / acc[:, ATT_HD:]
    for h in range(ATT_GROUP):
        o_ref[:, h * ATT_HD:(h + 1) * ATT_HD] = out[h * tq:(h + 1) * tq].astype(BF16)


def _flash(q_r, k_t, v_ext, nseq, seq):
    t = q_r.shape[0]
    tq = TQ_ATT
    nq = seq // tq
    gw = ATT_GROUP * ATT_HD
    return pl.pallas_call(
        functools.partial(_flash_kernel, seq=seq),
        grid=(nseq, ATT_HKV, nq),
        in_specs=[
            pl.BlockSpec((tq, gw), lambda b, g, i: (b * nq + i, g)),
            pl.BlockSpec((ATT_HD, seq), lambda b, g, i: (g, b)),
            pl.BlockSpec((seq, 2 * ATT_HD), lambda b, g, i: (b, g)),
        ],
        out_specs=pl.BlockSpec((tq, gw), lambda b, g, i: (b * nq + i, g)),
        out_shape=jax.ShapeDtypeStruct((t, ATT_HQ * ATT_HD), BF16),
        compiler_params=_cparams(("parallel", "parallel", "arbitrary")),
        name="flash_gqa",
    )(q_r, k_t, v_ext)


def _merge_kernel(x_ref, gof_ref, gob_ref, gz_ref, roa_ref, rob_ref, rg_ref, att_ref,
                  g0_ref, g1_ref, g2_ref, gn_ref, rn_ref, wg_ref, wr_ref, wa_ref, wo_ref, o_ref):
    go = gof_ref[...].astype(F32) + gob_ref[...].astype(F32)
    ro = roa_ref[...].astype(F32) + rob_ref[...].astype(F32)
    gz = gz_ref[...].astype(F32)
    rg = rg_ref[...].astype(F32)
    ga_parts, rb_parts = [], []
    for h in range(N_HEAD):
        vs = slice(h * DV, (h + 1) * DV)
        gh = go[:, vs]
        gh = gh * lax.rsqrt(jnp.mean(gh * gh, axis=-1, keepdims=True) + EPS) * gn_ref[...]
        ga_parts.append((gh * _silu(gz[:, vs])).astype(BF16))
        rh = ro[:, vs]
        mu = jnp.mean(rh, axis=-1, keepdims=True)
        cen = rh - mu
        var = jnp.mean(cen * cen, axis=-1, keepdims=True)
        rh = cen * lax.rsqrt(var + EPS) * rn_ref[:, vs]
        rb_parts.append((_silu(rg[:, vs]) * rh).astype(BF16))
    branch_a = _dot(jnp.concatenate(ga_parts, axis=1), wg_ref[...])
    branch_b = _dot(jnp.concatenate(rb_parts, axis=1), wr_ref[...])
    branch_c = _dot(att_ref[...], wa_ref[...])
    merged = (_sigmoid(g0_ref[...].astype(F32)) * branch_a
              + _sigmoid(g1_ref[...].astype(F32)) * branch_b
              + _sigmoid(g2_ref[...].astype(F32)) * branch_c)
    o_ref[...] = x_ref[...] + _dot(merged.astype(BF16), wo_ref[...])


def _merge(x, go_f, go_b, ro_a, ro_b, att, proj, gdn_norm, ret_norm, wg, wr, wa, wo, l):
    t = x.shape[0]
    tm = TM_MERGE
    d = D_MODEL

    def tok(col=0):
        return pl.BlockSpec((tm, d), lambda i: (i, col // d))

    return pl.pallas_call(
        _merge_kernel,
        grid=(t // tm,),
        in_specs=[
            tok(), tok(), tok(), tok(C_GZ), tok(), tok(), tok(C_RG), tok(),
            tok(C_GATE), tok(C_GATE + d), tok(C_GATE + 2 * d),
            _const_spec((1, DV)), _const_spec((1, d)),
        ] + [pl.BlockSpec((None, d, d), lambda i: (l, 0, 0), pipeline_mode=pl.Buffered(1))] * 4,
        out_specs=tok(),
        out_shape=jax.ShapeDtypeStruct((t, d), F32),
        compiler_params=_cparams(("parallel",)),
        name="merge",
    )(x, go_f, go_b, proj, ro_a, ro_b, proj, att, proj, proj, proj,
      gdn_norm, ret_norm, wg, wr, wa, wo)


def _rope_tables(seq):
    f32 = np.float32
    pos = np.arange(seq, dtype=f32)
    inv = (f32(ROPE_THETA) ** (-np.arange(0, DK, 2, dtype=f32) / f32(DK))).astype(f32)
    ang = pos[:, None] * inv[None, :]
    c, s = np.cos(ang), np.sin(ang)
    ret_cos = np.concatenate([c, c], axis=1)
    ret_sin = np.concatenate([-s, s], axis=1)
    half = ATT_HD // 2
    inv_a = (f32(ROPE_THETA) ** (-np.arange(0, half, 2, dtype=f32) / f32(half))).astype(f32)
    rows = (np.arange(seq) // GRID_W).astype(f32)
    cols = (np.arange(seq) % GRID_W).astype(f32)
    ar, ac = rows[:, None] * inv_a[None, :], cols[:, None] * inv_a[None, :]
    cr, sr, cc, sc = np.cos(ar), np.sin(ar), np.cos(ac), np.sin(ac)
    att_cos = np.concatenate([cr, cr, cc, cc], axis=1)
    att_sin = np.concatenate([-sr, sr, -sc, sc], axis=1)
    return tuple(jnp.asarray(t, F32) for t in (ret_cos, ret_sin, att_cos, att_sin))


def _ret_tables(decay_logit):
    c = TR_RET
    lg = jax.nn.log_sigmoid(decay_logit.astype(F32))
    idx = jnp.arange(c, dtype=F32)
    rel = idx[:, None] - idx[None, :]
    lf, lb = lg[0][:, None, None], lg[1][:, None, None]
    dsum = (jnp.exp(jnp.where(rel >= 0, rel * lf, -jnp.inf))
            + jnp.exp(jnp.where(rel <= 0, -rel * lb, -jnp.inf)))

    def lanes(tab):
        return jnp.repeat(tab.T, DK, axis=1)

    xi_f = lanes(jnp.exp((idx + 1.0)[None, :] * lg[0][:, None]))
    zeta_f = lanes(jnp.exp((c - 1.0 - idx)[None, :] * lg[0][:, None]))
    xi_b = lanes(jnp.exp((c - idx)[None, :] * lg[1][:, None]))
    zeta_b = lanes(jnp.exp(idx[None, :] * lg[1][:, None]))
    gch = jnp.repeat(jnp.exp(c * lg), DK, axis=1)[:, None, :]
    return dsum, xi_f, zeta_f, xi_b, zeta_b, gch


def _stacked_weights(p):
    w_in = p["w_in"]
    n_gdn = 2 * N_HEAD * DK + 2 * N_HEAD * DV
    rest = w_in[:, :, n_gdn + 16:]
    n_ret = 2 * N_HEAD * DK + 2 * N_HEAD * DV
    n_att = ATT_HQ * ATT_HD + 2 * ATT_HKV * ATT_HD
    w_main = jnp.concatenate(
        [w_in[:, :, :n_gdn], rest[:, :, n_ret + n_att:], rest[:, :, :n_ret + n_att]], axis=2).astype(BF16)
    w_small = jnp.pad(w_in[:, :, n_gdn:n_gdn + 16], ((0, 0), (0, 0), (0, 112))).astype(BF16)
    cast = {k: p[k].astype(BF16) for k in (
        "ffn1_w1", "ffn1_w3", "ffn1_w2", "ffn2_w1", "ffn2_w3", "ffn2_w2",
        "w_branch_gdn", "w_branch_ret", "w_branch_attn", "w_out")}
    return dict(cast, w_main=w_main, w_small=w_small)


def _layer_params(l, p):
    conv_w = jnp.pad(p["gdn_conv"][l].astype(F32), ((0, 8 - CONV_W), (0, 0)))
    alog_row = jnp.pad(p["gdn_A_log"][l].astype(F32).reshape(1, 8), ((0, 0), (8, 112)))
    bias_row = jnp.pad(p["gdn_dt_bias"][l].astype(F32).reshape(1, 8), ((0, 0), (8, 112)))
    return dict(
        ffn1_norm=p["ffn1_norm"][l][None, :], ffn2_norm=p["ffn2_norm"][l][None, :],
        mix_norm=p["mix_norm"][l][None, :],
        conv_w=conv_w, alog_row=alog_row, bias_row=bias_row,
        ret_tabs=_ret_tables(p["ret_decay_logit"][l]),
        gq=p["attn_q_norm"][l][None, :], gk=p["attn_k_norm"][l][None, :],
        gdn_norm=p["gdn_norm"][l][None, :], ret_norm=p["ret_norm"][l][None, :],
    )


def _encoder_layer(x, l, lp, sw, tabs, nseq, seq):
    ret_cos, ret_sin, att_cos, att_sin = tabs
    x = _ffn(x, lp["ffn1_norm"], sw["ffn1_w1"], sw["ffn1_w3"], sw["ffn1_w2"], l)
    proj, small = _inproj(x, lp["mix_norm"], sw["w_main"], sw["w_small"], l)
    u, w, qkm, qd, kd, gt = _gdn_prep(proj, small, lp["conv_w"], lp["alog_row"], lp["bias_row"], seq)
    go_f, go_b, ro_a, ro_b = _recurrences(
        (u, w, qkm, qd, kd, gt), (proj, ret_cos, ret_sin, *lp["ret_tabs"]), nseq, seq)
    q_r, k_t, v_ext = _attn_prep(proj, lp["gq"], lp["gk"], att_cos, att_sin, seq)
    att = _flash(q_r, k_t, v_ext, nseq, seq)
    x = _merge(x, go_f, go_b, ro_a, ro_b, att, proj, lp["gdn_norm"], lp["ret_norm"],
               sw["w_branch_gdn"], sw["w_branch_ret"], sw["w_branch_attn"], sw["w_out"], l)
    return _ffn(x, lp["ffn2_norm"], sw["ffn2_w1"], sw["ffn2_w3"], sw["ffn2_w2"], l)


def _trunk(x3, layers, sw, tabs):
    nseq, seq, d = x3.shape
    x = x3.reshape(nseq * seq, d)
    for l, lp in enumerate(layers):
        x = _encoder_layer(x, l, lp, sw, tabs, nseq, seq)
    return x.reshape(nseq, seq, d)


def kernel(x_prompt, x_sample, ffn1_norm, ffn1_w1, ffn1_w3, ffn1_w2, mix_norm, w_in, gdn_conv, gdn_A_log, gdn_dt_bias, gdn_norm, ret_decay_logit, ret_norm, attn_q_norm, attn_k_norm, w_branch_gdn, w_branch_ret, w_branch_attn, w_out, ffn2_norm, ffn2_w1, ffn2_w3, ffn2_w2):
    p = dict(ffn1_norm=ffn1_norm, ffn1_w1=ffn1_w1, ffn1_w3=ffn1_w3, ffn1_w2=ffn1_w2, mix_norm=mix_norm,
             w_in=w_in, gdn_conv=gdn_conv, gdn_A_log=gdn_A_log, gdn_dt_bias=gdn_dt_bias, gdn_norm=gdn_norm,
             ret_decay_logit=ret_decay_logit, ret_norm=ret_norm, attn_q_norm=attn_q_norm,
             attn_k_norm=attn_k_norm, w_branch_gdn=w_branch_gdn, w_branch_ret=w_branch_ret,
             w_branch_attn=w_branch_attn, w_out=w_out, ffn2_norm=ffn2_norm, ffn2_w1=ffn2_w1,
             ffn2_w3=ffn2_w3, ffn2_w2=ffn2_w2)
    depth = w_in.shape[0]
    layers = [_layer_params(l, p) for l in range(depth)]
    sw = _stacked_weights(p)
    assert x_prompt.shape[1] == x_sample.shape[1]
    tabs = _rope_tables(x_prompt.shape[1])
    return (_trunk(x_prompt, layers, sw, tabs), _trunk(x_sample, layers, sw, tabs))
```

```python
import functools
import math

import jax
import jax.numpy as jnp
import numpy as np
from jax import lax
from jax.experimental import pallas as pl
from jax.experimental.pallas import tpu as pltpu

F32 = jnp.float32
BF16 = jnp.bfloat16

D_MODEL = 1024
D_FF = 2816
EPS = 1e-6
ROPE_THETA = 10000.0
GRID_W = 64
LOG2E = math.log2(math.e)

N_HEAD = 4
DK = 128
DV = 256
CONV_W = 5
CHUNK = 64
ATT_HQ = 8
ATT_HKV = 2
ATT_HD = 128
ATT_GROUP = ATT_HQ // ATT_HKV

C_GQKV = 0
C_GZ = 2048
C_GATE = 3072
C_RQ = 6144
C_RK = 6656
C_RV = 7168
C_RG = 8192
C_AQ = 9216
C_AK = 10240
C_AV = 10496
N_PROJ = 10752

VMEM_LIMIT = 56 * 1024 * 1024

TM_FFN = 1024
FF_CHUNKS = ((0, 768), (768, 1536), (1536, 2304), (2304, 2816))
TM_PROJ = 1024
TN_PROJ = 3584
TM_GDN = 256
TR_RET = 256
TM_ATT_PREP = 1024
TQ_ATT = 256
TK_ATT = 2048
TM_MERGE = 512


def _cparams(sem):
    return pltpu.CompilerParams(dimension_semantics=sem, vmem_limit_bytes=VMEM_LIMIT)


def _dot(a, b):
    return jnp.dot(a, b, preferred_element_type=F32)


def _dot_nt(a, b):
    return lax.dot_general(a, b, (((1,), (1,)), ((), ())), preferred_element_type=F32)


def _dot_tn(a, b):
    return lax.dot_general(a, b, (((0,), (0,)), ((), ())), preferred_element_type=F32)


def _sigmoid(x):
    return 1.0 / (1.0 + jnp.exp2(x * (-LOG2E)))


def _silu(x):
    return x * _sigmoid(x)


def _rms(x, gain):
    ms = jnp.mean(x * x, axis=-1, keepdims=True)
    return x * lax.rsqrt(ms + EPS) * gain


def _const_spec(shape):
    n = len(shape)
    return pl.BlockSpec(shape, lambda *_: (0,) * n)


def _ffn_kernel(x_ref, g_ref, w1_ref, w3_ref, w2_ref, o_ref):
    x = x_ref[...]
    n = _rms(x, g_ref[...]).astype(BF16)
    acc = jnp.zeros(x.shape, F32)
    for lo, hi in FF_CHUNKS:
        h1 = _dot(n, w1_ref[:, lo:hi])
        h3 = _dot(n, w3_ref[:, lo:hi])
        a = (_silu(h1) * h3).astype(BF16)
        acc = acc + _dot(a, w2_ref[lo:hi, :])
    o_ref[...] = x + 0.5 * acc


def _ffn(x, gain, w1, w3, w2, l):
    t = x.shape[0]
    resident = dict(pipeline_mode=pl.Buffered(1))
    return pl.pallas_call(
        _ffn_kernel,
        grid=(t // TM_FFN,),
        in_specs=[
            pl.BlockSpec((TM_FFN, D_MODEL), lambda i: (i, 0)),
            pl.BlockSpec((1, D_MODEL), lambda i: (0, 0)),
            pl.BlockSpec((None, D_MODEL, D_FF), lambda i: (l, 0, 0), **resident),
            pl.BlockSpec((None, D_MODEL, D_FF), lambda i: (l, 0, 0), **resident),
            pl.BlockSpec((None, D_FF, D_MODEL), lambda i: (l, 0, 0), **resident),
        ],
        out_specs=pl.BlockSpec((TM_FFN, D_MODEL), lambda i: (i, 0)),
        out_shape=jax.ShapeDtypeStruct((t, D_MODEL), F32),
        compiler_params=_cparams(("parallel",)),
        name="ffn",
    )(x, gain, w1, w3, w2)


def _inproj_kernel(x_ref, g_ref, w_ref, ws_ref, o_ref, os_ref, n_sc):
    @pl.when(pl.program_id(1) == 0)
    def _():
        n = _rms(x_ref[...], g_ref[...]).astype(BF16)
        n_sc[...] = n
        os_ref[...] = _dot(n, ws_ref[...])

    o_ref[...] = _dot(n_sc[...], w_ref[...]).astype(BF16)


def _inproj(x, gain, w_main, w_small, l):
    t = x.shape[0]
    return pl.pallas_call(
        _inproj_kernel,
        grid=(t // TM_PROJ, N_PROJ // TN_PROJ),
        in_specs=[
            pl.BlockSpec((TM_PROJ, D_MODEL), lambda i, j: (i, 0)),
            pl.BlockSpec((1, D_MODEL), lambda i, j: (0, 0)),
            pl.BlockSpec((None, D_MODEL, TN_PROJ), lambda i, j: (l, 0, j)),
            pl.BlockSpec((None, D_MODEL, 128), lambda i, j: (l, 0, 0)),
        ],
        out_specs=[
            pl.BlockSpec((TM_PROJ, TN_PROJ), lambda i, j: (i, j)),
            pl.BlockSpec((TM_PROJ, 128), lambda i, j: (i, 0)),
        ],
        out_shape=[
            jax.ShapeDtypeStruct((t, N_PROJ), BF16),
            jax.ShapeDtypeStruct((t, 128), F32),
        ],
        scratch_shapes=[pltpu.VMEM((TM_PROJ, D_MODEL), BF16)],
        compiler_params=_cparams(("parallel", "arbitrary")),
        name="inproj",
    )(x, gain, w_main, w_small)


def _split_hi_lo(x):
    hi = x.astype(BF16)
    lo = (x - hi.astype(F32)).astype(BF16)
    return hi, lo


def _dot01(m01, x):
    hi, lo = _split_hi_lo(x)
    return _dot(m01, hi) + _dot(m01, lo)


def _block_diag(p, mask):
    return jnp.where(mask, jnp.concatenate([p] * 4, axis=0), jnp.zeros((), p.dtype))


def _neumann_inverse(l_mats, eye4, bd_mask):
    qs = [-l for l in l_mats]
    accs = [eye4 + q for q in qs]
    qbs = [q.astype(BF16) for q in qs]
    qs = [_dot(qb, _block_diag(qb, bd_mask)) for qb in qbs]
    for _ in range(4):
        qbs = [q.astype(BF16) for q in qs]
        rs = [_dot(jnp.concatenate([a.astype(BF16), qb], axis=0), _block_diag(qb, bd_mask))
              for a, qb in zip(accs, qbs)]
        accs = [a + r[:CHUNK] for a, r in zip(accs, rs)]
        qs = [r[CHUNK:] for r in rs]
    return [(a + _dot(a.astype(BF16), _block_diag(q.astype(BF16), bd_mask))).astype(BF16)
            for a, q in zip(accs, qs)]


def _gdn_prep_kernel(prev_ref, cur_ref, next_ref, sm_ref, cw_ref, alog_ref, bias_ref,
                     u_ref, w_ref, qkm_ref, qd_ref, kd_ref, gt_ref, ext_sc, *, tiles_per_seq):
    tm = TM_GDN
    i = pl.program_id(0)
    jt = i % tiles_per_seq
    m_prev = (jt != 0).astype(F32)
    m_next = (jt != tiles_per_seq - 1).astype(F32)

    ext_sc[0:16, :] = prev_ref[...].astype(F32) * m_prev
    ext_sc[16:16 + tm, :] = cur_ref[...].astype(F32)
    ext_sc[16 + tm:32 + tm, :] = next_ref[...].astype(F32) * m_next
    nch_all = 2 * N_HEAD * DK + N_HEAD * DV
    nt = tm // 8
    x3 = ext_sc[8:24 + tm, :].reshape(nt + 2, 8, nch_all)
    sub = lax.broadcasted_iota(jnp.int32, (nt, 8, nch_all), 1)
    conv = x3[1:1 + nt] * cw_ref[CONV_W // 2:CONV_W // 2 + 1, :]
    for w in range(CONV_W):
        s = w - CONV_W // 2
        if s == 0:
            continue
        rot = pltpu.roll(x3, (-s) % 8, axis=1)
        if s > 0:
            shifted = jnp.where(sub < 8 - s, rot[1:1 + nt], rot[2:2 + nt])
        else:
            shifted = jnp.where(sub >= -s, rot[1:1 + nt], rot[0:nt])
        conv = conv + shifted * cw_ref[w:w + 1, :]
    act = _silu(conv.reshape(tm, nch_all))

    nqk = N_HEAD * DK
    q_parts, k_parts = [], []
    for h in range(N_HEAD):
        qh = act[:, h * DK:(h + 1) * DK]
        kh = act[:, nqk + h * DK:nqk + (h + 1) * DK]
        q_parts.append(qh * lax.rsqrt(jnp.sum(qh * qh, axis=-1, keepdims=True) + EPS) * (DK ** -0.5))
        k_parts.append(kh * lax.rsqrt(jnp.sum(kh * kh, axis=-1, keepdims=True) + EPS))
    q_n = jnp.concatenate(q_parts, axis=1)
    k_n = jnp.concatenate(k_parts, axis=1)
    v = act[:, 2 * nqk:]

    sm = sm_ref[...]
    beta8 = _sigmoid(sm)
    zb = sm + bias_ref[...]
    softplus = jnp.maximum(zb, 0.0) + jnp.log(1.0 + jnp.exp(-jnp.abs(zb)))
    g8 = -jnp.exp(alog_ref[...]) * softplus

    def expand_mat(width, row0):
        r = lax.broadcasted_iota(jnp.int32, (128, 8 * width), 0)
        c = lax.broadcasted_iota(jnp.int32, (128, 8 * width), 1) // width
        return (r == c + row0).astype(BF16)

    e64b, e128b = expand_mat(64, 0), expand_mat(128, 0)
    e64g, e128g = expand_mat(64, 8), expand_mat(128, 8)
    beta8b = beta8.astype(BF16)
    beta64 = _dot(beta8b, e64b)
    beta128 = _dot(beta8b, e128b)
    g64 = _dot01_rhs(g8, e64g)
    g128 = _dot01_rhs(g8, e128g)

    ri = lax.broadcasted_iota(jnp.int32, (tm, tm), 0)
    ci = lax.broadcasted_iota(jnp.int32, (tm, tm), 1)
    same = (ri // CHUNK) == (ci // CHUNK)
    m_le = (same & (ci <= ri)).astype(BF16)
    m_ge = (same & (ci >= ri)).astype(BF16)
    m_lt = (same & (ci < ri)).astype(BF16)
    m_gt = (same & (ci > ri)).astype(BF16)

    tl = lax.broadcasted_iota(jnp.int32, (tm, 4 * CHUNK), 0) % CHUNK
    jl = lax.broadcasted_iota(jnp.int32, (tm, 4 * CHUNK), 1) % CHUNK
    hw = N_HEAD * CHUNK
    delta_f = _dot01(m_le, jnp.where(tl > jl, g64[:, :hw], 0.0))
    delta_b = _dot01(m_ge, jnp.where(tl < jl, g64[:, hw:], 0.0))
    hk = N_HEAD * DK
    sums_f = _dot01(jnp.concatenate([m_le, m_gt], axis=0), g128[:, :hk])
    sums_b = _dot01(jnp.concatenate([m_ge, m_lt], axis=0), g128[:, hk:])
    gc_f, rest_f = sums_f[:tm], sums_f[tm:]
    gc_b, rest_b = sums_b[:tm], sums_b[tm:]

    il = lax.broadcasted_iota(jnp.int32, (CHUNK, 4 * CHUNK), 0)
    jc = lax.broadcasted_iota(jnp.int32, (CHUNK, 4 * CHUNK), 1) % CHUNK
    eye4 = (il == jc).astype(F32)
    bd_mask = (lax.broadcasted_iota(jnp.int32, (4 * CHUNK, 4 * CHUNK), 0) // CHUNK
               == lax.broadcasted_iota(jnp.int32, (4 * CHUNK, 4 * CHUNK), 1) // CHUNK)
    bdk_mask = (lax.broadcasted_iota(jnp.int32, (4 * CHUNK, hk), 0) // CHUNK
                == lax.broadcasted_iota(jnp.int32, (4 * CHUNK, hk), 1) // DK)

    k_nb = k_n.astype(BF16)
    q_nb = q_n.astype(BF16)
    per_dir = []
    for d, gc, rest, delta, incl, strict in (
            (0, gc_f, rest_f, delta_f, il >= jc, il > jc),
            (1, gc_b, rest_b, delta_b, il <= jc, il < jc)):
        e_gc = jnp.exp(gc)
        qd_ref[d] = (q_n * e_gc).astype(BF16)
        kd_ref[d] = (k_n * jnp.exp(rest)).astype(BF16)
        b128 = beta128[:, d * hk:(d + 1) * hk]
        kbe = (k_n * b128 * e_gc).astype(BF16)
        for c in range(tm // CHUNK):
            r0 = c * CHUNK
            gt_ref[d, c] = jnp.exp(gc[r0:r0 + 1, :] + rest[r0:r0 + 1, :])
        per_dir.append((d, delta, incl, strict, b128, kbe, beta64[:, d * hw:(d + 1) * hw]))

    l_mats, keys = [], []
    for c in range(tm // CHUNK):
        r0 = c * CHUNK
        k_c = k_nb[r0:r0 + CHUNK]
        bdk = jnp.where(bdk_mask, jnp.concatenate([k_c] * 4, axis=0), jnp.zeros((), BF16))
        gq = _dot_nt(jnp.concatenate([k_c, q_nb[r0:r0 + CHUNK]], axis=0), bdk)
        kk, qk = gq[:CHUNK], gq[CHUNK:]
        for d, delta, incl, strict, b128, kbe, b64 in per_dir:
            dm = jnp.where(incl, jnp.exp(delta[r0:r0 + CHUNK]), 0.0)
            l_mats.append(jnp.where(strict, b64[r0:r0 + CHUNK] * kk * dm, 0.0))
            qkm_ref[d, r0:r0 + CHUNK, :] = (qk * dm).astype(BF16)
            keys.append((r0, d, b128, kbe))
    t_alls = _neumann_inverse(l_mats, eye4, bd_mask)
    for (r0, d, b128, kbe), t_all in zip(keys, t_alls):
        for h in range(N_HEAD):
            bh = b128[r0:r0 + CHUNK, h * DK:(h + 1) * DK]
            vh = v[r0:r0 + CHUNK, h * DV:(h + 1) * DV]
            vb = jnp.concatenate([vh[:, :DK] * bh, vh[:, DK:] * bh], axis=1).astype(BF16)
            rhs = jnp.concatenate([vb, kbe[r0:r0 + CHUNK, h * DK:(h + 1) * DK]], axis=1)
            uw = _dot(t_all[:, h * CHUNK:(h + 1) * CHUNK], rhs)
            u_ref[d, r0:r0 + CHUNK, h * DV:(h + 1) * DV] = uw[:, :DV].astype(BF16)
            w_ref[d, r0:r0 + CHUNK, h * DK:(h + 1) * DK] = uw[:, DV:].astype(BF16)


def _dot01_rhs(x, m01):
    hi, lo = _split_hi_lo(x)
    return _dot(hi, m01) + _dot(lo, m01)


def _gdn_prep(proj, small, conv_w, alog_row, bias_row, seq):
    t = proj.shape[0]
    tm = TM_GDN
    tps = seq // tm
    nqkv = 2 * N_HEAD * DK + N_HEAD * DV
    hb = tm // 16
    last16 = t // 16 - 1
    kern = functools.partial(_gdn_prep_kernel, tiles_per_seq=tps)
    return pl.pallas_call(
        kern,
        grid=(t // tm,),
        in_specs=[
            pl.BlockSpec((16, nqkv), lambda i: (jnp.maximum(i * hb - 1, 0), 0)),
            pl.BlockSpec((tm, nqkv), lambda i: (i, 0)),
            pl.BlockSpec((16, nqkv), lambda i: (jnp.minimum((i + 1) * hb, last16), 0)),
            pl.BlockSpec((tm, 128), lambda i: (i, 0)),
            _const_spec((8, nqkv)),
            _const_spec((1, 128)),
            _const_spec((1, 128)),
        ],
        out_specs=[
            pl.BlockSpec((2, tm, N_HEAD * DV), lambda i: (0, i, 0)),
            pl.BlockSpec((2, tm, N_HEAD * DK), lambda i: (0, i, 0)),
            pl.BlockSpec((2, tm, N_HEAD * CHUNK), lambda i: (0, i, 0)),
            pl.BlockSpec((2, tm, N_HEAD * DK), lambda i: (0, i, 0)),
            pl.BlockSpec((2, tm, N_HEAD * DK), lambda i: (0, i, 0)),
            pl.BlockSpec((2, tm // CHUNK, 1, N_HEAD * DK), lambda i: (0, i, 0, 0)),
        ],
        out_shape=[
            jax.ShapeDtypeStruct((2, t, N_HEAD * DV), BF16),
            jax.ShapeDtypeStruct((2, t, N_HEAD * DK), BF16),
            jax.ShapeDtypeStruct((2, t, N_HEAD * CHUNK), BF16),
            jax.ShapeDtypeStruct((2, t, N_HEAD * DK), BF16),
            jax.ShapeDtypeStruct((2, t, N_HEAD * DK), BF16),
            jax.ShapeDtypeStruct((2, t // CHUNK, 1, N_HEAD * DK), F32),
        ],
        scratch_shapes=[pltpu.VMEM((tm + 32, nqkv), F32)],
        compiler_params=_cparams(("parallel",)),
        name="gdn_prep",
    )(proj, proj, proj, small, conv_w, alog_row, bias_row)


def _gdn_scan_kernel(uf_ref, wf_ref, qkf_ref, qdf_ref, kdf_ref, gtf_ref,
                     ub_ref, wb_ref, qkb_ref, qdb_ref, kdb_ref, gtb_ref,
                     of_ref, ob_ref, s_sc, after_step):
    @pl.when(pl.program_id(1) == 0)
    def _():
        s_sc[...] = jnp.zeros(s_sc.shape, F32)

    nch = TM_GDN // CHUNK
    dirs = (
        (0, uf_ref, wf_ref, qkf_ref, qdf_ref, kdf_ref, gtf_ref, of_ref, range(nch)),
        (1, ub_ref, wb_ref, qkb_ref, qdb_ref, kdb_ref, gtb_ref, ob_ref, range(nch - 1, -1, -1)),
    )
    chains = [(dr, h) for dr in dirs for h in range(N_HEAD)]
    states = [s_sc[d * N_HEAD + h] for (d, *_), h in chains]
    for step in range(nch):
        wss = []
        for ((d, u_ref, w_ref, qk_ref, qd_ref, kd_ref, gt_ref, o_ref, order), h), s in zip(chains, states):
            r0 = order[step] * CHUNK
            wq = jnp.concatenate([w_ref[0, r0:r0 + CHUNK, h * DK:(h + 1) * DK],
                                  qd_ref[0, r0:r0 + CHUNK, h * DK:(h + 1) * DK]], axis=0)
            wss.append(_dot(wq, s.astype(BF16)))
        vbs = []
        for ((d, u_ref, w_ref, qk_ref, qd_ref, kd_ref, gt_ref, o_ref, order), h), ws in zip(chains, wss):
            r0 = order[step] * CHUNK
            v_new = u_ref[0, r0:r0 + CHUNK, h * DV:(h + 1) * DV].astype(F32) - ws[:CHUNK]
            vbs.append(v_new.astype(BF16))
        new_states = []
        for ((d, u_ref, w_ref, qk_ref, qd_ref, kd_ref, gt_ref, o_ref, order), h), s, vb in zip(chains, states, vbs):
            c = order[step]
            r0 = c * CHUNK
            gt = gt_ref[0, c, :, h * DK:(h + 1) * DK]
            gtb = jnp.concatenate([gt, gt], axis=1)
            new_states.append(s * gtb + _dot_tn(kd_ref[0, r0:r0 + CHUNK, h * DK:(h + 1) * DK], vb))
        for ((d, u_ref, w_ref, qk_ref, qd_ref, kd_ref, gt_ref, o_ref, order), h), ws, vb in zip(chains, wss, vbs):
            r0 = order[step] * CHUNK
            o = ws[CHUNK:] + _dot(qk_ref[0, r0:r0 + CHUNK, h * CHUNK:(h + 1) * CHUNK], vb)
            o_ref[r0:r0 + CHUNK, h * DV:(h + 1) * DV] = o.astype(BF16)
        states = new_states
        after_step(step)
    for ((d, *_), h), s in zip(chains, states):
        s_sc[d * N_HEAD + h] = s


def _gdn_scan_specs(u, w, qkm, qd, kd, gt, seq):
    tm = TM_GDN
    tps = seq // tm
    nch = tm // CHUNK

    def fwd(width):
        return pl.BlockSpec((1, tm, width), lambda b, j: (0, b * tps + j, 0))

    def bwd(width):
        return pl.BlockSpec((1, tm, width), lambda b, j: (1, b * tps + tps - 1 - j, 0))

    gt_f = pl.BlockSpec((1, nch, 1, N_HEAD * DK), lambda b, j: (0, b * tps + j, 0, 0))
    gt_b = pl.BlockSpec((1, nch, 1, N_HEAD * DK), lambda b, j: (1, b * tps + tps - 1 - j, 0, 0))
    widths = (N_HEAD * DV, N_HEAD * DK, N_HEAD * CHUNK, N_HEAD * DK, N_HEAD * DK)
    in_specs = [fwd(x) for x in widths] + [gt_f] + [bwd(x) for x in widths] + [gt_b]
    return in_specs, (u, w, qkm, qd, kd, gt, u, w, qkm, qd, kd, gt)


def _rope_half(x, cos, sin_signed):
    return x * cos + pltpu.roll(x, 64, axis=1) * sin_signed


def _ret_heads(qf_ref, kf_ref, vf_ref, cosf_ref, sinf_ref,
               qb_ref, kb_ref, vb_ref, cosb_ref, sinb_ref,
               dsum_ref, xif_ref, zf_ref, xib_ref, zb_ref, gch_ref,
               oa_ref, ob_ref, r_sc):
    @pl.when(pl.program_id(1) == 0)
    def _():
        r_sc[...] = jnp.zeros(r_sc.shape, F32)

    scale = DK ** -0.5
    cosf, sinf = cosf_ref[...], sinf_ref[...]
    cosb, sinb = cosb_ref[...], sinb_ref[...]

    def head(h):
        sl = slice(h * DK, (h + 1) * DK)
        vs = slice(h * DV, (h + 1) * DV)
        q = _rope_half(qf_ref[:, sl].astype(F32), cosf, sinf)
        k = _rope_half(kf_ref[:, sl].astype(F32), cosf, sinf) * scale
        v = vf_ref[:, vs]
        s = (_dot_nt(q.astype(BF16), k.astype(BF16)) * dsum_ref[h]).astype(BF16)
        o = _dot(s, v)
        rf = r_sc[h]
        o = o + _dot((q * xif_ref[:, sl]).astype(BF16), rf.astype(BF16))
        gf = gch_ref[0, :, sl]
        r_sc[h] = rf * jnp.concatenate([gf, gf], axis=1) + _dot_tn((k * zf_ref[:, sl]).astype(BF16), v)
        oa_ref[:, vs] = o.astype(BF16)
        q2 = _rope_half(qb_ref[:, sl].astype(F32), cosb, sinb)
        k2 = _rope_half(kb_ref[:, sl].astype(F32), cosb, sinb) * scale
        v2 = vb_ref[:, vs]
        rb = r_sc[N_HEAD + h]
        ob_ref[:, vs] = _dot((q2 * xib_ref[:, sl]).astype(BF16), rb.astype(BF16)).astype(BF16)
        gb = gch_ref[1, :, sl]
        r_sc[N_HEAD + h] = rb * jnp.concatenate([gb, gb], axis=1) + _dot_tn((k2 * zb_ref[:, sl]).astype(BF16), v2)

    return head


def _retention_specs(proj, cos_t, sin_t, dsum, xi_f, zeta_f, xi_b, zeta_b, gch, seq):
    tr = TR_RET
    tps = seq // tr
    hk, hv = N_HEAD * DK, N_HEAD * DV

    def tok(width, col, mirror):
        cb = col // width
        if mirror:
            return pl.BlockSpec((tr, width), lambda b, j: (b * tps + tps - 1 - j, cb))
        return pl.BlockSpec((tr, width), lambda b, j: (b * tps + j, cb))

    def pos(mirror):
        if mirror:
            return pl.BlockSpec((tr, DK), lambda b, j: (tps - 1 - j, 0))
        return pl.BlockSpec((tr, DK), lambda b, j: (j, 0))

    def side(mirror):
        return [tok(hk, C_RQ, mirror), tok(hk, C_RK, mirror), tok(hv, C_RV, mirror), pos(mirror), pos(mirror)]

    in_specs = side(False) + side(True) + [
        _const_spec((N_HEAD, tr, tr)),
        _const_spec((tr, hk)), _const_spec((tr, hk)), _const_spec((tr, hk)), _const_spec((tr, hk)),
        _const_spec((2, 1, hk)),
    ]
    args = (proj, proj, proj, cos_t, sin_t, proj, proj, proj, cos_t, sin_t,
            dsum, xi_f, zeta_f, xi_b, zeta_b, gch)
    return in_specs, args


N_SCAN_IN = 12
N_RET_IN = 16


def _recurrent_kernel(*refs):
    scan_in = refs[:N_SCAN_IN]
    ret_in = refs[N_SCAN_IN:N_SCAN_IN + N_RET_IN]
    gof_ref, gob_ref, roa_ref, rob_ref, s_sc, r_sc = refs[N_SCAN_IN + N_RET_IN:]
    assert TM_GDN // CHUNK == N_HEAD
    ret_head = _ret_heads(*ret_in, roa_ref, rob_ref, r_sc)
    _gdn_scan_kernel(*scan_in, gof_ref, gob_ref, s_sc, ret_head)


def _recurrences(scan_args, ret_args, nseq, seq):
    assert TM_GDN == TR_RET
    tm = TM_GDN
    tps = seq // tm
    hv = N_HEAD * DV
    scan_specs, scan_ops = _gdn_scan_specs(*scan_args, seq)
    ret_specs, ret_ops = _retention_specs(*ret_args, seq)
    assert len(scan_specs) == N_SCAN_IN and len(ret_specs) == N_RET_IN
    t = scan_ops[0].shape[1]
    fwd_out = pl.BlockSpec((tm, hv), lambda b, j: (b * tps + j, 0))
    bwd_out = pl.BlockSpec((tm, hv), lambda b, j: (b * tps + tps - 1 - j, 0))
    return pl.pallas_call(
        _recurrent_kernel,
        grid=(nseq, tps),
        in_specs=scan_specs + ret_specs,
        out_specs=[fwd_out, bwd_out, fwd_out, bwd_out],
        out_shape=[jax.ShapeDtypeStruct((t, hv), BF16)] * 4,
        scratch_shapes=[pltpu.VMEM((2 * N_HEAD, DK, DV), F32), pltpu.VMEM((2 * N_HEAD, DK, DV), F32)],
        compiler_params=_cparams(("parallel", "arbitrary")),
        name="recurrences",
    )(*scan_ops, *ret_ops)


def _axial_rope(x, cos, sin_signed, swap):
    return x * cos + _dot01_rhs(x, swap) * sin_signed


def _attn_prep_kernel(q_ref, k_ref, v_ref, gq_ref, gk_ref, cos_ref, sin_ref, qo_ref, kt_ref, vx_ref):
    cos, sin = cos_ref[...], sin_ref[...]
    src = lax.broadcasted_iota(jnp.int32, (ATT_HD, ATT_HD), 0)
    dst = lax.broadcasted_iota(jnp.int32, (ATT_HD, ATT_HD), 1)
    fq = (src == jnp.where((dst % 64) < 32, dst + 32, dst - 32)).astype(BF16)
    scale = ATT_HD ** -0.5 * LOG2E
    heads = [(q_ref, h, gq_ref) for h in range(ATT_HQ)] + [(k_ref, h, gk_ref) for h in range(ATT_HKV)]
    xs = [ref[:, h * ATT_HD:(h + 1) * ATT_HD].astype(F32) for ref, h, _ in heads]
    xs = [_rms(x, g[...]) for x, (_, _, g) in zip(xs, heads)]
    xs = [_axial_rope(x, cos, sin, fq) for x in xs]
    for h in range(ATT_HQ):
        qo_ref[:, h * ATT_HD:(h + 1) * ATT_HD] = (xs[h] * scale).astype(BF16)
    for h in range(ATT_HKV):
        sl = slice(h * ATT_HD, (h + 1) * ATT_HD)
        kt_ref[sl, :] = xs[ATT_HQ + h].T.astype(BF16)
        vx_ref[:, 2 * h * ATT_HD:(2 * h + 1) * ATT_HD] = v_ref[:, sl]
        vx_ref[:, (2 * h + 1) * ATT_HD:(2 * h + 2) * ATT_HD] = jnp.ones((v_ref.shape[0], ATT_HD), BF16)


def _attn_prep(proj, gq, gk, cos_t, sin_t, seq):
    t = proj.shape[0]
    tm = TM_ATT_PREP
    tps = seq // tm
    nq, nkv = ATT_HQ * ATT_HD, ATT_HKV * ATT_HD
    return pl.pallas_call(
        _attn_prep_kernel,
        grid=(t // tm,),
        in_specs=[
            pl.BlockSpec((tm, nq), lambda i: (i, C_AQ // nq)),
            pl.BlockSpec((tm, nkv), lambda i: (i, C_AK // nkv)),
            pl.BlockSpec((tm, nkv), lambda i: (i, C_AV // nkv)),
            _const_spec((1, ATT_HD)),
            _const_spec((1, ATT_HD)),
            pl.BlockSpec((tm, ATT_HD), lambda i: (i % tps, 0)),
            pl.BlockSpec((tm, ATT_HD), lambda i: (i % tps, 0)),
        ],
        out_specs=[
            pl.BlockSpec((tm, nq), lambda i: (i, 0)),
            pl.BlockSpec((nkv, tm), lambda i: (0, i)),
            pl.BlockSpec((tm, 2 * nkv), lambda i: (i, 0)),
        ],
        out_shape=[
            jax.ShapeDtypeStruct((t, nq), BF16),
            jax.ShapeDtypeStruct((nkv, t), BF16),
            jax.ShapeDtypeStruct((t, 2 * nkv), BF16),
        ],
        compiler_params=_cparams(("parallel",)),
        name="attn_prep",
    )(proj, proj, proj, gq, gk, cos_t, sin_t)


def _flash_kernel(q_ref, kt_ref, v_ref, o_ref, *, seq):
    tq = TQ_ATT
    tk = min(TK_ATT, seq)
    rows = ATT_GROUP * tq
    nk = seq // tk
    q4 = jnp.concatenate([q_ref[:, h * ATT_HD:(h + 1) * ATT_HD] for h in range(ATT_GROUP)], axis=0)

    def scores(j):
        s = _dot(q4, kt_ref[:, j * tk:(j + 1) * tk])
        return s, jnp.max(s, axis=-1, keepdims=True)

    def update(j, s, smax, m, acc):
        m_new = jnp.maximum(m, smax)
        alpha = jnp.exp2(m - m_new)
        p = jnp.exp2(s - m_new).astype(BF16)
        return m_new, alpha * acc + _dot(p, v_ref[j * tk:(j + 1) * tk, :])

    m = jnp.full((rows, 1), -jnp.inf, F32)
    acc = jnp.zeros((rows, 2 * ATT_HD), F32)
    s, smax = scores(0)
    for j in range(nk):
        if j + 1 < nk:
            s_next, smax_next = scores(j + 1)
        m, acc = update(j, s, smax, m, acc)
        if j + 1 < nk:
            s, smax = s_next, smax_next
    out = acc[:, :ATT_HD] / acc[:, ATT_HD:]
    for h in range(ATT_GROUP):
        o_ref[:, h * ATT_HD:(h + 1) * ATT_HD] = out[h * tq:(h + 1) * tq].astype(BF16)


def _flash(q_r, k_t, v_ext, nseq, seq):
    t = q_r.shape[0]
    tq = TQ_ATT
    nq = seq // tq
    gw = ATT_GROUP * ATT_HD
    return pl.pallas_call(
        functools.partial(_flash_kernel, seq=seq),
        grid=(nseq, ATT_HKV, nq),
        in_specs=[
            pl.BlockSpec((tq, gw), lambda b, g, i: (b * nq + i, g)),
            pl.BlockSpec((ATT_HD, seq), lambda b, g, i: (g, b)),
            pl.BlockSpec((seq, 2 * ATT_HD), lambda b, g, i: (b, g)),
        ],
        out_specs=pl.BlockSpec((tq, gw), lambda b, g, i: (b * nq + i, g)),
        out_shape=jax.ShapeDtypeStruct((t, ATT_HQ * ATT_HD), BF16),
        compiler_params=_cparams(("parallel", "parallel", "arbitrary")),
        name="flash_gqa",
    )(q_r, k_t, v_ext)


def _merge_kernel(x_ref, gof_ref, gob_ref, gz_ref, roa_ref, rob_ref, rg_ref, att_ref,
                  g0_ref, g1_ref, g2_ref, gn_ref, rn_ref, wg_ref, wr_ref, wa_ref, wo_ref, o_ref):
    go = gof_ref[...].astype(F32) + gob_ref[...].astype(F32)
    ro = roa_ref[...].astype(F32) + rob_ref[...].astype(F32)
    gz = gz_ref[...].astype(F32)
    rg = rg_ref[...].astype(F32)
    ga_parts, rb_parts = [], []
    for h in range(N_HEAD):
        vs = slice(h * DV, (h + 1) * DV)
        gh = go[:, vs]
        gh = gh * lax.rsqrt(jnp.mean(gh * gh, axis=-1, keepdims=True) + EPS) * gn_ref[...]
        ga_parts.append((gh * _silu(gz[:, vs])).astype(BF16))
        rh = ro[:, vs]
        mu = jnp.mean(rh, axis=-1, keepdims=True)
        cen = rh - mu
        var = jnp.mean(cen * cen, axis=-1, keepdims=True)
        rh = cen * lax.rsqrt(var + EPS) * rn_ref[:, vs]
        rb_parts.append((_silu(rg[:, vs]) * rh).astype(BF16))
    branch_a = _dot(jnp.concatenate(ga_parts, axis=1), wg_ref[...])
    branch_b = _dot(jnp.concatenate(rb_parts, axis=1), wr_ref[...])
    branch_c = _dot(att_ref[...], wa_ref[...])
    merged = (_sigmoid(g0_ref[...].astype(F32)) * branch_a
              + _sigmoid(g1_ref[...].astype(F32)) * branch_b
              + _sigmoid(g2_ref[...].astype(F32)) * branch_c)
    o_ref[...] = x_ref[...] + _dot(merged.astype(BF16), wo_ref[...])


def _merge(x, go_f, go_b, ro_a, ro_b, att, proj, gdn_norm, ret_norm, wg, wr, wa, wo, l):
    t = x.shape[0]
    tm = TM_MERGE
    d = D_MODEL

    def tok(col=0):
        return pl.BlockSpec((tm, d), lambda i: (i, col // d))

    return pl.pallas_call(
        _merge_kernel,
        grid=(t // tm,),
        in_specs=[
            tok(), tok(), tok(), tok(C_GZ), tok(), tok(), tok(C_RG), tok(),
            tok(C_GATE), tok(C_GATE + d), tok(C_GATE + 2 * d),
            _const_spec((1, DV)), _const_spec((1, d)),
        ] + [pl.BlockSpec((None, d, d), lambda i: (l, 0, 0), pipeline_mode=pl.Buffered(1))] * 4,
        out_specs=tok(),
        out_shape=jax.ShapeDtypeStruct((t, d), F32),
        compiler_params=_cparams(("parallel",)),
        name="merge",
    )(x, go_f, go_b, proj, ro_a, ro_b, proj, att, proj, proj, proj,
      gdn_norm, ret_norm, wg, wr, wa, wo)


def _rope_tables(seq):
    f32 = np.float32
    pos = np.arange(seq, dtype=f32)
    inv = (f32(ROPE_THETA) ** (-np.arange(0, DK, 2, dtype=f32) / f32(DK))).astype(f32)
    ang = pos[:, None] * inv[None, :]
    c, s = np.cos(ang), np.sin(ang)
    ret_cos = np.concatenate([c, c], axis=1)
    ret_sin = np.concatenate([-s, s], axis=1)
    half = ATT_HD // 2
    inv_a = (f32(ROPE_THETA) ** (-np.arange(0, half, 2, dtype=f32) / f32(half))).astype(f32)
    rows = (np.arange(seq) // GRID_W).astype(f32)
    cols = (np.arange(seq) % GRID_W).astype(f32)
    ar, ac = rows[:, None] * inv_a[None, :], cols[:, None] * inv_a[None, :]
    cr, sr, cc, sc = np.cos(ar), np.sin(ar), np.cos(ac), np.sin(ac)
    att_cos = np.concatenate([cr, cr, cc, cc], axis=1)
    att_sin = np.concatenate([-sr, sr, -sc, sc], axis=1)
    return tuple(jnp.asarray(t, F32) for t in (ret_cos, ret_sin, att_cos, att_sin))


def _ret_tables(decay_logit):
    c = TR_RET
    lg = jax.nn.log_sigmoid(decay_logit.astype(F32))
    idx = jnp.arange(c, dtype=F32)
    rel = idx[:, None] - idx[None, :]
    lf, lb = lg[0][:, None, None], lg[1][:, None, None]
    dsum = (jnp.exp(jnp.where(rel >= 0, rel * lf, -jnp.inf))
            + jnp.exp(jnp.where(rel <= 0, -rel * lb, -jnp.inf)))

    def lanes(tab):
        return jnp.repeat(tab.T, DK, axis=1)

    xi_f = lanes(jnp.exp((idx + 1.0)[None, :] * lg[0][:, None]))
    zeta_f = lanes(jnp.exp((c - 1.0 - idx)[None, :] * lg[0][:, None]))
    xi_b = lanes(jnp.exp((c - idx)[None, :] * lg[1][:, None]))
    zeta_b = lanes(jnp.exp(idx[None, :] * lg[1][:, None]))
    gch = jnp.repeat(jnp.exp(c * lg), DK, axis=1)[:, None, :]
    return dsum, xi_f, zeta_f, xi_b, zeta_b, gch


def _stacked_weights(p):
    w_in = p["w_in"]
    n_gdn = 2 * N_HEAD * DK + 2 * N_HEAD * DV
    rest = w_in[:, :, n_gdn + 16:]
    n_ret = 2 * N_HEAD * DK + 2 * N_HEAD * DV
    n_att = ATT_HQ * ATT_HD + 2 * ATT_HKV * ATT_HD
    w_main = jnp.concatenate(
        [w_in[:, :, :n_gdn], rest[:, :, n_ret + n_att:], rest[:, :, :n_ret + n_att]], axis=2).astype(BF16)
    w_small = jnp.pad(w_in[:, :, n_gdn:n_gdn + 16], ((0, 0), (0, 0), (0, 112))).astype(BF16)
    cast = {k: p[k].astype(BF16) for k in (
        "ffn1_w1", "ffn1_w3", "ffn1_w2", "ffn2_w1", "ffn2_w3", "ffn2_w2",
        "w_branch_gdn", "w_branch_ret", "w_branch_attn", "w_out")}
    return dict(cast, w_main=w_main, w_small=w_small)


def _layer_params(l, p):
    conv_w = jnp.pad(p["gdn_conv"][l].astype(F32), ((0, 8 - CONV_W), (0, 0)))
    alog_row = jnp.pad(p["gdn_A_log"][l].astype(F32).reshape(1, 8), ((0, 0), (8, 112)))
    bias_row = jnp.pad(p["gdn_dt_bias"][l].astype(F32).reshape(1, 8), ((0, 0), (8, 112)))
    return dict(
        ffn1_norm=p["ffn1_norm"][l][None, :], ffn2_norm=p["ffn2_norm"][l][None, :],
        mix_norm=p["mix_norm"][l][None, :],
        conv_w=conv_w, alog_row=alog_row, bias_row=bias_row,
        ret_tabs=_ret_tables(p["ret_decay_logit"][l]),
        gq=p["attn_q_norm"][l][None, :], gk=p["attn_k_norm"][l][None, :],
        gdn_norm=p["gdn_norm"][l][None, :], ret_norm=p["ret_norm"][l][None, :],
    )


def _encoder_layer(x, l, lp, sw, tabs, nseq, seq):
    ret_cos, ret_sin, att_cos, att_sin = tabs
    x = _ffn(x, lp["ffn1_norm"], sw["ffn1_w1"], sw["ffn1_w3"], sw["ffn1_w2"], l)
    proj, small = _inproj(x, lp["mix_norm"], sw["w_main"], sw["w_small"], l)
    u, w, qkm, qd, kd, gt = _gdn_prep(proj, small, lp["conv_w"], lp["alog_row"], lp["bias_row"], seq)
    go_f, go_b, ro_a, ro_b = _recurrences(
        (u, w, qkm, qd, kd, gt), (proj, ret_cos, ret_sin, *lp["ret_tabs"]), nseq, seq)
    q_r, k_t, v_ext = _attn_prep(proj, lp["gq"], lp["gk"], att_cos, att_sin, seq)
    att = _flash(q_r, k_t, v_ext, nseq, seq)
    x = _merge(x, go_f, go_b, ro_a, ro_b, att, proj, lp["gdn_norm"], lp["ret_norm"],
               sw["w_branch_gdn"], sw["w_branch_ret"], sw["w_branch_attn"], sw["w_out"], l)
    return _ffn(x, lp["ffn2_norm"], sw["ffn2_w1"], sw["ffn2_w3"], sw["ffn2_w2"], l)


def _trunk(x3, layers, sw, tabs):
    nseq, seq, d = x3.shape
    x = x3.reshape(nseq * seq, d)
    for l, lp in enumerate(layers):
        x = _encoder_layer(x, l, lp, sw, tabs, nseq, seq)
    return x.reshape(nseq, seq, d)


def kernel(x_prompt, x_sample, ffn1_norm, ffn1_w1, ffn1_w3, ffn1_w2, mix_norm, w_in, gdn_conv, gdn_A_log, gdn_dt_bias, gdn_norm, ret_decay_logit, ret_norm, attn_q_norm, attn_k_norm, w_branch_gdn, w_branch_ret, w_branch_attn, w_out, ffn2_norm, ffn2_w1, ffn2_w3, ffn2_w2):
    p = dict(ffn1_norm=ffn1_norm, ffn1_w1=ffn1_w1, ffn1_w3=ffn1_w3, ffn1_w2=ffn1_w2, mix_norm=mix_norm,
             w_in=w_in, gdn_conv=gdn_conv, gdn_A_log=gdn_A_log, gdn_dt_bias=gdn_dt_bias, gdn_norm=gdn_norm,
             ret_decay_logit=ret_decay_logit, ret_norm=ret_norm, attn_q_norm=attn_q_norm,
             attn_k_norm=attn_k_norm, w_branch_gdn=w_branch_gdn, w_branch_ret=w_branch_ret,
             w_branch_attn=w_branch_attn, w_out=w_out, ffn2_norm=ffn2_norm, ffn2_w1=ffn2_w1,
             ffn2_w3=ffn2_w3, ffn2_w2=ffn2_w2)
    depth = w_in.shape[0]
    layers = [_layer_params(l, p) for l in range(depth)]
    sw = _stacked_weights(p)
    assert x_prompt.shape[1] == x_sample.shape[1]
    tabs = _rope_tables(x_prompt.shape[1])
    return (_trunk(x_prompt, layers, sw, tabs), _trunk(x_sample, layers, sw, tabs))
```
